```python
import jax
import jax.numpy as jnp
from jax import lax
import numpy as np

D_MODEL = 2048
BATCH = 32
SEQ = 256
DEPTH = 2
DEC_BATCH = 8
DEC_SEQ = 1024
PAST_LEN = 512

GRID_W = 64
ROPE_THETA = 10000.0
HEAD_DIM = 128
MIX_W = D_MODEL
MLA_HEADS = 8
MLA_NOPE = 128
MLA_ROPE = 64
MLA_V = 128
MLA_QK = MLA_NOPE + MLA_ROPE
Q_LORA = 512
KV_LORA = 256
NA_HEADS = 4
NA_WIN_H = 8
NA_WIN_W = 16
NA_QCOLS = 16
NA_BAND_W = 32
NA_NCB = GRID_W // NA_QCOLS
GQA_HEADS = 4
GQA_KV_HEADS = 2
GQA_GROUP = GQA_HEADS // GQA_KV_HEADS
D_FF = 5632
Q_BLOCK = 128
EPS = 1e-6
NEG_INF = -1e30
N_MOD = 9
_S1 = Q_LORA
_S2 = _S1 + KV_LORA
_S3 = _S2 + MLA_ROPE
_S4 = _S3 + 3 * NA_HEADS * HEAD_DIM
_S5 = _S4 + GQA_HEADS * HEAD_DIM
IN_COLS = _S5 + 2 * GQA_KV_HEADS * HEAD_DIM
IN_SPLITS = (_S1, _S2, _S3, _S4, _S5)

kernel_name = 'hybrid_prefix_diffusion_step'


def rms_norm(x, g):
    xf = x.astype(jnp.float32)
    y = xf * lax.rsqrt(jnp.mean(xf * xf, axis=-1, keepdims=True) + EPS)
    return (y * g.astype(jnp.float32)).astype(x.dtype)


def axial_rope(x):
    B, S, H, d = x.shape
    quarter = d // 4
    t = jnp.arange(S)
    pos = jnp.stack([t // GRID_W, t % GRID_W], axis=-1).astype(jnp.float32)
    inv = ROPE_THETA ** (-jnp.arange(quarter, dtype=jnp.float32) / quarter)
    ang = pos[:, :, None] * inv
    cos = jnp.cos(ang)[None, :, None]
    sin = jnp.sin(ang)[None, :, None]
    xr = x.astype(jnp.float32).reshape(B, S, H, 2, 2, quarter)
    x1, x2 = xr[..., 0, :], xr[..., 1, :]
    out = jnp.stack([x1 * cos - x2 * sin, x2 * cos + x1 * sin], axis=-2)
    return out.reshape(B, S, H, d).astype(x.dtype)


def attend(q, k, v):
    B, Sq, Hk, G, Dk = q.shape
    nb = Sq // Q_BLOCK
    scale = Dk ** -0.5
    qb = jnp.moveaxis(q.reshape(B, nb, Q_BLOCK, Hk, G, Dk), 1, 0)

    def one_block(qi):
        s = jnp.einsum('bqhgd,bkhd->bhgqk', qi, k).astype(jnp.float32) * scale
        p = jax.nn.softmax(s, axis=-1).astype(v.dtype)
        return jnp.einsum('bhgqk,bkhe->bqhge', p, v)

    o = lax.map(one_block, qb)
    return jnp.moveaxis(o, 0, 1).reshape(B, Sq, Hk * G, v.shape[-1])


def neighbourhood_attention(q, k, v, k_ctx, v_ctx, rpb):
    B, S, H, D = q.shape
    rows = S // GRID_W
    wh = min(NA_WIN_H, rows)
    nk = wh * NA_BAND_W
    scale = D ** -0.5
    r = jnp.arange(rows)
    row_start = jnp.clip(r - wh // 2, 0, rows - wh)
    key_rows = row_start[:, None] + jnp.arange(wh)
    q_cols = jnp.arange(GRID_W).reshape(NA_NCB, NA_QCOLS)
    band_start = jnp.clip(q_cols[:, 0] - NA_WIN_W // 2, 0, GRID_W - NA_BAND_W)
    key_cols = band_start[:, None] + jnp.arange(NA_BAND_W)
    col_start = jnp.clip(q_cols - NA_WIN_W // 2, 0, GRID_W - NA_WIN_W)
    kc = key_cols[:, None, :]
    valid = (kc >= col_start[:, :, None]) & (kc < col_start[:, :, None] + NA_WIN_W)
    dr = key_rows - r[:, None] + (NA_WIN_H - 1)
    dc = jnp.clip(kc - q_cols[:, :, None] + (NA_WIN_W - 1), 0, 2 * NA_WIN_W - 2)
    bias = rpb[:, dr[:, None, None, :, None], dc[None, :, :, None, :]]
    bias = jnp.where(valid[None, None, :, :, None, :], bias.astype(jnp.float32), NEG_INF)
    bias = bias.reshape(H, rows, NA_NCB, NA_QCOLS, nk).transpose(1, 2, 0, 3, 4)
    ridx = key_rows[:, None, :, None]
    cidx = key_cols[None, :, None, :]
    kb = k.reshape(B, rows, GRID_W, H, D)[:, ridx, cidx].reshape(B, rows, NA_NCB, nk, H, D)
    vb = v.reshape(B, rows, GRID_W, H, D)[:, ridx, cidx].reshape(B, rows, NA_NCB, nk, H, D)
    qg = q.reshape(B, rows, NA_NCB, NA_QCOLS, H, D)
    s_loc = jnp.einsum('brcqhd,brckhd->brchqk', qg, kb).astype(jnp.float32) * scale + bias
    s_ctx = jnp.einsum('brcqhd,bphd->brchqp', qg, k_ctx).astype(jnp.float32) * scale
    p = jax.nn.softmax(jnp.concatenate([s_loc, s_ctx], axis=-1), axis=-1).astype(v.dtype)
    o = (jnp.einsum('brchqk,brckhd->brcqhd', p[..., :nk], vb)
         + jnp.einsum('brchqp,bphd->brcqhd', p[..., nk:], v_ctx))
    return o.reshape(B, S, H * D)


def _mixer_inputs(h, lp):
    B, S, _ = h.shape
    c_q, c_kv, k_rope, na_qkv, g_q, g_kv = jnp.split(h @ lp['w_in'], IN_SPLITS, axis=-1)
    q_mla = (rms_norm(c_q, lp['mla_q_norm']) @ lp['mla_wqb']).reshape(B, S, MLA_HEADS, MLA_QK)
    c_kv = rms_norm(c_kv, lp['mla_kv_norm'])
    na_qkv = na_qkv.reshape(B, S, 3, NA_HEADS, HEAD_DIM)
    g_q = rms_norm(g_q.reshape(B, S, GQA_HEADS, HEAD_DIM), lp['gqa_q_norm'])
    g_kv = g_kv.reshape(B, S, 2, GQA_KV_HEADS, HEAD_DIM)
    g_k = rms_norm(g_kv[:, :, 0], lp['gqa_k_norm'])
    return (q_mla, c_kv, k_rope, na_qkv[:, :, 0], na_qkv[:, :, 1], na_qkv[:, :, 2],
            g_q, g_k, g_kv[:, :, 1])


def _mla_kv(c_kv, k_rope, w_kvb):
    B, S, _ = c_kv.shape
    kv = (c_kv @ w_kvb).reshape(B, S, MLA_HEADS, MLA_NOPE + MLA_V)
    k = jnp.concatenate([kv[..., :MLA_NOPE],
                         jnp.broadcast_to(k_rope[:, :, None, :], (B, S, MLA_HEADS, MLA_ROPE))], axis=-1)
    return k, kv[..., MLA_NOPE:]


def _context_mixer(h, lp):
    B, S, _ = h.shape
    q_mla, c_kv, k_rope, na_q, na_k, na_v, g_q, g_k, g_v = _mixer_inputs(h, lp)
    k_mla, v_mla = _mla_kv(c_kv, k_rope, lp['mla_wkvb'])
    o_a = attend(q_mla[:, :, :, None], k_mla, v_mla).reshape(B, S, -1)
    o_b = attend(na_q[:, :, :, None], na_k, na_v).reshape(B, S, -1)
    o_c = attend(g_q.reshape(B, S, GQA_KV_HEADS, GQA_GROUP, HEAD_DIM), g_k, g_v).reshape(B, S, -1)
    return jnp.concatenate([o_a, o_b, o_c], axis=-1), (c_kv, k_rope, na_k, na_v, g_k, g_v)


def _latent_mixer(h, lp, caches):
    ckv_c, krope_c, nak_c, nav_c, gk_c, gv_c = caches
    B, S, _ = h.shape
    q_mla, c_kv, k_rope, na_q, na_k, na_v, g_q, g_k, g_v = _mixer_inputs(h, lp)
    q_mla = jnp.concatenate([q_mla[..., :MLA_NOPE], axial_rope(q_mla[..., MLA_NOPE:])], axis=-1)
    k_rope = axial_rope(k_rope[:, :, None, :])[:, :, 0]
    k_mla, v_mla = _mla_kv(jnp.concatenate([ckv_c, c_kv], axis=1),
                           jnp.concatenate([krope_c, k_rope], axis=1), lp['mla_wkvb'])
    o_a = attend(q_mla[:, :, :, None], k_mla, v_mla).reshape(B, S, -1)
    o_b = neighbourhood_attention(na_q, na_k, na_v, nak_c, nav_c, lp['na_rpb'])
    g_q = axial_rope(g_q)
    g_k = axial_rope(g_k)
    o_c = attend(g_q.reshape(B, S, GQA_KV_HEADS, GQA_GROUP, HEAD_DIM),
                 jnp.concatenate([gk_c, g_k], axis=1),
                 jnp.concatenate([gv_c, g_v], axis=1)).reshape(B, S, -1)
    return jnp.concatenate([o_a, o_b, o_c], axis=-1), ()


def _swiglu(h, wg, wu, wd):
    return (jax.nn.silu(h @ wg) * (h @ wu)) @ wd


def _modulation(cond, w, b):
    m = jax.nn.silu(cond) @ w + b
    return jnp.split(m[:, None, :], N_MOD, axis=-1)


def _modulate(x, g, shift, scale):
    return rms_norm(x, g) * (1.0 + scale) + shift


def _block(x, mods, lp, mixer):
    sh1, sc1, g1, sh2, sc2, g2, sh3, sc3, g3 = mods
    h = _modulate(x, lp['norm_g'][0], sh1, sc1)
    x = x + 0.5 * g1 * _swiglu(h, lp['ffn_wg'][0], lp['ffn_wu'][0], lp['ffn_wd'][0])
    o, state = mixer(_modulate(x, lp['norm_g'][1], sh2, sc2))
    x = x + g2 * (o @ lp['w_out'])
    h = _modulate(x, lp['norm_g'][2], sh3, sc3)
    x = x + 0.5 * g3 * _swiglu(h, lp['ffn_wg'][1], lp['ffn_wu'][1], lp['ffn_wd'][1])
    return x, state


def setup_inputs(seed: int = 0) -> dict:
    key = jax.random.key(seed)
    ks = jax.random.split(key, 26)
    f32 = jnp.float32

    def nrm(k, shape, s):
        return jax.random.normal(k, shape, f32) * s

    def gain(k, shape):
        return 1.0 + 0.02 * jax.random.normal(k, shape, f32)

    return {
        'x_prompt': nrm(ks[0], (BATCH, SEQ, D_MODEL), 1.0),
        'x_sample': nrm(ks[1], (DEC_BATCH, DEC_SEQ, D_MODEL), 1.0),
        'cache_mla_ckv': nrm(ks[2], (DEC_BATCH, DEPTH, PAST_LEN, KV_LORA), 1.0),
        'cache_mla_krope': nrm(ks[3], (DEC_BATCH, DEPTH, PAST_LEN, MLA_ROPE), 1.0),
        'cache_na_k': nrm(ks[4], (DEC_BATCH, DEPTH, PAST_LEN, NA_HEADS, HEAD_DIM), 1.0),
        'cache_na_v': nrm(ks[5], (DEC_BATCH, DEPTH, PAST_LEN, NA_HEADS, HEAD_DIM), 1.0),
        'cache_gqa_k': nrm(ks[6], (DEC_BATCH, DEPTH, PAST_LEN, GQA_KV_HEADS, HEAD_DIM), 1.0),
        'cache_gqa_v': nrm(ks[7], (DEC_BATCH, DEPTH, PAST_LEN, GQA_KV_HEADS, HEAD_DIM), 1.0),
        'c': nrm(ks[8], (DEC_BATCH, D_MODEL), 1.0),
        'c_ctx': nrm(ks[9], (D_MODEL,), 1.0),
        'ada_w': nrm(ks[10], (DEPTH, D_MODEL, N_MOD * D_MODEL), 0.5 * D_MODEL ** -0.5),
        'ada_b': nrm(ks[11], (DEPTH, N_MOD * D_MODEL), 0.02),
        'norm_g': gain(ks[12], (DEPTH, 3, D_MODEL)),
        'ffn_wg': nrm(ks[13], (DEPTH, 2, D_MODEL, D_FF), D_MODEL ** -0.5),
        'ffn_wu': nrm(ks[14], (DEPTH, 2, D_MODEL, D_FF), D_MODEL ** -0.5),
        'ffn_wd': nrm(ks[15], (DEPTH, 2, D_FF, D_MODEL), D_FF ** -0.5),
        'w_in': nrm(ks[16], (DEPTH, D_MODEL, IN_COLS), D_MODEL ** -0.5),
        'mla_q_norm': gain(ks[17], (DEPTH, Q_LORA)),
        'mla_wqb': nrm(ks[18], (DEPTH, Q_LORA, MLA_HEADS * MLA_QK), Q_LORA ** -0.5),
        'mla_kv_norm': gain(ks[19], (DEPTH, KV_LORA)),
        'mla_wkvb': nrm(ks[20], (DEPTH, KV_LORA, MLA_HEADS * (MLA_NOPE + MLA_V)), KV_LORA ** -0.5),
        'na_rpb': nrm(ks[21], (DEPTH, NA_HEADS, 2 * NA_WIN_H - 1, 2 * NA_WIN_W - 1), 0.1),
        'gqa_q_norm': gain(ks[22], (DEPTH, HEAD_DIM)),
        'gqa_k_norm': gain(ks[23], (DEPTH, HEAD_DIM)),
        'w_out': nrm(ks[24], (DEPTH, MIX_W, D_MODEL), MIX_W ** -0.5),
        'final_norm': gain(ks[25], (D_MODEL,)),
    }


def reference(x_prompt, x_sample, cache_mla_ckv, cache_mla_krope, cache_na_k, cache_na_v,
              cache_gqa_k, cache_gqa_v, c, c_ctx, ada_w, ada_b, norm_g, ffn_wg, ffn_wu, ffn_wd,
              w_in, mla_q_norm, mla_wqb, mla_kv_norm, mla_wkvb, na_rpb, gqa_q_norm, gqa_k_norm,
              w_out, final_norm):
    xp = x_prompt
    xs = x_sample
    ctx_states = []
    for l in range(DEPTH):
        lp = {
            'norm_g': norm_g[l], 'ffn_wg': ffn_wg[l], 'ffn_wu': ffn_wu[l], 'ffn_wd': ffn_wd[l],
            'w_in': w_in[l], 'mla_q_norm': mla_q_norm[l], 'mla_wqb': mla_wqb[l],
            'mla_kv_norm': mla_kv_norm[l], 'mla_wkvb': mla_wkvb[l], 'na_rpb': na_rpb[l],
            'gqa_q_norm': gqa_q_norm[l], 'gqa_k_norm': gqa_k_norm[l], 'w_out': w_out[l],
        }
        mods_ctx = _modulation(c_ctx[None, :], ada_w[l], ada_b[l])
        xp, st = _block(xp, mods_ctx, lp, lambda h: _context_mixer(h, lp))
        ctx_states.append(st)
        caches = (cache_mla_ckv[:, l], cache_mla_krope[:, l], cache_na_k[:, l], cache_na_v[:, l],
                  cache_gqa_k[:, l], cache_gqa_v[:, l])
        mods_lat = _modulation(c, ada_w[l], ada_b[l])
        xs, _ = _block(xs, mods_lat, lp, lambda h: _latent_mixer(h, lp, caches))
    y_prompt = rms_norm(xp, final_norm)
    y_sample = rms_norm(xs, final_norm)
    new_mla_ckv = jnp.stack([s[0] for s in ctx_states], axis=1)
    new_mla_krope = jnp.stack([s[1] for s in ctx_states], axis=1)
    new_na_k = jnp.stack([s[2] for s in ctx_states], axis=1)
    new_na_v = jnp.stack([s[3] for s in ctx_states], axis=1)
    new_gqa_k = jnp.stack([s[4] for s in ctx_states], axis=1)
    new_gqa_v = jnp.stack([s[5] for s in ctx_states], axis=1)
    return (y_prompt, y_sample, new_mla_ckv, new_mla_krope, new_na_k, new_na_v, new_gqa_k, new_gqa_v)
```

```python
import functools

import jax
import jax.numpy as jnp
from jax import lax
from jax.experimental import pallas as pl
from jax.experimental.pallas import tpu as pltpu

F32 = jnp.float32
BF16 = jnp.bfloat16

D_MODEL = 2048
D_FF = 5632
N_MOD = 9
GRID_W = 64
ROPE_THETA = 10000.0
HEAD_DIM = 128
MLA_HEADS = 8
MLA_NOPE = 128
MLA_ROPE = 64
MLA_QK = MLA_NOPE + MLA_ROPE
MLA_PAD = 256
Q_LORA = 512
KV_LORA = 256
NA_HEADS = 4
NA_WIN_H = 8
NA_WIN_W = 16
GQA_HEADS = 4
GQA_KV_HEADS = 2
EPS = 1e-6
NEG_INF = -1e30
MOD_ROWS = 16

COL_CQ = 0
COL_GQ = 512
COL_NAQ = 1024
COL_NAK = 1536
COL_NAV = 2048
COL_CKV = 2560
COL_GK = 2816
COL_GV = 3072
COL_KR = 3328
PROJ_W = 3584

VMEM_LIMIT = 56 * 1024 * 1024


def _params(*sem):
    return pltpu.CompilerParams(dimension_semantics=sem, vmem_limit_bytes=VMEM_LIMIT)


def _rms(x, g):
    return x * lax.rsqrt(jnp.mean(x * x, axis=-1, keepdims=True) + EPS) * g


def _rope(x, cos, sin, quarter):
    n = x.shape[-1]
    lane = lax.broadcasted_iota(jnp.int32, x.shape, 1)
    first = (lane % (2 * quarter)) < quarter
    sw = jnp.where(first, pltpu.roll(x, n - quarter, 1), pltpu.roll(x, quarter, 1))
    return x * cos + sw * sin


def _mod_kernel(c_ref, w_ref, b_ref, o_ref):
    c = c_ref[...]
    s = (c * jax.nn.sigmoid(c)).astype(BF16)
    o_ref[0] = jnp.dot(s, w_ref[0].astype(BF16), preferred_element_type=F32) + b_ref[0]


def _modulation(cond, ada_w, ada_b):
    depth, d, n = ada_w.shape
    tn = 512
    return pl.pallas_call(
        _mod_kernel,
        grid=(depth, n // tn),
        in_specs=[
            pl.BlockSpec((MOD_ROWS, d), lambda l, j: (0, 0)),
            pl.BlockSpec((1, d, tn), lambda l, j: (l, 0, j)),
            pl.BlockSpec((1, 1, tn), lambda l, j: (l, 0, j)),
        ],
        out_specs=pl.BlockSpec((1, MOD_ROWS, tn), lambda l, j: (l, 0, j)),
        out_shape=jax.ShapeDtypeStruct((depth, MOD_ROWS, n), F32),
        compiler_params=_params("parallel", "parallel"),
        name="modulation",
    )(cond, ada_w, ada_b.reshape(depth, 1, n))


def _mod_spec(l, chunk, row_fn, width=D_MODEL, col_fn=None):
    per = D_MODEL // width
    if col_fn is None:
        return pl.BlockSpec((1, 1, width), lambda i, j: (l * MOD_ROWS + row_fn(i), 0, chunk * per))
    return pl.BlockSpec((1, 1, width), lambda i, j: (l * MOD_ROWS + row_fn(i), 0, chunk * per + col_fn(j)))


def _ffn_kernel(x_ref, ng_ref, sh_ref, sc_ref, gt_ref, wg_ref, wu_ref, wd_ref, *rest, final):
    if final:
        fg_ref, o_ref, h_scr, acc_scr = rest
    else:
        o_ref, h_scr, acc_scr = rest
    f = pl.program_id(1)

    @pl.when(f == 0)
    def _():
        h = _rms(x_ref[...], ng_ref[0]) * (1.0 + sc_ref[0]) + sh_ref[0]
        h_scr[...] = h.astype(BF16)
        acc_scr[...] = jnp.zeros_like(acc_scr)

    h = h_scr[...]
    g = jnp.dot(h, wg_ref[...], preferred_element_type=F32)
    u = jnp.dot(h, wu_ref[...], preferred_element_type=F32)
    a = (g * jax.nn.sigmoid(g) * u).astype(BF16)
    acc_scr[...] += jnp.dot(a, wd_ref[...], preferred_element_type=F32)

    @pl.when(f == pl.num_programs(1) - 1)
    def _():
        y = x_ref[...] + 0.5 * gt_ref[0] * acc_scr[...]
        if final:
            y = _rms(y, fg_ref[0])
        o_ref[...] = y


def _ffn(x, mods, norm_g, l, sub, k, row_fn, wg, wu, wd, final_g=None):
    t, d = x.shape
    tm, tf = 512, 512
    chunk = 3 * sub
    in_specs = [
        pl.BlockSpec((tm, d), lambda i, j: (i, 0)),
        pl.BlockSpec((1, 1, d), lambda i, j: (l * 3 + sub, 0, 0)),
        _mod_spec(l, chunk, row_fn),
        _mod_spec(l, chunk + 1, row_fn),
        _mod_spec(l, chunk + 2, row_fn),
        pl.BlockSpec((None, None, d, tf), lambda i, j: (l, k, 0, j)),
        pl.BlockSpec((None, None, d, tf), lambda i, j: (l, k, 0, j)),
        pl.BlockSpec((None, None, tf, d), lambda i, j: (l, k, j, 0)),
    ]
    args = [x, norm_g, mods, mods, mods, wg, wu, wd]
    if final_g is not None:
        in_specs.append(pl.BlockSpec((1, 1, d), lambda i, j: (0, 0, 0)))
        args.append(final_g)
    return pl.pallas_call(
        functools.partial(_ffn_kernel, final=final_g is not None),
        grid=(t // tm, D_FF // tf),
        in_specs=in_specs,
        out_specs=pl.BlockSpec((tm, d), lambda i, j: (i, 0)),
        out_shape=jax.ShapeDtypeStruct((t, d), F32),
        scratch_shapes=[pltpu.VMEM((tm, d), BF16), pltpu.VMEM((tm, d), F32)],
        compiler_params=_params("parallel", "arbitrary"),
        name="ffn",
    )(*args)


def _inproj_kernel(x_ref, ng_ref, sh_ref, sc_ref, w_ref, o_ref, h_scr):
    @pl.when(pl.program_id(1) == 0)
    def _():
        h = _rms(x_ref[...], ng_ref[0]) * (1.0 + sc_ref[0]) + sh_ref[0]
        h_scr[...] = h.astype(BF16)

    o_ref[...] = jnp.dot(h_scr[...], w_ref[...], preferred_element_type=F32)


def _inproj(x, mods, norm_g, l, row_fn, w_in):
    t, d = x.shape
    tm, tn = 512, 512
    return pl.pallas_call(
        _inproj_kernel,
        grid=(t // tm, PROJ_W // tn),
        in_specs=[
            pl.BlockSpec((tm, d), lambda i, j: (i, 0)),
            pl.BlockSpec((1, 1, d), lambda i, j: (l * 3 + 1, 0, 0)),
            _mod_spec(l, 3, row_fn),
            _mod_spec(l, 4, row_fn),
            pl.BlockSpec((None, d, tn), lambda i, j: (l, 0, j)),
        ],
        out_specs=pl.BlockSpec((tm, tn), lambda i, j: (i, j)),
        out_shape=jax.ShapeDtypeStruct((t, PROJ_W), F32),
        scratch_shapes=[pltpu.VMEM((tm, d), BF16)],
        compiler_params=_params("parallel", "arbitrary"),
        name="inproj",
    )(x, norm_g, mods, mods, w_in)


def _expand_kv(ckv_bf, kr, wk_ref, wv_ref, k_out, v_out):
    kn = jnp.dot(ckv_bf, wk_ref[...], preferred_element_type=F32)
    v_out[...] = jnp.dot(ckv_bf, wv_ref[...], preferred_element_type=F32).astype(BF16)
    kr_bf = kr.astype(BF16)
    for h in range(MLA_HEADS):
        k_out[:, h * MLA_PAD:h * MLA_PAD + MLA_NOPE] = kn[:, h * MLA_NOPE:(h + 1) * MLA_NOPE].astype(BF16)
        k_out[:, h * MLA_PAD + MLA_NOPE:(h + 1) * MLA_PAD] = kr_bf


def _heads_kernel(cq_ref, gq_ref, ckv_ref, gk_ref, kr_ref, qn_ref, wqb_ref, kvn_ref, wk_ref, wv_ref,
                  gqn_ref, gkn_ref, *rest, rope):
    if rope:
        c64_ref, s64_ref, c128_ref, s128_ref = rest[:4]
        rest = rest[4:]
        rot64 = lambda v: _rope(v, c64_ref[...], s64_ref[...], MLA_ROPE // 4)
        rot128 = lambda v: _rope(v, c128_ref[...], s128_ref[...], HEAD_DIM // 4)
    else:
        rot64 = rot128 = lambda v: v
    q_out, k_out, v_out, ckv_out, kr_out, gq_out, gk_out = rest

    qn = _rms(cq_ref[...], qn_ref[0]).astype(BF16)
    q = jnp.dot(qn, wqb_ref[...], preferred_element_type=F32)
    for h in range(MLA_HEADS):
        lo = h * MLA_PAD
        q_out[:, lo:lo + MLA_NOPE] = q[:, lo:lo + MLA_NOPE].astype(BF16)
        q_out[:, lo + MLA_NOPE:lo + MLA_PAD] = rot64(q[:, lo + MLA_NOPE:lo + MLA_PAD]).astype(BF16)

    ckv = _rms(ckv_ref[...], kvn_ref[0])
    ckv_out[...] = ckv
    kr = rot64(kr_ref[...])
    kr_out[...] = kr
    _expand_kv(ckv.astype(BF16), kr, wk_ref, wv_ref, k_out, v_out)

    for h in range(GQA_HEADS):
        sl = slice(h * HEAD_DIM, (h + 1) * HEAD_DIM)
        gq_out[:, sl] = rot128(_rms(gq_ref[:, sl], gqn_ref[0])).astype(BF16)
    for h in range(GQA_KV_HEADS):
        sl = slice(h * HEAD_DIM, (h + 1) * HEAD_DIM)
        gk_out[:, sl] = rot128(_rms(gk_ref[:, sl], gkn_ref[0]))


def _heads(proj, l, wts, tables):
    t = proj.shape[0]
    tm = 512
    rope = tables is not None
    qw = MLA_HEADS * MLA_PAD
    vw = MLA_HEADS * HEAD_DIM
    in_specs = [
        pl.BlockSpec((tm, 512), lambda i: (i, COL_CQ // 512)),
        pl.BlockSpec((tm, 512), lambda i: (i, COL_GQ // 512)),
        pl.BlockSpec((tm, 256), lambda i: (i, COL_CKV // 256)),
        pl.BlockSpec((tm, 256), lambda i: (i, COL_GK // 256)),
        pl.BlockSpec((tm, 128), lambda i: (i, COL_KR // 128)),
        pl.BlockSpec((1, 1, Q_LORA), lambda i: (l, 0, 0)),
        pl.BlockSpec((None, Q_LORA, qw), lambda i: (l, 0, 0)),
        pl.BlockSpec((1, 1, KV_LORA), lambda i: (l, 0, 0)),
        pl.BlockSpec((None, KV_LORA, vw), lambda i: (l, 0, 0)),
        pl.BlockSpec((None, KV_LORA, vw), lambda i: (l, 0, 0)),
        pl.BlockSpec((1, 1, HEAD_DIM), lambda i: (l, 0, 0)),
        pl.BlockSpec((1, 1, HEAD_DIM), lambda i: (l, 0, 0)),
    ]
    args = [proj] * 5 + [wts["q_norm"], wts["wqb"], wts["kv_norm"], wts["wk"], wts["wv"],
                         wts["gq_norm"], wts["gk_norm"]]
    if rope:
        per = tables[0].shape[0] // tm
        in_specs += [pl.BlockSpec((tm, 128), lambda i: (i % per, 0))] * 4
        args += list(tables)
    row = lambda w: pl.BlockSpec((tm, w), lambda i: (i, 0))
    return pl.pallas_call(
        functools.partial(_heads_kernel, rope=rope),
        grid=(t // tm,),
        in_specs=in_specs,
        out_specs=[row(qw), row(qw), row(vw), row(KV_LORA), row(128), row(512), row(256)],
        out_shape=[
            jax.ShapeDtypeStruct((t, qw), BF16),
            jax.ShapeDtypeStruct((t, qw), BF16),
            jax.ShapeDtypeStruct((t, vw), BF16),
            jax.ShapeDtypeStruct((t, KV_LORA), F32),
            jax.ShapeDtypeStruct((t, 128), F32),
            jax.ShapeDtypeStruct((t, 512), BF16),
            jax.ShapeDtypeStruct((t, 256), F32),
        ],
        compiler_params=_params("parallel"),
        name="heads",
    )(*args)


def _cache_kv_kernel(ckv_ref, kr_ref, wk_ref, wv_ref, k_out, v_out):
    _expand_kv(ckv_ref[...].astype(BF16), kr_ref[...], wk_ref, wv_ref, k_out, v_out)


def _cache_kv(cache_ckv, cache_kr_pad, l, wts):
    nb, _, past, _ = cache_ckv.shape
    qw = MLA_HEADS * MLA_PAD
    vw = MLA_HEADS * HEAD_DIM
    return pl.pallas_call(
        _cache_kv_kernel,
        grid=(nb,),
        in_specs=[
            pl.BlockSpec((None, None, past, KV_LORA), lambda b: (b, l, 0, 0)),
            pl.BlockSpec((None, None, past, 128), lambda b: (b, l, 0, 0)),
            pl.BlockSpec((None, KV_LORA, vw), lambda b: (l, 0, 0)),
            pl.BlockSpec((None, KV_LORA, vw), lambda b: (l, 0, 0)),
        ],
        out_specs=[pl.BlockSpec((past, qw), lambda b: (b, 0)), pl.BlockSpec((past, vw), lambda b: (b, 0))],
        out_shape=[jax.ShapeDtypeStruct((nb * past, qw), BF16), jax.ShapeDtypeStruct((nb * past, vw), BF16)],
        compiler_params=_params("parallel"),
        name="cache_kv",
    )(cache_ckv, cache_kr_pad, wts["wk"], wts["wv"])


def _attn_kernel(*refs, n_heads, group, dk, dv, scale, seg_bias):
    q_ref = refs[0]
    pos = 1
    segs = []
    for has_bias in seg_bias:
        b_ref = refs[pos + 2] if has_bias else None
        segs.append((refs[pos], refs[pos + 1], b_ref))
        pos += 3 if has_bias else 2
    o_ref = refs[pos]
    for h in range(n_heads):
        hk = h // group
        q = q_ref[:, h * dk:(h + 1) * dk].astype(BF16)
        scores = []
        for k_ref, _, b_ref in segs:
            k = k_ref[:, hk * dk:(hk + 1) * dk].astype(BF16)
            s = lax.dot_general(q, k, (((1,), (1,)), ((), ())), preferred_element_type=F32) * scale
            if b_ref is not None:
                s = s + b_ref[h]
            scores.append(s)
        m = functools.reduce(jnp.maximum, [s.max(axis=-1, keepdims=True) for s in scores])
        probs = [jnp.exp(s - m) for s in scores]
        denom = functools.reduce(jnp.add, [p.sum(axis=-1, keepdims=True) for p in probs])
        o = None
        for p, (_, v_ref, _) in zip(probs, segs):
            v = v_ref[:, hk * dv:(hk + 1) * dv].astype(BF16)
            pv = jnp.dot(p.astype(BF16), v, preferred_element_type=F32)
            o = pv if o is None else o + pv
        o_ref[:, h * dv:(h + 1) * dv] = (o / denom).astype(o_ref.dtype)


def _attention(q, segs, *, t, grid, tq, q_index, n_heads, group, dk, dv, scale, name):
    in_specs = [pl.BlockSpec((tq, n_heads * dk), q_index)]
    args = [q]
    seg_bias = []
    for k_arr, k_spec, v_arr, v_spec, b_arr, b_spec in segs:
        in_specs += [k_spec, v_spec]
        args += [k_arr, v_arr]
        seg_bias.append(b_arr is not None)
        if b_arr is not None:
            in_specs.append(b_spec)
            args.append(b_arr)
    out_index = lambda *g: (q_index(*g)[0], 0)
    return pl.pallas_call(
        functools.partial(_attn_kernel, n_heads=n_heads, group=group, dk=dk, dv=dv, scale=scale,
                          seg_bias=tuple(seg_bias)),
        grid=grid,
        in_specs=in_specs,
        out_specs=pl.BlockSpec((tq, n_heads * dv), out_index),
        out_shape=jax.ShapeDtypeStruct((t, n_heads * dv), BF16),
        compiler_params=_params(*(("parallel",) * len(grid))),
        name=name,
    )(*args)


def _context_attention(proj, q_mla, k_mla, v_mla, gq, gk, seq):
    t = proj.shape[0]
    grid = (t // seq,)
    blk = lambda w, c: pl.BlockSpec((seq, w), lambda b: (b, c))
    common = dict(t=t, grid=grid, tq=seq)
    o_a = _attention(q_mla, [(k_mla, blk(MLA_HEADS * MLA_PAD, 0), v_mla, blk(MLA_HEADS * HEAD_DIM, 0), None, None)],
                     q_index=lambda b: (b, 0), n_heads=MLA_HEADS, group=1, dk=MLA_PAD, dv=HEAD_DIM,
                     scale=MLA_QK ** -0.5, name="ctx_mla", **common)
    o_b = _attention(proj, [(proj, blk(512, COL_NAK // 512), proj, blk(512, COL_NAV // 512), None, None)],
                     q_index=lambda b: (b, COL_NAQ // 512), n_heads=NA_HEADS, group=1, dk=HEAD_DIM, dv=HEAD_DIM,
                     scale=HEAD_DIM ** -0.5, name="ctx_na", **common)
    o_c = _attention(gq, [(gk, blk(256, 0), proj, blk(256, COL_GV // 256), None, None)],
                     q_index=lambda b: (b, 0), n_heads=GQA_HEADS, group=GQA_HEADS // GQA_KV_HEADS,
                     dk=HEAD_DIM, dv=HEAD_DIM, scale=HEAD_DIM ** -0.5, name="ctx_gqa", **common)
    return o_a, o_b, o_c


def _latent_attention(proj, q_mla, k_mla, v_mla, gq, gk, kc_mla, vc_mla, caches, bias, l, seq):
    cache_na_k, cache_na_v, cache_gqa_k, cache_gqa_v = caches
    t = proj.shape[0]
    past = cache_na_k.shape[2]
    tq = 512
    nq = seq // tq
    grid = (t // seq, nq)
    own = lambda w, c: pl.BlockSpec((seq, w), lambda b, i: (b, c))
    flat = lambda w: pl.BlockSpec((past, w), lambda b, i: (b, 0))
    cached = lambda w: pl.BlockSpec((None, None, past, w), lambda b, i: (b, l, 0, 0))
    common = dict(t=t, grid=grid, tq=tq)
    qw, vw = MLA_HEADS * MLA_PAD, MLA_HEADS * HEAD_DIM
    o_a = _attention(q_mla, [(kc_mla, flat(qw), vc_mla, flat(vw), None, None),
                             (k_mla, own(qw, 0), v_mla, own(vw, 0), None, None)],
                     q_index=lambda b, i: (b * nq + i, 0), n_heads=MLA_HEADS, group=1, dk=MLA_PAD, dv=HEAD_DIM,
                     scale=MLA_QK ** -0.5, name="lat_mla", **common)
    bias_spec = pl.BlockSpec((NA_HEADS, tq, seq), lambda b, i: (0, i, 0))
    o_b = _attention(proj, [(cache_na_k, cached(512), cache_na_v, cached(512), None, None),
                            (proj, own(512, COL_NAK // 512), proj, own(512, COL_NAV // 512), bias, bias_spec)],
                     q_index=lambda b, i: (b * nq + i, COL_NAQ // 512), n_heads=NA_HEADS, group=1,
                     dk=HEAD_DIM, dv=HEAD_DIM, scale=HEAD_DIM ** -0.5, name="lat_na", **common)
    o_c = _attention(gq, [(cache_gqa_k, cached(256), cache_gqa_v, cached(256), None, None),
                          (gk, own(256, 0), proj, own(256, COL_GV // 256), None, None)],
                     q_index=lambda b, i: (b * nq + i, 0), n_heads=GQA_HEADS, group=GQA_HEADS // GQA_KV_HEADS,
                     dk=HEAD_DIM, dv=HEAD_DIM, scale=HEAD_DIM ** -0.5, name="lat_gqa", **common)
    return o_a, o_b, o_c


def _outproj_kernel(x_ref, gt_ref, oa_ref, ob_ref, oc_ref, w_ref, o_ref):
    wa = oa_ref.shape[1]
    wb = ob_ref.shape[1]
    acc = jnp.dot(oa_ref[...], w_ref[0:wa, :], preferred_element_type=F32)
    acc += jnp.dot(ob_ref[...], w_ref[wa:wa + wb, :], preferred_element_type=F32)
    acc += jnp.dot(oc_ref[...], w_ref[wa + wb:, :], preferred_element_type=F32)
    o_ref[...] = x_ref[...] + gt_ref[0] * acc


def _outproj(x, mods, l, row_fn, o_a, o_b, o_c, w_out):
    t, d = x.shape
    tm, tn = 512, 512
    row = lambda a: pl.BlockSpec((tm, a.shape[1]), lambda i, j: (i, 0))
    return pl.pallas_call(
        _outproj_kernel,
        grid=(t // tm, d // tn),
        in_specs=[
            pl.BlockSpec((tm, tn), lambda i, j: (i, j)),
            _mod_spec(l, 5, row_fn, width=tn, col_fn=lambda j: j),
            row(o_a), row(o_b), row(o_c),
            pl.BlockSpec((None, d, tn), lambda i, j: (l, 0, j)),
        ],
        out_specs=pl.BlockSpec((tm, tn), lambda i, j: (i, j)),
        out_shape=jax.ShapeDtypeStruct((t, d), F32),
        compiler_params=_params("parallel", "parallel"),
        name="outproj",
    )(x, mods, o_a, o_b, o_c, w_out)


def _rope_tables(seq, d):
    quarter = d // 4
    tt = jnp.arange(seq)
    pos = jnp.stack([tt // GRID_W, tt % GRID_W], axis=-1).astype(jnp.float32)
    inv = ROPE_THETA ** (-jnp.arange(quarter, dtype=jnp.float32) / quarter)
    ang = pos[:, :, None] * inv
    cos = jnp.cos(ang)
    sin = jnp.sin(ang)
    cos_t = jnp.stack([cos, cos], axis=2).reshape(seq, d)
    sin_t = jnp.stack([-sin, sin], axis=2).reshape(seq, d)
    pad = 128 - d
    if pad:
        cos_t = jnp.concatenate([cos_t, jnp.ones((seq, pad), cos_t.dtype)], axis=-1)
        sin_t = jnp.concatenate([sin_t, jnp.zeros((seq, pad), sin_t.dtype)], axis=-1)
    return cos_t.astype(F32), sin_t.astype(F32)


def _na_bias(rpb, seq):
    rows = seq // GRID_W
    wh = min(NA_WIN_H, rows)
    r = jnp.arange(rows)
    c = jnp.arange(GRID_W)
    rs = jnp.clip(r - wh // 2, 0, rows - wh)
    cs = jnp.clip(c - NA_WIN_W // 2, 0, GRID_W - NA_WIN_W)
    vr = (r[None, :] >= rs[:, None]) & (r[None, :] < rs[:, None] + wh)
    vc = (c[None, :] >= cs[:, None]) & (c[None, :] < cs[:, None] + NA_WIN_W)
    dr = jnp.clip(r[None, :] - r[:, None] + (NA_WIN_H - 1), 0, 2 * NA_WIN_H - 2)
    dc = jnp.clip(c[None, :] - c[:, None] + (NA_WIN_W - 1), 0, 2 * NA_WIN_W - 2)
    b = rpb[:, dr[:, None, :, None], dc[None, :, None, :]]
    valid = vr[:, None, :, None] & vc[None, :, None, :]
    b = jnp.where(valid[None], b.astype(F32), NEG_INF)
    return b.reshape(rpb.shape[0], seq, seq)


def kernel(x_prompt, x_sample, cache_mla_ckv, cache_mla_krope, cache_na_k, cache_na_v, cache_gqa_k, cache_gqa_v, c, c_ctx, ada_w, ada_b, norm_g, ffn_wg, ffn_wu, ffn_wd, w_in, mla_q_norm, mla_wqb, mla_kv_norm, mla_wkvb, na_rpb, gqa_q_norm, gqa_k_norm, w_out, final_norm):
    depth = ada_w.shape[0]
    nb_ctx, seq_ctx, d = x_prompt.shape
    nb_lat, seq_lat, _ = x_sample.shape
    past = cache_mla_ckv.shape[2]

    wg = ffn_wg.astype(BF16)
    wu = ffn_wu.astype(BF16)
    wd = ffn_wd.astype(BF16)
    s1, s2, s3 = Q_LORA, Q_LORA + KV_LORA, Q_LORA + KV_LORA + MLA_ROPE
    s4 = s3 + 3 * NA_HEADS * HEAD_DIM
    s5 = s4 + GQA_HEADS * HEAD_DIM
    n_in = w_in.shape[-1]
    w_in_p = jnp.concatenate(
        [w_in[..., 0:s1], w_in[..., s4:s5], w_in[..., s3:s4], w_in[..., s1:s2], w_in[..., s5:n_in],
         w_in[..., s2:s3], jnp.zeros((depth, d, PROJ_W - n_in), w_in.dtype)], axis=-1).astype(BF16)
    wqb = jnp.pad(mla_wqb.reshape(depth, Q_LORA, MLA_HEADS, MLA_QK),
                  ((0, 0), (0, 0), (0, 0), (0, MLA_PAD - MLA_QK))).reshape(depth, Q_LORA, MLA_HEADS * MLA_PAD)
    wkv = mla_wkvb.reshape(depth, KV_LORA, MLA_HEADS, 2 * HEAD_DIM)
    wts = {
        "q_norm": mla_q_norm.reshape(depth, 1, Q_LORA),
        "wqb": wqb.astype(BF16),
        "kv_norm": mla_kv_norm.reshape(depth, 1, KV_LORA),
        "wk": wkv[..., :MLA_NOPE].reshape(depth, KV_LORA, MLA_HEADS * MLA_NOPE).astype(BF16),
        "wv": wkv[..., MLA_NOPE:].reshape(depth, KV_LORA, MLA_HEADS * HEAD_DIM).astype(BF16),
        "gq_norm": gqa_q_norm.reshape(depth, 1, HEAD_DIM),
        "gk_norm": gqa_k_norm.reshape(depth, 1, HEAD_DIM),
    }
    w_out_b = w_out.astype(BF16)
    norm_g3 = norm_g.reshape(depth * 3, 1, d)
    final_g = final_norm.reshape(1, 1, d)
    tables = _rope_tables(seq_lat, MLA_ROPE) + _rope_tables(seq_lat, HEAD_DIM)
    kr_cache = jnp.pad(cache_mla_krope, ((0, 0), (0, 0), (0, 0), (0, 128 - MLA_ROPE)))
    caches = (cache_na_k.reshape(nb_lat, depth, past, NA_HEADS * HEAD_DIM),
              cache_na_v.reshape(nb_lat, depth, past, NA_HEADS * HEAD_DIM),
              cache_gqa_k.reshape(nb_lat, depth, past, GQA_KV_HEADS * HEAD_DIM),
              cache_gqa_v.reshape(nb_lat, depth, past, GQA_KV_HEADS * HEAD_DIM))

    cond = jnp.concatenate([c_ctx[None, :], c, jnp.zeros((MOD_ROWS - 1 - nb_lat, d), c.dtype)], axis=0)
    mods = _modulation(cond, ada_w, ada_b).reshape(depth * MOD_ROWS, 1, N_MOD * d)

    xp = x_prompt.reshape(nb_ctx * seq_ctx, d)
    xs = x_sample.reshape(nb_lat * seq_lat, d)
    ctx_row = lambda i: 0
    tm = 512
    lat_row = lambda i: 1 + (i * tm) // seq_lat

    states = []
    for l in range(depth):
        last = l == depth - 1
        xp = _ffn(xp, mods, norm_g3, l, 0, 0, ctx_row, wg, wu, wd)
        proj = _inproj(xp, mods, norm_g3, l, ctx_row, w_in_p)
        q_mla, k_mla, v_mla, ckv_n, kr, gq, gk = _heads(proj, l, wts, None)
        o_a, o_b, o_c = _context_attention(proj, q_mla, k_mla, v_mla, gq, gk, seq_ctx)
        xp = _outproj(xp, mods, l, ctx_row, o_a, o_b, o_c, w_out_b)
        xp = _ffn(xp, mods, norm_g3, l, 2, 1, ctx_row, wg, wu, wd, final_g if last else None)
        states.append((
            ckv_n.reshape(nb_ctx, seq_ctx, KV_LORA),
            kr[:, :MLA_ROPE].reshape(nb_ctx, seq_ctx, MLA_ROPE),
            proj[:, COL_NAK:COL_NAK + 512].reshape(nb_ctx, seq_ctx, NA_HEADS, HEAD_DIM),
            proj[:, COL_NAV:COL_NAV + 512].reshape(nb_ctx, seq_ctx, NA_HEADS, HEAD_DIM),
            gk.reshape(nb_ctx, seq_ctx, GQA_KV_HEADS, HEAD_DIM),
            proj[:, COL_GV:COL_GV + 256].reshape(nb_ctx, seq_ctx, GQA_KV_HEADS, HEAD_DIM),
        ))
        xs = _ffn(xs, mods, norm_g3, l, 0, 0, lat_row, wg, wu, wd)
        proj = _inproj(xs, mods, norm_g3, l, lat_row, w_in_p)
        q_mla, k_mla, v_mla, _, _, gq, gk = _heads(proj, l, wts, tables)
        kc_mla, vc_mla = _cache_kv(cache_mla_ckv, kr_cache, l, wts)
        bias = _na_bias(na_rpb[l], seq_lat)
        o_a, o_b, o_c = _latent_attention(proj, q_mla, k_mla, v_mla, gq, gk, kc_mla, vc_mla, caches, bias, l,
                                          seq_lat)
        xs = _outproj(xs, mods, l, lat_row, o_a, o_b, o_c, w_out_b)
        xs = _ffn(xs, mods, norm_g3, l, 2, 1, lat_row, wg, wu, wd, final_g if last else None)

    y_prompt = xp.reshape(nb_ctx, seq_ctx, d)
    y_sample = xs.reshape(nb_lat, seq_lat, d)
    outs = tuple(jnp.stack([s[k] for s in states], axis=1) for k in range(6))
    return (y_prompt, y_sample) + outs
```

```python
import functools

import jax
import jax.numpy as jnp
from jax import lax
from jax.experimental import pallas as pl
from jax.experimental.pallas import tpu as pltpu

F32 = jnp.float32
BF16 = jnp.bfloat16

D_MODEL = 2048
D_FF = 5632
N_MOD = 9
GRID_W = 64
ROPE_THETA = 10000.0
HEAD_DIM = 128
MLA_HEADS = 8
MLA_NOPE = 128
MLA_ROPE = 64
MLA_QK = MLA_NOPE + MLA_ROPE
MLA_PAD = 256
Q_LORA = 512
KV_LORA = 256
NA_HEADS = 4
NA_WIN_H = 8
NA_WIN_W = 16
GQA_HEADS = 4
GQA_KV_HEADS = 2
EPS = 1e-6
NEG_INF = -1e30
MOD_ROWS = 16

COL_CQ = 0
COL_GQ = 512
COL_NAQ = 1024
COL_NAK = 1536
COL_NAV = 2048
COL_CKV = 2560
COL_GK = 2816
COL_GV = 3072
COL_KR = 3328
PROJ_W = 3584

VMEM_LIMIT = 56 * 1024 * 1024


def _params(*sem):
    return pltpu.CompilerParams(dimension_semantics=sem, vmem_limit_bytes=VMEM_LIMIT)


def _rms(x, g):
    return x * lax.rsqrt(jnp.mean(x * x, axis=-1, keepdims=True) + EPS) * g


NORM_ROWS = 256


def _norm_modulate_store(x_ref, ng_ref, sc_ref, sh_ref, h_scr):
    gain = ng_ref[0] * (1.0 + sc_ref[0])
    shift = sh_ref[0]

    def body(i, carry):
        rows = pl.ds(pl.multiple_of(i * NORM_ROWS, NORM_ROWS), NORM_ROWS)
        x = x_ref[rows, :]
        r = lax.rsqrt(jnp.mean(x * x, axis=-1, keepdims=True) + EPS)
        h_scr[rows, :] = (x * r * gain + shift).astype(BF16)
        return carry

    lax.fori_loop(0, x_ref.shape[0] // NORM_ROWS, body, 0)


def _rope(x, cos, sin, quarter):
    n = x.shape[-1]
    lane = lax.broadcasted_iota(jnp.int32, x.shape, 1)
    first = (lane % (2 * quarter)) < quarter
    sw = jnp.where(first, pltpu.roll(x, n - quarter, 1), pltpu.roll(x, quarter, 1))
    return x * cos + sw * sin


def _mod_kernel(c_ref, w_ref, b_ref, o_ref):
    c = c_ref[...]
    s = (c * jax.nn.sigmoid(c)).astype(BF16)
    o_ref[0] = jnp.dot(s, w_ref[0].astype(BF16), preferred_element_type=F32) + b_ref[0]


def _modulation(cond, ada_w, ada_b):
    depth, d, n = ada_w.shape
    tn = 512
    return pl.pallas_call(
        _mod_kernel,
        grid=(depth, n // tn),
        in_specs=[
            pl.BlockSpec((MOD_ROWS, d), lambda l, j: (0, 0)),
            pl.BlockSpec((1, d, tn), lambda l, j: (l, 0, j)),
            pl.BlockSpec((1, 1, tn), lambda l, j: (l, 0, j)),
        ],
        out_specs=pl.BlockSpec((1, MOD_ROWS, tn), lambda l, j: (l, 0, j)),
        out_shape=jax.ShapeDtypeStruct((depth, MOD_ROWS, n), F32),
        compiler_params=_params("parallel", "parallel"),
        name="modulation",
    )(cond, ada_w, ada_b.reshape(depth, 1, n))


def _mod_spec(l, chunk, stream, tm, width=D_MODEL, tiled=False):
    base, tokens = stream
    per = D_MODEL // width
    row = lambda i: l * MOD_ROWS + base + (i * tm) // tokens
    if tiled:
        return pl.BlockSpec((1, 1, width), lambda i, j: (row(i), 0, chunk * per + j))
    return pl.BlockSpec((1, 1, width), lambda i, j: (row(i), 0, chunk * per))


def _ffn_kernel(x_ref, ng_ref, sh_ref, sc_ref, gt_ref, wg_ref, wu_ref, wd_ref, *rest, final):
    if final:
        fg_ref, o_ref, h_scr = rest
    else:
        o_ref, h_scr = rest
    f = pl.program_id(1)

    @pl.when(f == 0)
    def _():
        _norm_modulate_store(x_ref, ng_ref, sc_ref, sh_ref, h_scr)
        o_ref[...] = jnp.zeros_like(o_ref)

    h = h_scr[...]
    g = jnp.dot(h, wg_ref[...], preferred_element_type=F32)
    u = jnp.dot(h, wu_ref[...], preferred_element_type=F32)
    a = (g * jax.nn.sigmoid(g) * u).astype(BF16)
    o_ref[...] += jnp.dot(a, wd_ref[...], preferred_element_type=F32)

    @pl.when(f == pl.num_programs(1) - 1)
    def _():
        y = x_ref[...] + 0.5 * gt_ref[0] * o_ref[...]
        if final:
            y = _rms(y, fg_ref[0])
        o_ref[...] = y


def _ffn(x, mods, norm_g, l, sub, k, stream, wg, wu, wd, final_g=None):
    t, d = x.shape
    tm, tf = 1024, 512
    chunk = 3 * sub
    in_specs = [
        pl.BlockSpec((tm, d), lambda i, j: (i, 0), pipeline_mode=pl.Buffered(1)),
        pl.BlockSpec((1, 1, d), lambda i, j: (l * 3 + sub, 0, 0)),
        _mod_spec(l, chunk, stream, tm),
        _mod_spec(l, chunk + 1, stream, tm),
        _mod_spec(l, chunk + 2, stream, tm),
        pl.BlockSpec((None, None, d, tf), lambda i, j: (l, k, 0, j)),
        pl.BlockSpec((None, None, d, tf), lambda i, j: (l, k, 0, j)),
        pl.BlockSpec((None, None, tf, d), lambda i, j: (l, k, j, 0)),
    ]
    args = [x, norm_g, mods, mods, mods, wg, wu, wd]
    if final_g is not None:
        in_specs.append(pl.BlockSpec((1, 1, d), lambda i, j: (0, 0, 0)))
        args.append(final_g)
    return pl.pallas_call(
        functools.partial(_ffn_kernel, final=final_g is not None),
        grid=(t // tm, D_FF // tf),
        in_specs=in_specs,
        out_specs=pl.BlockSpec((tm, d), lambda i, j: (i, 0)),
        out_shape=jax.ShapeDtypeStruct((t, d), F32),
        scratch_shapes=[pltpu.VMEM((tm, d), BF16)],
        compiler_params=_params("parallel", "arbitrary"),
        name="ffn",
    )(*args)


def _inproj_kernel(x_ref, ng_ref, sh_ref, sc_ref, w_ref, o_ref, h_scr):
    @pl.when(pl.program_id(1) == 0)
    def _():
        _norm_modulate_store(x_ref, ng_ref, sc_ref, sh_ref, h_scr)

    o_ref[...] = jnp.dot(h_scr[...], w_ref[...], preferred_element_type=F32)


def _inproj(x, mods, norm_g, l, stream, w_in):
    t, d = x.shape
    tm, tn = 1024, 512
    return pl.pallas_call(
        _inproj_kernel,
        grid=(t // tm, PROJ_W // tn),
        in_specs=[
            pl.BlockSpec((tm, d), lambda i, j: (i, 0)),
            pl.BlockSpec((1, 1, d), lambda i, j: (l * 3 + 1, 0, 0)),
            _mod_spec(l, 3, stream, tm),
            _mod_spec(l, 4, stream, tm),
            pl.BlockSpec((None, d, tn), lambda i, j: (l, 0, j)),
        ],
        out_specs=pl.BlockSpec((tm, tn), lambda i, j: (i, j)),
        out_shape=jax.ShapeDtypeStruct((t, PROJ_W), F32),
        scratch_shapes=[pltpu.VMEM((tm, d), BF16)],
        compiler_params=_params("parallel", "arbitrary"),
        name="inproj",
    )(x, norm_g, mods, mods, w_in)


def _expand_kv(ckv_bf, kr, wk_ref, wv_ref, k_out, v_out):
    kn = jnp.dot(ckv_bf, wk_ref[...], preferred_element_type=F32)
    v_out[...] = jnp.dot(ckv_bf, wv_ref[...], preferred_element_type=F32).astype(BF16)
    kr_bf = kr.astype(BF16)
    for h in range(MLA_HEADS):
        k_out[:, h * MLA_PAD:h * MLA_PAD + MLA_NOPE] = kn[:, h * MLA_NOPE:(h + 1) * MLA_NOPE].astype(BF16)
        k_out[:, h * MLA_PAD + MLA_NOPE:(h + 1) * MLA_PAD] = kr_bf


def _heads_kernel(cq_ref, gq_ref, ckv_ref, gk_ref, kr_ref, qn_ref, wqb_ref, kvn_ref, wk_ref, wv_ref,
                  gqn_ref, gkn_ref, *rest, rope):
    if rope:
        c64_ref, s64_ref, c128_ref, s128_ref = rest[:4]
        rest = rest[4:]
        rot64 = lambda v: _rope(v, c64_ref[...], s64_ref[...], MLA_ROPE // 4)
        rot128 = lambda v: _rope(v, c128_ref[...], s128_ref[...], HEAD_DIM // 4)
    else:
        rot64 = rot128 = lambda v: v
    q_out, k_out, v_out, ckv_out, kr_out, gq_out, gk_out = rest

    qn = _rms(cq_ref[...], qn_ref[0]).astype(BF16)
    q = jnp.dot(qn, wqb_ref[...], preferred_element_type=F32)
    for h in range(MLA_HEADS):
        lo = h * MLA_PAD
        q_out[:, lo:lo + MLA_NOPE] = q[:, lo:lo + MLA_NOPE].astype(BF16)
        q_out[:, lo + MLA_NOPE:lo + MLA_PAD] = rot64(q[:, lo + MLA_NOPE:lo + MLA_PAD]).astype(BF16)

    ckv = _rms(ckv_ref[...], kvn_ref[0])
    ckv_out[...] = ckv
    kr = rot64(kr_ref[...])
    kr_out[...] = kr
    _expand_kv(ckv.astype(BF16), kr, wk_ref, wv_ref, k_out, v_out)

    for h in range(GQA_HEADS):
        sl = slice(h * HEAD_DIM, (h + 1) * HEAD_DIM)
        gq_out[:, sl] = rot128(_rms(gq_ref[:, sl], gqn_ref[0])).astype(BF16)
    for h in range(GQA_KV_HEADS):
        sl = slice(h * HEAD_DIM, (h + 1) * HEAD_DIM)
        gk_out[:, sl] = rot128(_rms(gk_ref[:, sl], gkn_ref[0]))


def _heads(proj, l, wts, tables):
    t = proj.shape[0]
    tm = 512
    rope = tables is not None
    qw = MLA_HEADS * MLA_PAD
    vw = MLA_HEADS * HEAD_DIM
    in_specs = [
        pl.BlockSpec((tm, 512), lambda i: (i, COL_CQ // 512)),
        pl.BlockSpec((tm, 512), lambda i: (i, COL_GQ // 512)),
        pl.BlockSpec((tm, 256), lambda i: (i, COL_CKV // 256)),
        pl.BlockSpec((tm, 256), lambda i: (i, COL_GK // 256)),
        pl.BlockSpec((tm, 128), lambda i: (i, COL_KR // 128)),
        pl.BlockSpec((1, 1, Q_LORA), lambda i: (l, 0, 0)),
        pl.BlockSpec((None, Q_LORA, qw), lambda i: (l, 0, 0)),
        pl.BlockSpec((1, 1, KV_LORA), lambda i: (l, 0, 0)),
        pl.BlockSpec((None, KV_LORA, vw), lambda i: (l, 0, 0)),
        pl.BlockSpec((None, KV_LORA, vw), lambda i: (l, 0, 0)),
        pl.BlockSpec((1, 1, HEAD_DIM), lambda i: (l, 0, 0)),
        pl.BlockSpec((1, 1, HEAD_DIM), lambda i: (l, 0, 0)),
    ]
    args = [proj] * 5 + [wts["q_norm"], wts["wqb"], wts["kv_norm"], wts["wk"], wts["wv"],
                         wts["gq_norm"], wts["gk_norm"]]
    if rope:
        per = tables[0].shape[0] // tm
        in_specs += [pl.BlockSpec((tm, 128), lambda i: (i % per, 0))] * 4
        args += list(tables)
    row = lambda w: pl.BlockSpec((tm, w), lambda i: (i, 0))
    return pl.pallas_call(
        functools.partial(_heads_kernel, rope=rope),
        grid=(t // tm,),
        in_specs=in_specs,
        out_specs=[row(qw), row(qw), row(vw), row(KV_LORA), row(128), row(512), row(256)],
        out_shape=[
            jax.ShapeDtypeStruct((t, qw), BF16),
            jax.ShapeDtypeStruct((t, qw), BF16),
            jax.ShapeDtypeStruct((t, vw), BF16),
            jax.ShapeDtypeStruct((t, KV_LORA), F32),
            jax.ShapeDtypeStruct((t, 128), F32),
            jax.ShapeDtypeStruct((t, 512), BF16),
            jax.ShapeDtypeStruct((t, 256), F32),
        ],
        compiler_params=_params("parallel"),
        name="heads",
    )(*args)


def _cache_kv_kernel(ckv_ref, kr_ref, wk_ref, wv_ref, k_out, v_out):
    _expand_kv(ckv_ref[...].astype(BF16), kr_ref[...], wk_ref, wv_ref, k_out, v_out)


def _cache_kv(cache_ckv, cache_kr_pad, l, wts):
    nb, _, past, _ = cache_ckv.shape
    qw = MLA_HEADS * MLA_PAD
    vw = MLA_HEADS * HEAD_DIM
    return pl.pallas_call(
        _cache_kv_kernel,
        grid=(nb,),
        in_specs=[
            pl.BlockSpec((None, None, past, KV_LORA), lambda b: (b, l, 0, 0)),
            pl.BlockSpec((None, None, past, 128), lambda b: (b, l, 0, 0)),
            pl.BlockSpec((None, KV_LORA, vw), lambda b: (l, 0, 0)),
            pl.BlockSpec((None, KV_LORA, vw), lambda b: (l, 0, 0)),
        ],
        out_specs=[pl.BlockSpec((past, qw), lambda b: (b, 0)), pl.BlockSpec((past, vw), lambda b: (b, 0))],
        out_shape=[jax.ShapeDtypeStruct((nb * past, qw), BF16), jax.ShapeDtypeStruct((nb * past, vw), BF16)],
        compiler_params=_params("parallel"),
        name="cache_kv",
    )(cache_ckv, cache_kr_pad, wts["wk"], wts["wv"])


def _attn_kernel(*refs, n_heads, group, dk, dv, scale, seg_bias):
    q_ref = refs[0]
    pos = 1
    segs = []
    for has_bias in seg_bias:
        b_ref = refs[pos + 2] if has_bias else None
        segs.append((refs[pos], refs[pos + 1], b_ref))
        pos += 3 if has_bias else 2
    o_ref = refs[pos]
    for h in range(n_heads):
        hk = h // group
        q = q_ref[:, h * dk:(h + 1) * dk].astype(BF16)
        scores = []
        for k_ref, _, b_ref in segs:
            k = k_ref[:, hk * dk:(hk + 1) * dk].astype(BF16)
            s = lax.dot_general(q, k, (((1,), (1,)), ((), ())), preferred_element_type=F32) * scale
            if b_ref is not None:
                s = s + b_ref[h]
            scores.append(s)
        m = functools.reduce(jnp.maximum, [s.max(axis=-1, keepdims=True) for s in scores])
        probs = [jnp.exp(s - m) for s in scores]
        denom = functools.reduce(jnp.add, [p.sum(axis=-1, keepdims=True) for p in probs])
        o = None
        for p, (_, v_ref, _) in zip(probs, segs):
            v = v_ref[:, hk * dv:(hk + 1) * dv].astype(BF16)
            pv = jnp.dot(p.astype(BF16), v, preferred_element_type=F32)
            o = pv if o is None else o + pv
        o_ref[:, h * dv:(h + 1) * dv] = (o / denom).astype(o_ref.dtype)


def _attention(q, segs, *, t, grid, tq, q_index, n_heads, group, dk, dv, scale, name):
    in_specs = [pl.BlockSpec((tq, n_heads * dk), q_index)]
    args = [q]
    seg_bias = []
    for k_arr, k_spec, v_arr, v_spec, b_arr, b_spec in segs:
        in_specs += [k_spec, v_spec]
        args += [k_arr, v_arr]
        seg_bias.append(b_arr is not None)
        if b_arr is not None:
            in_specs.append(b_spec)
            args.append(b_arr)
    out_index = lambda *g: (q_index(*g)[0], 0)
    return pl.pallas_call(
        functools.partial(_attn_kernel, n_heads=n_heads, group=group, dk=dk, dv=dv, scale=scale,
                          seg_bias=tuple(seg_bias)),
        grid=grid,
        in_specs=in_specs,
        out_specs=pl.BlockSpec((tq, n_heads * dv), out_index),
        out_shape=jax.ShapeDtypeStruct((t, n_heads * dv), BF16),
        compiler_params=_params(*(("parallel",) * len(grid))),
        name=name,
    )(*args)


def _context_attention(proj, q_mla, k_mla, v_mla, gq, gk, seq):
    t = proj.shape[0]
    grid = (t // seq,)
    blk = lambda w, c: pl.BlockSpec((seq, w), lambda b: (b, c))
    common = dict(t=t, grid=grid, tq=seq)
    o_a = _attention(q_mla, [(k_mla, blk(MLA_HEADS * MLA_PAD, 0), v_mla, blk(MLA_HEADS * HEAD_DIM, 0), None, None)],
                     q_index=lambda b: (b, 0), n_heads=MLA_HEADS, group=1, dk=MLA_PAD, dv=HEAD_DIM,
                     scale=MLA_QK ** -0.5, name="ctx_mla", **common)
    o_b = _attention(proj, [(proj, blk(512, COL_NAK // 512), proj, blk(512, COL_NAV // 512), None, None)],
                     q_index=lambda b: (b, COL_NAQ // 512), n_heads=NA_HEADS, group=1, dk=HEAD_DIM, dv=HEAD_DIM,
                     scale=HEAD_DIM ** -0.5, name="ctx_na", **common)
    o_c = _attention(gq, [(gk, blk(256, 0), proj, blk(256, COL_GV // 256), None, None)],
                     q_index=lambda b: (b, 0), n_heads=GQA_HEADS, group=GQA_HEADS // GQA_KV_HEADS,
                     dk=HEAD_DIM, dv=HEAD_DIM, scale=HEAD_DIM ** -0.5, name="ctx_gqa", **common)
    return o_a, o_b, o_c


def _latent_attention(proj, q_mla, k_mla, v_mla, gq, gk, kc_mla, vc_mla, caches, bias, l, seq):
    cache_na_k, cache_na_v, cache_gqa_k, cache_gqa_v = caches
    t = proj.shape[0]
    past = cache_na_k.shape[2]
    tq = 512
    nq = seq // tq
    grid = (t // seq, nq)
    own = lambda w, c: pl.BlockSpec((seq, w), lambda b, i: (b, c))
    flat = lambda w: pl.BlockSpec((past, w), lambda b, i: (b, 0))
    cached = lambda w: pl.BlockSpec((None, None, past, w), lambda b, i: (b, l, 0, 0))
    common = dict(t=t, grid=grid, tq=tq)
    qw, vw = MLA_HEADS * MLA_PAD, MLA_HEADS * HEAD_DIM
    o_a = _attention(q_mla, [(kc_mla, flat(qw), vc_mla, flat(vw), None, None),
                             (k_mla, own(qw, 0), v_mla, own(vw, 0), None, None)],
                     q_index=lambda b, i: (b * nq + i, 0), n_heads=MLA_HEADS, group=1, dk=MLA_PAD, dv=HEAD_DIM,
                     scale=MLA_QK ** -0.5, name="lat_mla", **common)
    bias_spec = pl.BlockSpec((NA_HEADS, tq, seq), lambda b, i: (0, i, 0))
    o_b = _attention(proj, [(cache_na_k, cached(512), cache_na_v, cached(512), None, None),
                            (proj, own(512, COL_NAK // 512), proj, own(512, COL_NAV // 512), bias, bias_spec)],
                     q_index=lambda b, i: (b * nq + i, COL_NAQ // 512), n_heads=NA_HEADS, group=1,
                     dk=HEAD_DIM, dv=HEAD_DIM, scale=HEAD_DIM ** -0.5, name="lat_na", **common)
    o_c = _attention(gq, [(cache_gqa_k, cached(256), cache_gqa_v, cached(256), None, None),
                          (gk, own(256, 0), proj, own(256, COL_GV // 256), None, None)],
                     q_index=lambda b, i: (b * nq + i, 0), n_heads=GQA_HEADS, group=GQA_HEADS // GQA_KV_HEADS,
                     dk=HEAD_DIM, dv=HEAD_DIM, scale=HEAD_DIM ** -0.5, name="lat_gqa", **common)
    return o_a, o_b, o_c


def _outproj_kernel(x_ref, gt_ref, oa_ref, ob_ref, oc_ref, w_ref, o_ref):
    wa = oa_ref.shape[1]
    wb = ob_ref.shape[1]
    acc = jnp.dot(oa_ref[...], w_ref[0:wa, :], preferred_element_type=F32)
    acc += jnp.dot(ob_ref[...], w_ref[wa:wa + wb, :], preferred_element_type=F32)
    acc += jnp.dot(oc_ref[...], w_ref[wa + wb:, :], preferred_element_type=F32)
    o_ref[...] = x_ref[...] + gt_ref[0] * acc


def _outproj(x, mods, l, stream, o_a, o_b, o_c, w_out):
    t, d = x.shape
    tm, tn = 1024, 512
    row = lambda a: pl.BlockSpec((tm, a.shape[1]), lambda i, j: (i, 0))
    return pl.pallas_call(
        _outproj_kernel,
        grid=(t // tm, d // tn),
        in_specs=[
            pl.BlockSpec((tm, tn), lambda i, j: (i, j)),
            _mod_spec(l, 5, stream, tm, width=tn, tiled=True),
            row(o_a), row(o_b), row(o_c),
            pl.BlockSpec((None, d, tn), lambda i, j: (l, 0, j)),
        ],
        out_specs=pl.BlockSpec((tm, tn), lambda i, j: (i, j)),
        out_shape=jax.ShapeDtypeStruct((t, d), F32),
        compiler_params=_params("parallel", "parallel"),
        name="outproj",
    )(x, mods, o_a, o_b, o_c, w_out)


def _rope_tables(seq, d):
    quarter = d // 4
    tt = jnp.arange(seq)
    pos = jnp.stack([tt // GRID_W, tt % GRID_W], axis=-1).astype(jnp.float32)
    inv = ROPE_THETA ** (-jnp.arange(quarter, dtype=jnp.float32) / quarter)
    ang = pos[:, :, None] * inv
    cos = jnp.cos(ang)
    sin = jnp.sin(ang)
    cos_t = jnp.stack([cos, cos], axis=2).reshape(seq, d)
    sin_t = jnp.stack([-sin, sin], axis=2).reshape(seq, d)
    pad = 128 - d
    if pad:
        cos_t = jnp.concatenate([cos_t, jnp.ones((seq, pad), cos_t.dtype)], axis=-1)
        sin_t = jnp.concatenate([sin_t, jnp.zeros((seq, pad), sin_t.dtype)], axis=-1)
    return cos_t.astype(F32), sin_t.astype(F32)


def _na_bias_kernel(t_ref, o_ref, *, rows, wh):
    r = pl.program_id(1)
    shape = (GRID_W, 2 * GRID_W)
    c = lax.broadcasted_iota(jnp.int32, shape, 0)
    lane = lax.broadcasted_iota(jnp.int32, shape, 1)
    kc = lane % GRID_W
    rs = jnp.clip(r - wh // 2, 0, rows - wh)
    cs = jnp.clip(c - NA_WIN_W // 2, 0, GRID_W - NA_WIN_W)
    col_ok = (kc >= cs) & (kc < cs + NA_WIN_W)
    for j in range(rows // 2):
        kr = 2 * j + lane // GRID_W
        ok = col_ok & (kr >= rs) & (kr < rs + wh)
        d = jnp.clip(2 * j - r + NA_WIN_H, 0, 2 * NA_WIN_H - 1)
        pair = jnp.broadcast_to(t_ref[0, pl.ds(d, 1), :], shape)
        toeplitz = pltpu.roll(pair, 2 * GRID_W - (NA_WIN_W - 1), 1, stride=1, stride_axis=0)
        o_ref[0, :, j * 2 * GRID_W:(j + 1) * 2 * GRID_W] = jnp.where(ok, toeplitz, NEG_INF)


def _na_bias(rpb, seq):
    nh, nr, nc = rpb.shape
    rows = seq // GRID_W
    wh = min(NA_WIN_H, rows)
    padded = jnp.pad(rpb.astype(F32), ((0, 0), (1, 1), (0, GRID_W - nc)))
    table = jnp.concatenate([padded[:, :nr + 1], padded[:, 1:]], axis=-1)
    return pl.pallas_call(
        functools.partial(_na_bias_kernel, rows=rows, wh=wh),
        grid=(nh, rows),
        in_specs=[pl.BlockSpec((1, nr + 1, 2 * GRID_W), lambda h, r: (h, 0, 0))],
        out_specs=pl.BlockSpec((1, GRID_W, seq), lambda h, r: (h, r, 0)),
        out_shape=jax.ShapeDtypeStruct((nh, seq, seq), F32),
        compiler_params=_params("parallel", "parallel"),
        name="na_bias",
    )(table)


def kernel(x_prompt, x_sample, cache_mla_ckv, cache_mla_krope, cache_na_k, cache_na_v, cache_gqa_k, cache_gqa_v, c, c_ctx, ada_w, ada_b, norm_g, ffn_wg, ffn_wu, ffn_wd, w_in, mla_q_norm, mla_wqb, mla_kv_norm, mla_wkvb, na_rpb, gqa_q_norm, gqa_k_norm, w_out, final_norm):
    depth = ada_w.shape[0]
    nb_ctx, seq_ctx, d = x_prompt.shape
    nb_lat, seq_lat, _ = x_sample.shape
    past = cache_mla_ckv.shape[2]

    wg = ffn_wg.astype(BF16)
    wu = ffn_wu.astype(BF16)
    wd = ffn_wd.astype(BF16)
    s1, s2, s3 = Q_LORA, Q_LORA + KV_LORA, Q_LORA + KV_LORA + MLA_ROPE
    s4 = s3 + 3 * NA_HEADS * HEAD_DIM
    s5 = s4 + GQA_HEADS * HEAD_DIM
    n_in = w_in.shape[-1]
    w_in_p = jnp.concatenate(
        [w_in[..., 0:s1], w_in[..., s4:s5], w_in[..., s3:s4], w_in[..., s1:s2], w_in[..., s5:n_in],
         w_in[..., s2:s3], jnp.zeros((depth, d, PROJ_W - n_in), w_in.dtype)], axis=-1).astype(BF16)
    wqb = jnp.pad(mla_wqb.reshape(depth, Q_LORA, MLA_HEADS, MLA_QK),
                  ((0, 0), (0, 0), (0, 0), (0, MLA_PAD - MLA_QK))).reshape(depth, Q_LORA, MLA_HEADS * MLA_PAD)
    wkv = mla_wkvb.reshape(depth, KV_LORA, MLA_HEADS, 2 * HEAD_DIM)
    wts = {
        "q_norm": mla_q_norm.reshape(depth, 1, Q_LORA),
        "wqb": wqb.astype(BF16),
        "kv_norm": mla_kv_norm.reshape(depth, 1, KV_LORA),
        "wk": wkv[..., :MLA_NOPE].reshape(depth, KV_LORA, MLA_HEADS * MLA_NOPE).astype(BF16),
        "wv": wkv[..., MLA_NOPE:].reshape(depth, KV_LORA, MLA_HEADS * HEAD_DIM).astype(BF16),
        "gq_norm": gqa_q_norm.reshape(depth, 1, HEAD_DIM),
        "gk_norm": gqa_k_norm.reshape(depth, 1, HEAD_DIM),
    }
    w_out_b = w_out.astype(BF16)
    norm_g3 = norm_g.reshape(depth * 3, 1, d)
    final_g = final_norm.reshape(1, 1, d)
    tables = _rope_tables(seq_lat, MLA_ROPE) + _rope_tables(seq_lat, HEAD_DIM)
    kr_cache = jnp.pad(cache_mla_krope, ((0, 0), (0, 0), (0, 0), (0, 128 - MLA_ROPE)))
    caches = (cache_na_k.reshape(nb_lat, depth, past, NA_HEADS * HEAD_DIM),
              cache_na_v.reshape(nb_lat, depth, past, NA_HEADS * HEAD_DIM),
              cache_gqa_k.reshape(nb_lat, depth, past, GQA_KV_HEADS * HEAD_DIM),
              cache_gqa_v.reshape(nb_lat, depth, past, GQA_KV_HEADS * HEAD_DIM))

    cond = jnp.concatenate([c_ctx[None, :], c, jnp.zeros((MOD_ROWS - 1 - nb_lat, d), c.dtype)], axis=0)
    mods = _modulation(cond, ada_w, ada_b).reshape(depth * MOD_ROWS, 1, N_MOD * d)

    xp = x_prompt.reshape(nb_ctx * seq_ctx, d)
    xs = x_sample.reshape(nb_lat * seq_lat, d)
    ctx_row = (0, nb_ctx * seq_ctx)
    lat_row = (1, seq_lat)
    biases = [_na_bias(na_rpb[l], seq_lat) for l in range(depth)]

    states = []
    for l in range(depth):
        last = l == depth - 1
        xp = _ffn(xp, mods, norm_g3, l, 0, 0, ctx_row, wg, wu, wd)
        proj = _inproj(xp, mods, norm_g3, l, ctx_row, w_in_p)
        q_mla, k_mla, v_mla, ckv_n, kr, gq, gk = _heads(proj, l, wts, None)
        o_a, o_b, o_c = _context_attention(proj, q_mla, k_mla, v_mla, gq, gk, seq_ctx)
        xp = _outproj(xp, mods, l, ctx_row, o_a, o_b, o_c, w_out_b)
        xp = _ffn(xp, mods, norm_g3, l, 2, 1, ctx_row, wg, wu, wd, final_g if last else None)
        states.append((
            ckv_n.reshape(nb_ctx, seq_ctx, KV_LORA),
            kr[:, :MLA_ROPE].reshape(nb_ctx, seq_ctx, MLA_ROPE),
            proj[:, COL_NAK:COL_NAK + 512].reshape(nb_ctx, seq_ctx, NA_HEADS, HEAD_DIM),
            proj[:, COL_NAV:COL_NAV + 512].reshape(nb_ctx, seq_ctx, NA_HEADS, HEAD_DIM),
            gk.reshape(nb_ctx, seq_ctx, GQA_KV_HEADS, HEAD_DIM),
            proj[:, COL_GV:COL_GV + 256].reshape(nb_ctx, seq_ctx, GQA_KV_HEADS, HEAD_DIM),
        ))
        xs = _ffn(xs, mods, norm_g3, l, 0, 0, lat_row, wg, wu, wd)
        proj = _inproj(xs, mods, norm_g3, l, lat_row, w_in_p)
        q_mla, k_mla, v_mla, _, _, gq, gk = _heads(proj, l, wts, tables)
        kc_mla, vc_mla = _cache_kv(cache_mla_ckv, kr_cache, l, wts)
        o_a, o_b, o_c = _latent_attention(proj, q_mla, k_mla, v_mla, gq, gk, kc_mla, vc_mla, caches, biases[l], l,
                                          seq_lat)
        xs = _outproj(xs, mods, l, lat_row, o_a, o_b, o_c, w_out_b)
        xs = _ffn(xs, mods, norm_g3, l, 2, 1, lat_row, wg, wu, wd, final_g if last else None)

    y_prompt = xp.reshape(nb_ctx, seq_ctx, d)
    y_sample = xs.reshape(nb_lat, seq_lat, d)
    outs = tuple(jnp.stack([s[k] for s in states], axis=1) for k in range(6))
    return (y_prompt, y_sample) + outs
```

```python
import functools

import jax
import jax.numpy as jnp
from jax import lax
from jax.experimental import pallas as pl
from jax.experimental.pallas import tpu as pltpu

F32 = jnp.float32
BF16 = jnp.bfloat16

D_MODEL = 2048
D_FF = 5632
N_MOD = 9
GRID_W = 64
ROPE_THETA = 10000.0
HEAD_DIM = 128
MLA_HEADS = 8
MLA_NOPE = 128
MLA_ROPE = 64
MLA_QK = MLA_NOPE + MLA_ROPE
MLA_PAD = 256
Q_LORA = 512
KV_LORA = 256
NA_HEADS = 4
NA_WIN_H = 8
NA_WIN_W = 16
GQA_HEADS = 4
GQA_KV_HEADS = 2
EPS = 1e-6
NEG_INF = -1e30
MOD_ROWS = 16

COL_CQ = 0
COL_GQ = 512
COL_NAQ = 1024
COL_NAK = 1536
COL_NAV = 2048
COL_CKV = 2560
COL_GK = 2816
COL_GV = 3072
COL_KR = 3328
PROJ_W = 3584

VMEM_LIMIT = 56 * 1024 * 1024


def _params(*sem):
    return pltpu.CompilerParams(dimension_semantics=sem, vmem_limit_bytes=VMEM_LIMIT)


def _rms(x, g):
    return x * lax.rsqrt(jnp.mean(x * x, axis=-1, keepdims=True) + EPS) * g


NORM_ROWS = 256


def _norm_modulate_store(x_ref, ng_ref, sc_ref, sh_ref, h_scr):
    gain = ng_ref[0] * (1.0 + sc_ref[0])
    shift = sh_ref[0]

    def body(i, carry):
        rows = pl.ds(pl.multiple_of(i * NORM_ROWS, NORM_ROWS), NORM_ROWS)
        x = x_ref[rows, :]
        r = lax.rsqrt(jnp.mean(x * x, axis=-1, keepdims=True) + EPS)
        h_scr[rows, :] = (x * r * gain + shift).astype(BF16)
        return carry

    lax.fori_loop(0, x_ref.shape[0] // NORM_ROWS, body, 0)


def _rope(x, cos, sin, quarter):
    n = x.shape[-1]
    lane = lax.broadcasted_iota(jnp.int32, x.shape, 1)
    first = (lane % (2 * quarter)) < quarter
    sw = jnp.where(first, pltpu.roll(x, n - quarter, 1), pltpu.roll(x, quarter, 1))
    return x * cos + sw * sin


def _mod_kernel(c_ref, w_ref, b_ref, o_ref):
    c = c_ref[...]
    s = (c * jax.nn.sigmoid(c)).astype(BF16)
    o_ref[0] = jnp.dot(s, w_ref[0].astype(BF16), preferred_element_type=F32) + b_ref[0]


def _modulation(cond, ada_w, ada_b):
    depth, d, n = ada_w.shape
    tn = 512
    return pl.pallas_call(
        _mod_kernel,
        grid=(depth, n // tn),
        in_specs=[
            pl.BlockSpec((MOD_ROWS, d), lambda l, j: (0, 0)),
            pl.BlockSpec((1, d, tn), lambda l, j: (l, 0, j)),
            pl.BlockSpec((1, 1, tn), lambda l, j: (l, 0, j)),
        ],
        out_specs=pl.BlockSpec((1, MOD_ROWS, tn), lambda l, j: (l, 0, j)),
        out_shape=jax.ShapeDtypeStruct((depth, MOD_ROWS, n), F32),
        compiler_params=_params("parallel", "parallel"),
        name="modulation",
    )(cond, ada_w, ada_b.reshape(depth, 1, n))


def _mod_spec(l, chunk, stream, tm, width=D_MODEL, tiled=False):
    base, tokens = stream
    per = D_MODEL // width
    row = lambda i: l * MOD_ROWS + base + (i * tm) // tokens
    if tiled:
        return pl.BlockSpec((1, 1, width), lambda i, j: (row(i), 0, chunk * per + j))
    return pl.BlockSpec((1, 1, width), lambda i, *_: (row(i), 0, chunk * per))


def _ffn_kernel(x_ref, ng_ref, sh_ref, sc_ref, gt_ref, wg_ref, wu_ref, wd_ref, *rest, final):
    if final:
        fg_ref, o_ref, h_scr = rest
    else:
        o_ref, h_scr = rest
    f = pl.program_id(1)

    @pl.when(f == 0)
    def _():
        _norm_modulate_store(x_ref, ng_ref, sc_ref, sh_ref, h_scr)
        o_ref[...] = jnp.zeros_like(o_ref)

    h = h_scr[...]
    g = jnp.dot(h, wg_ref[...], preferred_element_type=F32)
    u = jnp.dot(h, wu_ref[...], preferred_element_type=F32)
    a = (g * jax.nn.sigmoid(g) * u).astype(BF16)
    o_ref[...] += jnp.dot(a, wd_ref[...], preferred_element_type=F32)

    @pl.when(f == pl.num_programs(1) - 1)
    def _():
        y = x_ref[...] + 0.5 * gt_ref[0] * o_ref[...]
        if final:
            y = _rms(y, fg_ref[0])
        o_ref[...] = y


def _ffn(x, mods, norm_g, l, sub, k, stream, wg, wu, wd, final_g=None):
    t, d = x.shape
    tm, tf = 1024, 512
    chunk = 3 * sub
    in_specs = [
        pl.BlockSpec((tm, d), lambda i, j: (i, 0), pipeline_mode=pl.Buffered(1)),
        pl.BlockSpec((1, 1, d), lambda i, j: (l * 3 + sub, 0, 0)),
        _mod_spec(l, chunk, stream, tm),
        _mod_spec(l, chunk + 1, stream, tm),
        _mod_spec(l, chunk + 2, stream, tm),
        pl.BlockSpec((None, None, d, tf), lambda i, j: (l, k, 0, j)),
        pl.BlockSpec((None, None, d, tf), lambda i, j: (l, k, 0, j)),
        pl.BlockSpec((None, None, tf, d), lambda i, j: (l, k, j, 0)),
    ]
    args = [x, norm_g, mods, mods, mods, wg, wu, wd]
    if final_g is not None:
        in_specs.append(pl.BlockSpec((1, 1, d), lambda i, j: (0, 0, 0)))
        args.append(final_g)
    return pl.pallas_call(
        functools.partial(_ffn_kernel, final=final_g is not None),
        grid=(t // tm, D_FF // tf),
        in_specs=in_specs,
        out_specs=pl.BlockSpec((tm, d), lambda i, j: (i, 0)),
        out_shape=jax.ShapeDtypeStruct((t, d), F32),
        scratch_shapes=[pltpu.VMEM((tm, d), BF16)],
        compiler_params=_params("parallel", "arbitrary"),
        name="ffn",
    )(*args)


def _inproj_kernel(x_ref, ng_ref, sh_ref, sc_ref, w_ref, o_ref, h_scr):
    _norm_modulate_store(x_ref, ng_ref, sc_ref, sh_ref, h_scr)
    o_ref[...] = jnp.dot(h_scr[...], w_ref[...], preferred_element_type=F32)


def _inproj(x, mods, norm_g, l, stream, w_in):
    t, d = x.shape
    tm = 512
    return pl.pallas_call(
        _inproj_kernel,
        grid=(t // tm,),
        in_specs=[
            pl.BlockSpec((tm, d), lambda i: (i, 0)),
            pl.BlockSpec((1, 1, d), lambda i: (l * 3 + 1, 0, 0)),
            _mod_spec(l, 3, stream, tm),
            _mod_spec(l, 4, stream, tm),
            pl.BlockSpec((None, d, PROJ_W), lambda i: (l, 0, 0), pipeline_mode=pl.Buffered(1)),
        ],
        out_specs=pl.BlockSpec((tm, PROJ_W), lambda i: (i, 0)),
        out_shape=jax.ShapeDtypeStruct((t, PROJ_W), F32),
        scratch_shapes=[pltpu.VMEM((tm, d), BF16)],
        compiler_params=_params("parallel"),
        name="inproj",
    )(x, norm_g, mods, mods, w_in)


def _expand_kv(ckv_bf, kr, wk_ref, wv_ref, k_out, v_out):
    kn = jnp.dot(ckv_bf, wk_ref[...], preferred_element_type=F32)
    v_out[...] = jnp.dot(ckv_bf, wv_ref[...], preferred_element_type=F32).astype(BF16)
    kr_bf = kr.astype(BF16)
    for h in range(MLA_HEADS):
        k_out[:, h * MLA_PAD:h * MLA_PAD + MLA_NOPE] = kn[:, h * MLA_NOPE:(h + 1) * MLA_NOPE].astype(BF16)
        k_out[:, h * MLA_PAD + MLA_NOPE:(h + 1) * MLA_PAD] = kr_bf


def _heads_kernel(cq_ref, gq_ref, ckv_ref, gk_ref, kr_ref, qn_ref, wqb_ref, kvn_ref, wk_ref, wv_ref,
                  gqn_ref, gkn_ref, *rest, rope):
    if rope:
        c64_ref, s64_ref, c128_ref, s128_ref = rest[:4]
        rest = rest[4:]
        rot64 = lambda v: _rope(v, c64_ref[...], s64_ref[...], MLA_ROPE // 4)
        rot128 = lambda v: _rope(v, c128_ref[...], s128_ref[...], HEAD_DIM // 4)
    else:
        rot64 = rot128 = lambda v: v
    q_out, k_out, v_out, ckv_out, kr_out, gq_out, gk_out = rest

    qn = _rms(cq_ref[...], qn_ref[0]).astype(BF16)
    q = jnp.dot(qn, wqb_ref[...], preferred_element_type=F32)
    for h in range(MLA_HEADS):
        lo = h * MLA_PAD
        q_out[:, lo:lo + MLA_NOPE] = q[:, lo:lo + MLA_NOPE].astype(BF16)
        q_out[:, lo + MLA_NOPE:lo + MLA_PAD] = rot64(q[:, lo + MLA_NOPE:lo + MLA_PAD]).astype(BF16)

    ckv = _rms(ckv_ref[...], kvn_ref[0])
    ckv_out[...] = ckv
    kr = rot64(kr_ref[...])
    kr_out[...] = kr
    _expand_kv(ckv.astype(BF16), kr, wk_ref, wv_ref, k_out, v_out)

    for h in range(GQA_HEADS):
        sl = slice(h * HEAD_DIM, (h + 1) * HEAD_DIM)
        gq_out[:, sl] = rot128(_rms(gq_ref[:, sl], gqn_ref[0])).astype(BF16)
    for h in range(GQA_KV_HEADS):
        sl = slice(h * HEAD_DIM, (h + 1) * HEAD_DIM)
        gk_out[:, sl] = rot128(_rms(gk_ref[:, sl], gkn_ref[0]))


def _heads(proj, l, wts, tables):
    t = proj.shape[0]
    tm = 512
    rope = tables is not None
    qw = MLA_HEADS * MLA_PAD
    vw = MLA_HEADS * HEAD_DIM
    in_specs = [
        pl.BlockSpec((tm, 512), lambda i: (i, COL_CQ // 512)),
        pl.BlockSpec((tm, 512), lambda i: (i, COL_GQ // 512)),
        pl.BlockSpec((tm, 256), lambda i: (i, COL_CKV // 256)),
        pl.BlockSpec((tm, 256), lambda i: (i, COL_GK // 256)),
        pl.BlockSpec((tm, 128), lambda i: (i, COL_KR // 128)),
        pl.BlockSpec((1, 1, Q_LORA), lambda i: (l, 0, 0)),
        pl.BlockSpec((None, Q_LORA, qw), lambda i: (l, 0, 0)),
        pl.BlockSpec((1, 1, KV_LORA), lambda i: (l, 0, 0)),
        pl.BlockSpec((None, KV_LORA, vw), lambda i: (l, 0, 0)),
        pl.BlockSpec((None, KV_LORA, vw), lambda i: (l, 0, 0)),
        pl.BlockSpec((1, 1, HEAD_DIM), lambda i: (l, 0, 0)),
        pl.BlockSpec((1, 1, HEAD_DIM), lambda i: (l, 0, 0)),
    ]
    args = [proj] * 5 + [wts["q_norm"], wts["wqb"], wts["kv_norm"], wts["wk"], wts["wv"],
                         wts["gq_norm"], wts["gk_norm"]]
    if rope:
        per = tables[0].shape[0] // tm
        in_specs += [pl.BlockSpec((tm, 128), lambda i: (i % per, 0))] * 4
        args += list(tables)
    row = lambda w: pl.BlockSpec((tm, w), lambda i: (i, 0))
    return pl.pallas_call(
        functools.partial(_heads_kernel, rope=rope),
        grid=(t // tm,),
        in_specs=in_specs,
        out_specs=[row(qw), row(qw), row(vw), row(KV_LORA), row(128), row(512), row(256)],
        out_shape=[
            jax.ShapeDtypeStruct((t, qw), BF16),
            jax.ShapeDtypeStruct((t, qw), BF16),
            jax.ShapeDtypeStruct((t, vw), BF16),
            jax.ShapeDtypeStruct((t, KV_LORA), F32),
            jax.ShapeDtypeStruct((t, 128), F32),
            jax.ShapeDtypeStruct((t, 512), BF16),
            jax.ShapeDtypeStruct((t, 256), F32),
        ],
        compiler_params=_params("parallel"),
        name="heads",
    )(*args)


def _cache_kv_kernel(ckv_ref, kr_ref, wk_ref, wv_ref, k_out, v_out):
    _expand_kv(ckv_ref[...].astype(BF16), kr_ref[...], wk_ref, wv_ref, k_out, v_out)


def _cache_kv(cache_ckv, cache_kr_pad, l, wts):
    nb, _, past, _ = cache_ckv.shape
    qw = MLA_HEADS * MLA_PAD
    vw = MLA_HEADS * HEAD_DIM
    return pl.pallas_call(
        _cache_kv_kernel,
        grid=(nb,),
        in_specs=[
            pl.BlockSpec((None, None, past, KV_LORA), lambda b: (b, l, 0, 0)),
            pl.BlockSpec((None, None, past, 128), lambda b: (b, l, 0, 0)),
            pl.BlockSpec((None, KV_LORA, vw), lambda b: (l, 0, 0)),
            pl.BlockSpec((None, KV_LORA, vw), lambda b: (l, 0, 0)),
        ],
        out_specs=[pl.BlockSpec((past, qw), lambda b: (b, 0)), pl.BlockSpec((past, vw), lambda b: (b, 0))],
        out_shape=[jax.ShapeDtypeStruct((nb * past, qw), BF16), jax.ShapeDtypeStruct((nb * past, vw), BF16)],
        compiler_params=_params("parallel"),
        name="cache_kv",
    )(cache_ckv, cache_kr_pad, wts["wk"], wts["wv"])


def _attn_kernel(*refs, n_heads, group, dk, dv, scale, seg_bias, n_sub):
    q_ref = refs[0]
    pos = 1
    segs = []
    for has_bias in seg_bias:
        b_ref = refs[pos + 2] if has_bias else None
        segs.append((refs[pos], refs[pos + 1], b_ref))
        pos += 3 if has_bias else 2
    o_ref = refs[pos]

    def attend(q_rows, seg_rows):
        for h in range(n_heads):
            hk = h // group
            q = q_ref[q_rows, h * dk:(h + 1) * dk].astype(BF16)
            scores = []
            for (k_ref, _, b_ref), rows in zip(segs, seg_rows):
                k = k_ref[rows, hk * dk:(hk + 1) * dk].astype(BF16)
                s = lax.dot_general(q, k, (((1,), (1,)), ((), ())), preferred_element_type=F32) * scale
                if b_ref is not None:
                    s = s + b_ref[h]
                scores.append(s)
            m = functools.reduce(jnp.maximum, [s.max(axis=-1, keepdims=True) for s in scores])
            probs = [jnp.exp(s - m) for s in scores]
            denom = functools.reduce(jnp.add, [p.sum(axis=-1, keepdims=True) for p in probs])
            o = None
            for p, (_, v_ref, _), rows in zip(probs, segs, seg_rows):
                v = v_ref[rows, hk * dv:(hk + 1) * dv].astype(BF16)
                pv = jnp.dot(p.astype(BF16), v, preferred_element_type=F32)
                o = pv if o is None else o + pv
            o_ref[q_rows, h * dv:(h + 1) * dv] = (o / denom).astype(o_ref.dtype)

    if n_sub == 1:
        attend(slice(None), [slice(None)] * len(segs))
    else:
        def body(s, carry):
            rows = lambda ref: pl.ds(pl.multiple_of(s * (ref.shape[0] // n_sub), ref.shape[0] // n_sub),
                                     ref.shape[0] // n_sub)
            attend(rows(q_ref), [rows(k_ref) for k_ref, _, _ in segs])
            return carry

        lax.fori_loop(0, n_sub, body, 0)


def _attention(q, segs, *, t, grid, tq, q_index, n_heads, group, dk, dv, scale, name, n_sub=1):
    assert n_sub == 1 or all(s[4] is None for s in segs)
    in_specs = [pl.BlockSpec((tq, n_heads * dk), q_index)]
    args = [q]
    seg_bias = []
    for k_arr, k_spec, v_arr, v_spec, b_arr, b_spec in segs:
        in_specs += [k_spec, v_spec]
        args += [k_arr, v_arr]
        seg_bias.append(b_arr is not None)
        if b_arr is not None:
            in_specs.append(b_spec)
            args.append(b_arr)
    out_index = lambda *g: (q_index(*g)[0], 0)
    return pl.pallas_call(
        functools.partial(_attn_kernel, n_heads=n_heads, group=group, dk=dk, dv=dv, scale=scale,
                          seg_bias=tuple(seg_bias), n_sub=n_sub),
        grid=grid,
        in_specs=in_specs,
        out_specs=pl.BlockSpec((tq, n_heads * dv), out_index),
        out_shape=jax.ShapeDtypeStruct((t, n_heads * dv), BF16),
        compiler_params=_params(*(("parallel",) * len(grid))),
        name=name,
    )(*args)


def _context_attention(proj, q_mla, k_mla, v_mla, gq, gk, seq):
    t = proj.shape[0]
    n_sub = 4
    rows = seq * n_sub
    grid = (t // rows,)
    blk = lambda w, c: pl.BlockSpec((rows, w), lambda b: (b, c))
    common = dict(t=t, grid=grid, tq=rows, n_sub=n_sub)
    o_a = _attention(q_mla, [(k_mla, blk(MLA_HEADS * MLA_PAD, 0), v_mla, blk(MLA_HEADS * HEAD_DIM, 0), None, None)],
                     q_index=lambda b: (b, 0), n_heads=MLA_HEADS, group=1, dk=MLA_PAD, dv=HEAD_DIM,
                     scale=MLA_QK ** -0.5, name="ctx_mla", **common)
    o_b = _attention(proj, [(proj, blk(512, COL_NAK // 512), proj, blk(512, COL_NAV // 512), None, None)],
                     q_index=lambda b: (b, COL_NAQ // 512), n_heads=NA_HEADS, group=1, dk=HEAD_DIM, dv=HEAD_DIM,
                     scale=HEAD_DIM ** -0.5, name="ctx_na", **common)
    o_c = _attention(gq, [(gk, blk(256, 0), proj, blk(256, COL_GV // 256), None, None)],
                     q_index=lambda b: (b, 0), n_heads=GQA_HEADS, group=GQA_HEADS // GQA_KV_HEADS,
                     dk=HEAD_DIM, dv=HEAD_DIM, scale=HEAD_DIM ** -0.5, name="ctx_gqa", **common)
    return o_a, o_b, o_c


def _latent_attention(proj, q_mla, k_mla, v_mla, gq, gk, kc_mla, vc_mla, caches, bias, l, seq):
    cache_na_k, cache_na_v, cache_gqa_k, cache_gqa_v = caches
    t = proj.shape[0]
    past = cache_na_k.shape[2]
    own = lambda w, c: pl.BlockSpec((seq, w), lambda b, i: (b, c))
    flat = lambda w: pl.BlockSpec((past, w), lambda b, i: (b, 0))
    cached = lambda w: pl.BlockSpec((None, None, past, w), lambda b, i: (b, l, 0, 0))
    qw, vw = MLA_HEADS * MLA_PAD, MLA_HEADS * HEAD_DIM
    o_c = _attention(gq, [(cache_gqa_k, cached(256), cache_gqa_v, cached(256), None, None),
                          (gk, own(256, 0), proj, own(256, COL_GV // 256), None, None)],
                     q_index=lambda b, i: (b, 0), n_heads=GQA_HEADS, group=GQA_HEADS // GQA_KV_HEADS,
                     dk=HEAD_DIM, dv=HEAD_DIM, scale=HEAD_DIM ** -0.5, name="lat_gqa",
                     t=t, grid=(t // seq, 1), tq=seq)
    tq = 512
    nq = seq // tq
    common = dict(t=t, grid=(t // seq, nq), tq=tq)
    o_a = _attention(q_mla, [(kc_mla, flat(qw), vc_mla, flat(vw), None, None),
                             (k_mla, own(qw, 0), v_mla, own(vw, 0), None, None)],
                     q_index=lambda b, i: (b * nq + i, 0), n_heads=MLA_HEADS, group=1, dk=MLA_PAD, dv=HEAD_DIM,
                     scale=MLA_QK ** -0.5, name="lat_mla", **common)
    bias_spec = pl.BlockSpec((NA_HEADS, tq, seq), lambda b, i: (0, i, 0))
    o_b = _attention(proj, [(cache_na_k, cached(512), cache_na_v, cached(512), None, None),
                            (proj, own(512, COL_NAK // 512), proj, own(512, COL_NAV // 512), bias, bias_spec)],
                     q_index=lambda b, i: (b * nq + i, COL_NAQ // 512), n_heads=NA_HEADS, group=1,
                     dk=HEAD_DIM, dv=HEAD_DIM, scale=HEAD_DIM ** -0.5, name="lat_na", **common)
    return o_a, o_b, o_c


def _outproj_kernel(x_ref, gt_ref, oa_ref, ob_ref, oc_ref, w_ref, o_ref):
    wa = oa_ref.shape[1]
    wb = ob_ref.shape[1]
    acc = jnp.dot(oa_ref[...], w_ref[0:wa, :], preferred_element_type=F32)
    acc += jnp.dot(ob_ref[...], w_ref[wa:wa + wb, :], preferred_element_type=F32)
    acc += jnp.dot(oc_ref[...], w_ref[wa + wb:, :], preferred_element_type=F32)
    o_ref[...] = x_ref[...] + gt_ref[0] * acc


def _outproj(x, mods, l, stream, o_a, o_b, o_c, w_out):
    t, d = x.shape
    tm = 512
    row = lambda a: pl.BlockSpec((tm, a.shape[1]), lambda i: (i, 0))
    return pl.pallas_call(
        _outproj_kernel,
        grid=(t // tm,),
        in_specs=[
            row(x),
            _mod_spec(l, 5, stream, tm),
            row(o_a), row(o_b), row(o_c),
            pl.BlockSpec((None, d, d), lambda i: (l, 0, 0), pipeline_mode=pl.Buffered(1)),
        ],
        out_specs=row(x),
        out_shape=jax.ShapeDtypeStruct((t, d), F32),
        compiler_params=_params("parallel"),
        name="outproj",
    )(x, mods, o_a, o_b, o_c, w_out)


def _rope_tables(seq, d):
    quarter = d // 4
    tt = jnp.arange(seq)
    pos = jnp.stack([tt // GRID_W, tt % GRID_W], axis=-1).astype(jnp.float32)
    inv = ROPE_THETA ** (-jnp.arange(quarter, dtype=jnp.float32) / quarter)
    ang = pos[:, :, None] * inv
    cos = jnp.cos(ang)
    sin = jnp.sin(ang)
    cos_t = jnp.stack([cos, cos], axis=2).reshape(seq, d)
    sin_t = jnp.stack([-sin, sin], axis=2).reshape(seq, d)
    pad = 128 - d
    if pad:
        cos_t = jnp.concatenate([cos_t, jnp.ones((seq, pad), cos_t.dtype)], axis=-1)
        sin_t = jnp.concatenate([sin_t, jnp.zeros((seq, pad), sin_t.dtype)], axis=-1)
    return cos_t.astype(F32), sin_t.astype(F32)


def _na_bias_kernel(t_ref, o_ref, *, rows, wh):
    r = pl.program_id(1)
    shape = (GRID_W, 2 * GRID_W)
    c = lax.broadcasted_iota(jnp.int32, shape, 0)
    lane = lax.broadcasted_iota(jnp.int32, shape, 1)
    kc = lane % GRID_W
    rs = jnp.clip(r - wh // 2, 0, rows - wh)
    cs = jnp.clip(c - NA_WIN_W // 2, 0, GRID_W - NA_WIN_W)
    col_ok = (kc >= cs) & (kc < cs + NA_WIN_W)
    for j in range(rows // 2):
        kr = 2 * j + lane // GRID_W
        ok = col_ok & (kr >= rs) & (kr < rs + wh)
        d = jnp.clip(2 * j - r + NA_WIN_H, 0, 2 * NA_WIN_H - 1)
        pair = jnp.broadcast_to(t_ref[0, pl.ds(d, 1), :], shape)
        toeplitz = pltpu.roll(pair, 2 * GRID_W - (NA_WIN_W - 1), 1, stride=1, stride_axis=0)
        o_ref[0, :, j * 2 * GRID_W:(j + 1) * 2 * GRID_W] = jnp.where(ok, toeplitz, NEG_INF)


def _na_bias(rpb, seq):
    nh, nr, nc = rpb.shape
    rows = seq // GRID_W
    wh = min(NA_WIN_H, rows)
    padded = jnp.pad(rpb.astype(F32), ((0, 0), (1, 1), (0, GRID_W - nc)))
    table = jnp.concatenate([padded[:, :nr + 1], padded[:, 1:]], axis=-1)
    return pl.pallas_call(
        functools.partial(_na_bias_kernel, rows=rows, wh=wh),
        grid=(nh, rows),
        in_specs=[pl.BlockSpec((1, nr + 1, 2 * GRID_W), lambda h, r: (h, 0, 0))],
        out_specs=pl.BlockSpec((1, GRID_W, seq), lambda h, r: (h, r, 0)),
        out_shape=jax.ShapeDtypeStruct((nh, seq, seq), F32),
        compiler_params=_params("parallel", "parallel"),
        name="na_bias",
    )(table)


def kernel(x_prompt, x_sample, cache_mla_ckv, cache_mla_krope, cache_na_k, cache_na_v, cache_gqa_k, cache_gqa_v, c, c_ctx, ada_w, ada_b, norm_g, ffn_wg, ffn_wu, ffn_wd, w_in, mla_q_norm, mla_wqb, mla_kv_norm, mla_wkvb, na_rpb, gqa_q_norm, gqa_k_norm, w_out, final_norm):
    depth = ada_w.shape[0]
    nb_ctx, seq_ctx, d = x_prompt.shape
    nb_lat, seq_lat, _ = x_sample.shape
    past = cache_mla_ckv.shape[2]

    wg = ffn_wg.astype(BF16)
    wu = ffn_wu.astype(BF16)
    wd = ffn_wd.astype(BF16)
    s1, s2, s3 = Q_LORA, Q_LORA + KV_LORA, Q_LORA + KV_LORA + MLA_ROPE
    s4 = s3 + 3 * NA_HEADS * HEAD_DIM
    s5 = s4 + GQA_HEADS * HEAD_DIM
    n_in = w_in.shape[-1]
    w_in_p = jnp.concatenate(
        [w_in[..., 0:s1], w_in[..., s4:s5], w_in[..., s3:s4], w_in[..., s1:s2], w_in[..., s5:n_in],
         w_in[..., s2:s3], jnp.zeros((depth, d, PROJ_W - n_in), w_in.dtype)], axis=-1).astype(BF16)
    wqb = jnp.pad(mla_wqb.reshape(depth, Q_LORA, MLA_HEADS, MLA_QK),
                  ((0, 0), (0, 0), (0, 0), (0, MLA_PAD - MLA_QK))).reshape(depth, Q_LORA, MLA_HEADS * MLA_PAD)
    wkv = mla_wkvb.reshape(depth, KV_LORA, MLA_HEADS, 2 * HEAD_DIM)
    wts = {
        "q_norm": mla_q_norm.reshape(depth, 1, Q_LORA),
        "wqb": wqb.astype(BF16),
        "kv_norm": mla_kv_norm.reshape(depth, 1, KV_LORA),
        "wk": wkv[..., :MLA_NOPE].reshape(depth, KV_LORA, MLA_HEADS * MLA_NOPE).astype(BF16),
        "wv": wkv[..., MLA_NOPE:].reshape(depth, KV_LORA, MLA_HEADS * HEAD_DIM).astype(BF16),
        "gq_norm": gqa_q_norm.reshape(depth, 1, HEAD_DIM),
        "gk_norm": gqa_k_norm.reshape(depth, 1, HEAD_DIM),
    }
    w_out_b = w_out.astype(BF16)
    norm_g3 = norm_g.reshape(depth * 3, 1, d)
    final_g = final_norm.reshape(1, 1, d)
    tables = _rope_tables(seq_lat, MLA_ROPE) + _rope_tables(seq_lat, HEAD_DIM)
    kr_cache = jnp.pad(cache_mla_krope, ((0, 0), (0, 0), (0, 0), (0, 128 - MLA_ROPE)))
    caches = (cache_na_k.reshape(nb_lat, depth, past, NA_HEADS * HEAD_DIM),
              cache_na_v.reshape(nb_lat, depth, past, NA_HEADS * HEAD_DIM),
              cache_gqa_k.reshape(nb_lat, depth, past, GQA_KV_HEADS * HEAD_DIM),
              cache_gqa_v.reshape(nb_lat, depth, past, GQA_KV_HEADS * HEAD_DIM))

    cond = jnp.concatenate([c_ctx[None, :], c, jnp.zeros((MOD_ROWS - 1 - nb_lat, d), c.dtype)], axis=0)
    mods = _modulation(cond, ada_w, ada_b).reshape(depth * MOD_ROWS, 1, N_MOD * d)

    xp = x_prompt.reshape(nb_ctx * seq_ctx, d)
    xs = x_sample.reshape(nb_lat * seq_lat, d)
    ctx_row = (0, nb_ctx * seq_ctx)
    lat_row = (1, seq_lat)
    biases = [_na_bias(na_rpb[l], seq_lat) for l in range(depth)]

    states = []
    for l in range(depth):
        last = l == depth - 1
        xp = _ffn(xp, mods, norm_g3, l, 0, 0, ctx_row, wg, wu, wd)
        proj = _inproj(xp, mods, norm_g3, l, ctx_row, w_in_p)
        q_mla, k_mla, v_mla, ckv_n, kr, gq, gk = _heads(proj, l, wts, None)
        o_a, o_b, o_c = _context_attention(proj, q_mla, k_mla, v_mla, gq, gk, seq_ctx)
        xp = _outproj(xp, mods, l, ctx_row, o_a, o_b, o_c, w_out_b)
        xp = _ffn(xp, mods, norm_g3, l, 2, 1, ctx_row, wg, wu, wd, final_g if last else None)
        states.append((
            ckv_n.reshape(nb_ctx, seq_ctx, KV_LORA),
            kr[:, :MLA_ROPE].reshape(nb_ctx, seq_ctx, MLA_ROPE),
            proj[:, COL_NAK:COL_NAK + 512].reshape(nb_ctx, seq_ctx, NA_HEADS, HEAD_DIM),
            proj[:, COL_NAV:COL_NAV + 512].reshape(nb_ctx, seq_ctx, NA_HEADS, HEAD_DIM),
            gk.reshape(nb_ctx, seq_ctx, GQA_KV_HEADS, HEAD_DIM),
            proj[:, COL_GV:COL_GV + 256].reshape(nb_ctx, seq_ctx, GQA_KV_HEADS, HEAD_DIM),
        ))
        xs = _ffn(xs, mods, norm_g3, l, 0, 0, lat_row, wg, wu, wd)
        proj = _inproj(xs, mods, norm_g3, l, lat_row, w_in_p)
        q_mla, k_mla, v_mla, _, _, gq, gk = _heads(proj, l, wts, tables)
        kc_mla, vc_mla = _cache_kv(cache_mla_ckv, kr_cache, l, wts)
        o_a, o_b, o_c = _latent_attention(proj, q_mla, k_mla, v_mla, gq, gk, kc_mla, vc_mla, caches, biases[l], l,
                                          seq_lat)
        xs = _outproj(xs, mods, l, lat_row, o_a, o_b, o_c, w_out_b)
        xs = _ffn(xs, mods, norm_g3, l, 2, 1, lat_row, wg, wu, wd, final_g if last else None)

    y_prompt = xp.reshape(nb_ctx, seq_ctx, d)
    y_sample = xs.reshape(nb_lat, seq_lat, d)
    outs = tuple(jnp.stack([s[k] for s in states], axis=1) for k in range(6))
    return (y_prompt, y_sample) + outs
```

```python
import functools

import jax
import jax.numpy as jnp
import numpy as np
from jax import lax
from jax.experimental import pallas as pl
from jax.experimental.pallas import tpu as pltpu

F32 = jnp.float32
BF16 = jnp.bfloat16

D_MODEL = 2048
D_FF = 5632
N_MOD = 9
GRID_W = 64
ROPE_THETA = 10000.0
HEAD_DIM = 128
MLA_HEADS = 8
MLA_NOPE = 128
MLA_ROPE = 64
MLA_QK = MLA_NOPE + MLA_ROPE
MLA_PAD = 256
Q_LORA = 512
KV_LORA = 256
NA_HEADS = 4
NA_WIN_H = 8
NA_WIN_W = 16
GQA_HEADS = 4
GQA_KV_HEADS = 2
EPS = 1e-6
NEG_INF = -1e30
MOD_ROWS = 16

LANES = 128
COL_CQ = 0
COL_CKV = COL_CQ + Q_LORA
COL_KR = COL_CKV + KV_LORA
COL_NA = COL_KR + LANES
COL_GQ = COL_NA + 3 * NA_HEADS * HEAD_DIM
COL_GK = COL_GQ + GQA_HEADS * HEAD_DIM
COL_GV = COL_GK + GQA_KV_HEADS * HEAD_DIM
PROJ_W = COL_GV + GQA_KV_HEADS * HEAD_DIM

VMEM_LIMIT = 56 * 1024 * 1024


def _params(*sem):
    return pltpu.CompilerParams(dimension_semantics=sem, vmem_limit_bytes=VMEM_LIMIT)


def _resident(shape, index_map):
    return pl.BlockSpec(shape, index_map, pipeline_mode=pl.Buffered(1))


def _rms(x, g):
    return x * lax.rsqrt(jnp.mean(x * x, axis=-1, keepdims=True) + EPS) * g


NORM_ROWS = 256


def _norm_modulate_store(x_ref, ng_ref, sc_ref, sh_ref, h_scr):
    gain = ng_ref[0] * (1.0 + sc_ref[0])
    shift = sh_ref[0]

    def body(i, carry):
        rows = pl.ds(pl.multiple_of(i * NORM_ROWS, NORM_ROWS), NORM_ROWS)
        x = x_ref[rows, :]
        r = lax.rsqrt(jnp.mean(x * x, axis=-1, keepdims=True) + EPS)
        h_scr[rows, :] = (x * r * gain + shift).astype(BF16)
        return carry

    lax.fori_loop(0, x_ref.shape[0] // NORM_ROWS, body, 0)


def _rope(x, cos, sin, quarter):
    n = x.shape[-1]
    lane = lax.broadcasted_iota(jnp.int32, x.shape, 1)
    first = (lane % (2 * quarter)) < quarter
    sw = jnp.where(first, pltpu.roll(x, n - quarter, 1), pltpu.roll(x, quarter, 1))
    return x * cos + sw * sin


def _mod_kernel(c_ref, w_ref, b_ref, o_ref):
    c = c_ref[...]
    s = (c * jax.nn.sigmoid(c)).astype(BF16)
    o_ref[0] = jnp.dot(s, w_ref[0].astype(BF16), preferred_element_type=F32) + b_ref[0]


def _modulation(cond, ada_w, ada_b):
    depth, d, n = ada_w.shape
    tn = 512
    return pl.pallas_call(
        _mod_kernel,
        grid=(depth, n // tn),
        in_specs=[
            pl.BlockSpec((MOD_ROWS, d), lambda l, j: (0, 0)),
            pl.BlockSpec((1, d, tn), lambda l, j: (l, 0, j)),
            pl.BlockSpec((1, 1, tn), lambda l, j: (l, 0, j)),
        ],
        out_specs=pl.BlockSpec((1, MOD_ROWS, tn), lambda l, j: (l, 0, j)),
        out_shape=jax.ShapeDtypeStruct((depth, MOD_ROWS, n), F32),
        compiler_params=_params("parallel", "parallel"),
        name="modulation",
    )(cond, ada_w, ada_b.reshape(depth, 1, n))


def _mod_spec(l, chunk, stream, tm):
    base, tokens = stream
    return pl.BlockSpec((1, 1, D_MODEL), lambda i, *_: (l * MOD_ROWS + base + (i * tm) // tokens, 0, chunk))


def _ffn_kernel(x_ref, ng_ref, sh_ref, sc_ref, gt_ref, wg_ref, wu_ref, wd_ref, *rest, final):
    if final:
        fg_ref, o_ref, h_scr = rest
    else:
        o_ref, h_scr = rest
    f = pl.program_id(1)

    @pl.when(f == 0)
    def _():
        _norm_modulate_store(x_ref, ng_ref, sc_ref, sh_ref, h_scr)
        o_ref[...] = jnp.zeros_like(o_ref)

    h = h_scr[...]
    g = jnp.dot(h, wg_ref[...], preferred_element_type=F32)
    u = jnp.dot(h, wu_ref[...], preferred_element_type=F32)
    a = (g * jax.nn.sigmoid(g) * u).astype(BF16)
    o_ref[...] += jnp.dot(a, wd_ref[...], preferred_element_type=F32)

    @pl.when(f == pl.num_programs(1) - 1)
    def _():
        y = x_ref[...] + 0.5 * gt_ref[0] * o_ref[...]
        if final:
            y = _rms(y, fg_ref[0])
        o_ref[...] = y


def _ffn(x, mods, norm_g, l, sub, k, stream, wg, wu, wd, final_g=None):
    t, d = x.shape
    tm, tf = 1024, 512
    chunk = 3 * sub
    in_specs = [
        pl.BlockSpec((tm, d), lambda i, j: (i, 0), pipeline_mode=pl.Buffered(1)),
        pl.BlockSpec((1, 1, d), lambda i, j: (l * 3 + sub, 0, 0)),
        _mod_spec(l, chunk, stream, tm),
        _mod_spec(l, chunk + 1, stream, tm),
        _mod_spec(l, chunk + 2, stream, tm),
        pl.BlockSpec((None, None, d, tf), lambda i, j: (l, k, 0, j)),
        pl.BlockSpec((None, None, d, tf), lambda i, j: (l, k, 0, j)),
        pl.BlockSpec((None, None, tf, d), lambda i, j: (l, k, j, 0)),
    ]
    args = [x, norm_g, mods, mods, mods, wg, wu, wd]
    if final_g is not None:
        in_specs.append(pl.BlockSpec((1, 1, d), lambda i, j: (0, 0, 0)))
        args.append(final_g)
    return pl.pallas_call(
        functools.partial(_ffn_kernel, final=final_g is not None),
        grid=(t // tm, D_FF // tf),
        in_specs=in_specs,
        out_specs=pl.BlockSpec((tm, d), lambda i, j: (i, 0)),
        out_shape=jax.ShapeDtypeStruct((t, d), F32),
        scratch_shapes=[pltpu.VMEM((tm, d), BF16)],
        compiler_params=_params("parallel", "arbitrary"),
        name="ffn",
    )(*args)


def _expand_kv(ckv_bf, kr, wk_ref, wv_ref, k_out, v_out):
    kn = jnp.dot(ckv_bf, wk_ref[...], preferred_element_type=F32)
    v_out[...] = jnp.dot(ckv_bf, wv_ref[...], preferred_element_type=F32).astype(BF16)
    kr_bf = kr.astype(BF16)
    for h in range(MLA_HEADS):
        k_out[:, h * MLA_PAD:h * MLA_PAD + MLA_NOPE] = kn[:, h * MLA_NOPE:(h + 1) * MLA_NOPE].astype(BF16)
        k_out[:, h * MLA_PAD + MLA_NOPE:(h + 1) * MLA_PAD] = kr_bf


N_QKV_OUT = 7
N_CACHE_OUT = 6


def _qkv_kernel(x_ref, ng_ref, sh_ref, sc_ref, w_ref, qn_ref, wqb_ref, kvn_ref, wk_ref, wv_ref, gqn_ref, gkn_ref,
                *rest, rope, n_prev, cache_out):
    if rope:
        c64_ref, s64_ref, c128_ref, s128_ref = rest[:4]
        rest = rest[4:]
        rot64 = lambda v: _rope(v, c64_ref[...], s64_ref[...], MLA_ROPE // 4)
        rot128 = lambda v: _rope(v, c128_ref[...], s128_ref[...], HEAD_DIM // 4)
    else:
        rot64 = rot128 = lambda v: v
    rest = rest[n_prev:]
    q_out, k_out, v_out, gq_out, gk_out, gv_out, na_out = rest[:N_QKV_OUT]
    rest = rest[N_QKV_OUT:]
    if cache_out:
        ckv_c, kr_c, nak_c, nav_c, gk_c, gv_c = rest[:N_CACHE_OUT]
        rest = rest[N_CACHE_OUT:]
    (h_scr,) = rest
    tm = x_ref.shape[0]

    _norm_modulate_store(x_ref, ng_ref, sc_ref, sh_ref, h_scr)
    h = h_scr[...]
    low = jnp.dot(h, w_ref[:, COL_CQ:COL_NA], preferred_element_type=F32)
    na = jnp.dot(h, w_ref[:, COL_NA:COL_GQ], preferred_element_type=F32)
    gg = jnp.dot(h, w_ref[:, COL_GQ:PROJ_W], preferred_element_type=F32)

    qn = _rms(low[:, COL_CQ:COL_CKV], qn_ref[0]).astype(BF16)
    q = jnp.dot(qn, wqb_ref[...], preferred_element_type=F32)
    for hh in range(MLA_HEADS):
        lo = hh * MLA_PAD
        q_out[:, lo:lo + MLA_NOPE] = q[:, lo:lo + MLA_NOPE].astype(BF16)
        q_out[:, lo + MLA_NOPE:lo + MLA_PAD] = rot64(q[:, lo + MLA_NOPE:lo + MLA_PAD]).astype(BF16)

    ckv = _rms(low[:, COL_CKV:COL_KR], kvn_ref[0])
    kr = rot64(low[:, COL_KR:COL_NA])
    _expand_kv(ckv.astype(BF16), kr, wk_ref, wv_ref, k_out, v_out)

    na_out[...] = na.astype(BF16)
    nw = NA_HEADS * HEAD_DIM
    gw = GQA_HEADS * HEAD_DIM
    kw = GQA_KV_HEADS * HEAD_DIM
    for hh in range(GQA_HEADS):
        sl = slice(hh * HEAD_DIM, (hh + 1) * HEAD_DIM)
        gq_out[:, sl] = rot128(_rms(gg[:, sl], gqn_ref[0])).astype(BF16)
    gv = gg[:, gw + kw:]
    gv_out[...] = gv.astype(BF16)
    for hh in range(GQA_KV_HEADS):
        sl = slice(hh * HEAD_DIM, (hh + 1) * HEAD_DIM)
        gk = rot128(_rms(gg[:, gw + hh * HEAD_DIM:gw + (hh + 1) * HEAD_DIM], gkn_ref[0]))
        gk_out[:, sl] = gk.astype(BF16)
        if cache_out:
            gk_c[pl.ds(hh, tm, stride=GQA_KV_HEADS), :] = gk
            gv_c[pl.ds(hh, tm, stride=GQA_KV_HEADS), :] = gv[:, sl]
    if cache_out:
        ckv_c[...] = ckv
        kr_c[...] = kr[:, :MLA_ROPE]
        for hh in range(NA_HEADS):
            sl = slice(hh * HEAD_DIM, (hh + 1) * HEAD_DIM)
            nak_c[pl.ds(hh, tm, stride=NA_HEADS), :] = na[:, nw + hh * HEAD_DIM:nw + (hh + 1) * HEAD_DIM]
            nav_c[pl.ds(hh, tm, stride=NA_HEADS), :] = na[:, 2 * nw + hh * HEAD_DIM:2 * nw + (hh + 1) * HEAD_DIM]


def _cache_shapes(nb, depth, seq):
    return [
        (nb, depth, seq, KV_LORA), (nb, depth, seq, MLA_ROPE),
        (nb, depth, seq * NA_HEADS, HEAD_DIM), (nb, depth, seq * NA_HEADS, HEAD_DIM),
        (nb, depth, seq * GQA_KV_HEADS, HEAD_DIM), (nb, depth, seq * GQA_KV_HEADS, HEAD_DIM),
    ]


def _qkv(x, mods, norm_g, l, stream, w_in, wts, tm, tables=None, cache=None):
    t, d = x.shape
    rope = tables is not None
    qw = MLA_HEADS * MLA_PAD
    vw = MLA_HEADS * HEAD_DIM
    in_specs = [
        pl.BlockSpec((tm, d), lambda i: (i, 0)),
        pl.BlockSpec((1, 1, d), lambda i: (l * 3 + 1, 0, 0)),
        _mod_spec(l, 3, stream, tm),
        _mod_spec(l, 4, stream, tm),
        _resident((None, d, PROJ_W), lambda i: (l, 0, 0)),
        pl.BlockSpec((1, 1, Q_LORA), lambda i: (l, 0, 0)),
        _resident((None, Q_LORA, qw), lambda i: (l, 0, 0)),
        pl.BlockSpec((1, 1, KV_LORA), lambda i: (l, 0, 0)),
        _resident((None, KV_LORA, vw), lambda i: (l, 0, 0)),
        _resident((None, KV_LORA, vw), lambda i: (l, 0, 0)),
        pl.BlockSpec((1, 1, HEAD_DIM), lambda i: (l, 0, 0)),
        pl.BlockSpec((1, 1, HEAD_DIM), lambda i: (l, 0, 0)),
    ]
    args = [x, norm_g, mods, mods, w_in, wts["q_norm"], wts["wqb"], wts["kv_norm"], wts["wk"], wts["wv"],
            wts["gq_norm"], wts["gk_norm"]]
    if rope:
        per = tables[0].shape[0] // tm
        in_specs += [pl.BlockSpec((tm, LANES), lambda i: (i % per, 0))] * 4
        args += list(tables)
    row = lambda w: pl.BlockSpec((tm, w), lambda i: (i, 0))
    widths = [qw, qw, vw, GQA_HEADS * HEAD_DIM, GQA_KV_HEADS * HEAD_DIM, GQA_KV_HEADS * HEAD_DIM,
              3 * NA_HEADS * HEAD_DIM]
    out_specs = [row(w) for w in widths]
    out_shape = [jax.ShapeDtypeStruct((t, w), BF16) for w in widths]
    aliases = {}
    n_prev = 0
    if cache is not None:
        nb, depth, prev = cache
        assert tm * nb == t
        shapes = _cache_shapes(nb, depth, tm)
        if prev is not None:
            n_prev = len(prev)
            aliases = {len(args) + k: N_QKV_OUT + k for k in range(n_prev)}
            in_specs += [pl.BlockSpec(memory_space=pl.ANY)] * n_prev
            args += list(prev)
        out_specs += [pl.BlockSpec((None, None) + s[2:], lambda i: (i, l, 0, 0)) for s in shapes]
        out_shape += [jax.ShapeDtypeStruct(s, F32) for s in shapes]
    return pl.pallas_call(
        functools.partial(_qkv_kernel, rope=rope, n_prev=n_prev, cache_out=cache is not None),
        grid=(t // tm,),
        in_specs=in_specs,
        out_specs=out_specs,
        out_shape=out_shape,
        input_output_aliases=aliases,
        scratch_shapes=[pltpu.VMEM((tm, d), BF16)],
        compiler_params=_params("parallel"),
        name="qkv",
    )(*args)


def _cache_kv_kernel(ckv_ref, kr_ref, wk_ref, wv_ref, k_out, v_out):
    _expand_kv(ckv_ref[...].astype(BF16), kr_ref[...], wk_ref, wv_ref, k_out, v_out)


def _cache_kv(cache_ckv, cache_kr_pad, l, wts):
    nb, _, past, _ = cache_ckv.shape
    qw = MLA_HEADS * MLA_PAD
    vw = MLA_HEADS * HEAD_DIM
    return pl.pallas_call(
        _cache_kv_kernel,
        grid=(nb,),
        in_specs=[
            pl.BlockSpec((None, None, past, KV_LORA), lambda b: (b, l, 0, 0)),
            pl.BlockSpec((None, None, past, LANES), lambda b: (b, l, 0, 0)),
            pl.BlockSpec((None, KV_LORA, vw), lambda b: (l, 0, 0)),
            pl.BlockSpec((None, KV_LORA, vw), lambda b: (l, 0, 0)),
        ],
        out_specs=[pl.BlockSpec((past, qw), lambda b: (b, 0)), pl.BlockSpec((past, vw), lambda b: (b, 0))],
        out_shape=[jax.ShapeDtypeStruct((nb * past, qw), BF16), jax.ShapeDtypeStruct((nb * past, vw), BF16)],
        compiler_params=_params("parallel"),
        name="cache_kv",
    )(cache_ckv, cache_kr_pad, wts["wk"], wts["wv"])


def _attn_kernel(*refs, n_heads, group, dk, dv, scale, seg_kinds, n_sub):
    q_ref = refs[0]
    pos = 1
    segs = []
    for has_bias, inter in seg_kinds:
        b_ref = refs[pos + 2] if has_bias else None
        segs.append((refs[pos], refs[pos + 1], b_ref, inter))
        pos += 3 if has_bias else 2
    o_ref = refs[pos]

    def head_rows(ref, rows, hk, width, inter):
        if inter:
            return ref[pl.ds(hk, ref.shape[0] // inter, stride=inter), :]
        return ref[rows, hk * width:(hk + 1) * width]

    def attend(q_rows, seg_rows):
        for h in range(n_heads):
            hk = h // group
            q = q_ref[q_rows, h * dk:(h + 1) * dk].astype(BF16)
            scores = []
            for (k_ref, _, b_ref, inter), rows in zip(segs, seg_rows):
                k = head_rows(k_ref, rows, hk, dk, inter).astype(BF16)
                s = lax.dot_general(q, k, (((1,), (1,)), ((), ())), preferred_element_type=F32) * scale
                if b_ref is not None:
                    s = s + b_ref[h]
                scores.append(s)
            m = functools.reduce(jnp.maximum, [s.max(axis=-1, keepdims=True) for s in scores])
            probs = [jnp.exp(s - m) for s in scores]
            denom = functools.reduce(jnp.add, [p.sum(axis=-1, keepdims=True) for p in probs])
            o = None
            for p, (_, v_ref, _, inter), rows in zip(probs, segs, seg_rows):
                v = head_rows(v_ref, rows, hk, dv, inter).astype(BF16)
                pv = jnp.dot(p.astype(BF16), v, preferred_element_type=F32)
                o = pv if o is None else o + pv
            o_ref[q_rows, h * dv:(h + 1) * dv] = (o / denom).astype(o_ref.dtype)

    if n_sub == 1:
        attend(slice(None), [slice(None)] * len(segs))
    else:
        def body(s, carry):
            rows = lambda ref: pl.ds(pl.multiple_of(s * (ref.shape[0] // n_sub), ref.shape[0] // n_sub),
                                     ref.shape[0] // n_sub)
            attend(rows(q_ref), [rows(seg[0]) for seg in segs])
            return carry

        lax.fori_loop(0, n_sub, body, 0)


def _attention(q, segs, *, t, grid, tq, q_index, n_heads, group, dk, dv, scale, name, n_sub=1):
    assert n_sub == 1 or all(s[4] is None and not s[6] for s in segs)
    in_specs = [pl.BlockSpec((tq, n_heads * dk), q_index)]
    args = [q]
    seg_kinds = []
    for k_arr, k_spec, v_arr, v_spec, b_arr, b_spec, inter in segs:
        in_specs += [k_spec, v_spec]
        args += [k_arr, v_arr]
        seg_kinds.append((b_arr is not None, inter))
        if b_arr is not None:
            in_specs.append(b_spec)
            args.append(b_arr)
    out_index = lambda *g: (q_index(*g)[0], 0)
    return pl.pallas_call(
        functools.partial(_attn_kernel, n_heads=n_heads, group=group, dk=dk, dv=dv, scale=scale,
                          seg_kinds=tuple(seg_kinds), n_sub=n_sub),
        grid=grid,
        in_specs=in_specs,
        out_specs=pl.BlockSpec((tq, n_heads * dv), out_index),
        out_shape=jax.ShapeDtypeStruct((t, n_heads * dv), BF16),
        compiler_params=_params(*(("parallel",) * len(grid))),
        name=name,
    )(*args)


def _context_attention(q_mla, k_mla, v_mla, gq, gk, gv, na, seq):
    t = na.shape[0]
    n_sub = 4
    rows = seq * n_sub
    grid = (t // rows,)
    blk = lambda w, c: pl.BlockSpec((rows, w), lambda b: (b, c))
    common = dict(t=t, grid=grid, tq=rows, n_sub=n_sub)
    nw = NA_HEADS * HEAD_DIM
    kw = GQA_KV_HEADS * HEAD_DIM
    o_a = _attention(q_mla, [(k_mla, blk(MLA_HEADS * MLA_PAD, 0), v_mla, blk(MLA_HEADS * HEAD_DIM, 0), None, None, 0)],
                     q_index=lambda b: (b, 0), n_heads=MLA_HEADS, group=1, dk=MLA_PAD, dv=HEAD_DIM,
                     scale=MLA_QK ** -0.5, name="ctx_mla", **common)
    o_b = _attention(na, [(na, blk(nw, 1), na, blk(nw, 2), None, None, 0)],
                     q_index=lambda b: (b, 0), n_heads=NA_HEADS, group=1, dk=HEAD_DIM, dv=HEAD_DIM,
                     scale=HEAD_DIM ** -0.5, name="ctx_na", **common)
    o_c = _attention(gq, [(gk, blk(kw, 0), gv, blk(kw, 0), None, None, 0)],
                     q_index=lambda b: (b, 0), n_heads=GQA_HEADS, group=GQA_HEADS // GQA_KV_HEADS,
                     dk=HEAD_DIM, dv=HEAD_DIM, scale=HEAD_DIM ** -0.5, name="ctx_gqa", **common)
    return o_a, o_b, o_c


def _latent_attention(q_mla, k_mla, v_mla, gq, gk, gv, na, kc_mla, vc_mla, caches, bias, l, seq):
    cache_na_k, cache_na_v, cache_gqa_k, cache_gqa_v = caches
    t = na.shape[0]
    past = kc_mla.shape[0] // (t // seq)
    own = lambda w, c: pl.BlockSpec((seq, w), lambda b, i: (b, c))
    flat = lambda w: pl.BlockSpec((past, w), lambda b, i: (b, 0))
    cached = lambda heads: pl.BlockSpec((None, None, past * heads, HEAD_DIM), lambda b, i: (b, l, 0, 0))
    qw, vw = MLA_HEADS * MLA_PAD, MLA_HEADS * HEAD_DIM
    nw = NA_HEADS * HEAD_DIM
    kw = GQA_KV_HEADS * HEAD_DIM
    o_c = _attention(gq, [(cache_gqa_k, cached(GQA_KV_HEADS), cache_gqa_v, cached(GQA_KV_HEADS), None, None,
                           GQA_KV_HEADS),
                          (gk, own(kw, 0), gv, own(kw, 0), None, None, 0)],
                     q_index=lambda b, i: (b, 0), n_heads=GQA_HEADS, group=GQA_HEADS // GQA_KV_HEADS,
                     dk=HEAD_DIM, dv=HEAD_DIM, scale=HEAD_DIM ** -0.5, name="lat_gqa",
                     t=t, grid=(t // seq, 1), tq=seq)
    tq = 512
    nq = seq // tq
    common = dict(t=t, grid=(t // seq, nq), tq=tq)
    o_a = _attention(q_mla, [(kc_mla, flat(qw), vc_mla, flat(vw), None, None, 0),
                             (k_mla, own(qw, 0), v_mla, own(vw, 0), None, None, 0)],
                     q_index=lambda b, i: (b * nq + i, 0), n_heads=MLA_HEADS, group=1, dk=MLA_PAD, dv=HEAD_DIM,
                     scale=MLA_QK ** -0.5, name="lat_mla", **common)
    bias_spec = pl.BlockSpec((NA_HEADS, tq, seq), lambda b, i: (0, i, 0))
    o_b = _attention(na, [(cache_na_k, cached(NA_HEADS), cache_na_v, cached(NA_HEADS), None, None, NA_HEADS),
                          (na, own(nw, 1), na, own(nw, 2), bias, bias_spec, 0)],
                     q_index=lambda b, i: (b * nq + i, 0), n_heads=NA_HEADS, group=1,
                     dk=HEAD_DIM, dv=HEAD_DIM, scale=HEAD_DIM ** -0.5, name="lat_na", **common)
    return o_a, o_b, o_c


def _outproj_kernel(x_ref, gt_ref, oa_ref, ob_ref, oc_ref, w_ref, o_ref):
    wa = oa_ref.shape[1]
    wb = ob_ref.shape[1]
    acc = jnp.dot(oa_ref[...], w_ref[0:wa, :], preferred_element_type=F32)
    acc += jnp.dot(ob_ref[...], w_ref[wa:wa + wb, :], preferred_element_type=F32)
    acc += jnp.dot(oc_ref[...], w_ref[wa + wb:, :], preferred_element_type=F32)
    o_ref[...] = x_ref[...] + gt_ref[0] * acc


def _outproj(x, mods, l, stream, o_a, o_b, o_c, w_out):
    t, d = x.shape
    tm = 512
    row = lambda a: pl.BlockSpec((tm, a.shape[1]), lambda i: (i, 0))
    return pl.pallas_call(
        _outproj_kernel,
        grid=(t // tm,),
        in_specs=[
            row(x),
            _mod_spec(l, 5, stream, tm),
            row(o_a), row(o_b), row(o_c),
            _resident((None, d, d), lambda i: (l, 0, 0)),
        ],
        out_specs=row(x),
        out_shape=jax.ShapeDtypeStruct((t, d), F32),
        compiler_params=_params("parallel"),
        name="outproj",
    )(x, mods, o_a, o_b, o_c, w_out)


def _rope_tables(seq, d):
    quarter = d // 4
    tt = np.arange(seq)
    pos = np.stack([tt // GRID_W, tt % GRID_W], axis=-1).astype(np.float64)
    inv = ROPE_THETA ** (-np.arange(quarter, dtype=np.float64) / quarter)
    ang = pos[:, :, None] * inv
    cos = np.cos(ang)
    sin = np.sin(ang)
    cos_t = np.stack([cos, cos], axis=2).reshape(seq, d)
    sin_t = np.stack([-sin, sin], axis=2).reshape(seq, d)
    pad = LANES - d
    if pad:
        cos_t = np.concatenate([cos_t, np.ones((seq, pad))], axis=-1)
        sin_t = np.concatenate([sin_t, np.zeros((seq, pad))], axis=-1)
    return jnp.asarray(cos_t, F32), jnp.asarray(sin_t, F32)


def _na_bias_kernel(t_ref, o_ref, *, rows, wh):
    r = pl.program_id(1)
    shape = (GRID_W, 2 * GRID_W)
    c = lax.broadcasted_iota(jnp.int32, shape, 0)
    lane = lax.broadcasted_iota(jnp.int32, shape, 1)
    kc = lane % GRID_W
    rs = jnp.clip(r - wh // 2, 0, rows - wh)
    cs = jnp.clip(c - NA_WIN_W // 2, 0, GRID_W - NA_WIN_W)
    col_ok = (kc >= cs) & (kc < cs + NA_WIN_W)
    for j in range(rows // 2):
        kr = 2 * j + lane // GRID_W
        ok = col_ok & (kr >= rs) & (kr < rs + wh)
        d = jnp.clip(2 * j - r + NA_WIN_H, 0, 2 * NA_WIN_H - 1)
        pair = jnp.broadcast_to(t_ref[0, pl.ds(d, 1), :], shape)
        toeplitz = pltpu.roll(pair, 2 * GRID_W - (NA_WIN_W - 1), 1, stride=1, stride_axis=0)
        o_ref[0, :, j * 2 * GRID_W:(j + 1) * 2 * GRID_W] = jnp.where(ok, toeplitz, NEG_INF)


def _na_bias(rpb, seq):
    nh, nr, nc = rpb.shape
    rows = seq // GRID_W
    wh = min(NA_WIN_H, rows)
    padded = jnp.pad(rpb.astype(F32), ((0, 0), (1, 1), (0, GRID_W - nc)))
    table = jnp.concatenate([padded[:, :nr + 1], padded[:, 1:]], axis=-1)
    return pl.pallas_call(
        functools.partial(_na_bias_kernel, rows=rows, wh=wh),
        grid=(nh, rows),
        in_specs=[pl.BlockSpec((1, nr + 1, 2 * GRID_W), lambda h, r: (h, 0, 0))],
        out_specs=pl.BlockSpec((1, GRID_W, seq), lambda h, r: (h, r, 0)),
        out_shape=jax.ShapeDtypeStruct((nh, seq, seq), F32),
        compiler_params=_params("parallel", "parallel"),
        name="na_bias",
    )(table)


def kernel(x_prompt, x_sample, cache_mla_ckv, cache_mla_krope, cache_na_k, cache_na_v, cache_gqa_k, cache_gqa_v, c, c_ctx, ada_w, ada_b, norm_g, ffn_wg, ffn_wu, ffn_wd, w_in, mla_q_norm, mla_wqb, mla_kv_norm, mla_wkvb, na_rpb, gqa_q_norm, gqa_k_norm, w_out, final_norm):
    depth = ada_w.shape[0]
    nb_ctx, seq_ctx, d = x_prompt.shape
    nb_lat, seq_lat, _ = x_sample.shape
    past = cache_mla_ckv.shape[2]

    wg = ffn_wg.astype(BF16)
    wu = ffn_wu.astype(BF16)
    wd = ffn_wd.astype(BF16)
    split = Q_LORA + KV_LORA + MLA_ROPE
    w_in_p = jnp.concatenate(
        [w_in[..., :split], jnp.zeros((depth, d, LANES - MLA_ROPE), w_in.dtype), w_in[..., split:]],
        axis=-1).astype(BF16)
    wqb = jnp.pad(mla_wqb.reshape(depth, Q_LORA, MLA_HEADS, MLA_QK),
                  ((0, 0), (0, 0), (0, 0), (0, MLA_PAD - MLA_QK))).reshape(depth, Q_LORA, MLA_HEADS * MLA_PAD)
    wkv = mla_wkvb.reshape(depth, KV_LORA, MLA_HEADS, 2 * HEAD_DIM)
    wts = {
        "q_norm": mla_q_norm.reshape(depth, 1, Q_LORA),
        "wqb": wqb.astype(BF16),
        "kv_norm": mla_kv_norm.reshape(depth, 1, KV_LORA),
        "wk": wkv[..., :MLA_NOPE].reshape(depth, KV_LORA, MLA_HEADS * MLA_NOPE).astype(BF16),
        "wv": wkv[..., MLA_NOPE:].reshape(depth, KV_LORA, MLA_HEADS * HEAD_DIM).astype(BF16),
        "gq_norm": gqa_q_norm.reshape(depth, 1, HEAD_DIM),
        "gk_norm": gqa_k_norm.reshape(depth, 1, HEAD_DIM),
    }
    w_out_b = w_out.astype(BF16)
    norm_g3 = norm_g.reshape(depth * 3, 1, d)
    final_g = final_norm.reshape(1, 1, d)
    tables = _rope_tables(seq_lat, MLA_ROPE) + _rope_tables(seq_lat, HEAD_DIM)
    kr_cache = jnp.pad(cache_mla_krope, ((0, 0), (0, 0), (0, 0), (0, LANES - MLA_ROPE)))
    caches = (cache_na_k.reshape(nb_lat, depth, past * NA_HEADS, HEAD_DIM),
              cache_na_v.reshape(nb_lat, depth, past * NA_HEADS, HEAD_DIM),
              cache_gqa_k.reshape(nb_lat, depth, past * GQA_KV_HEADS, HEAD_DIM),
              cache_gqa_v.reshape(nb_lat, depth, past * GQA_KV_HEADS, HEAD_DIM))

    cond = jnp.concatenate([c_ctx[None, :], c, jnp.zeros((MOD_ROWS - 1 - nb_lat, d), c.dtype)], axis=0)
    mods = _modulation(cond, ada_w, ada_b).reshape(depth * MOD_ROWS, 1, N_MOD * d)

    xp = x_prompt.reshape(nb_ctx * seq_ctx, d)
    xs = x_sample.reshape(nb_lat * seq_lat, d)
    ctx_row = (0, nb_ctx * seq_ctx)
    lat_row = (1, seq_lat)
    biases = [_na_bias(na_rpb[l], seq_lat) for l in range(depth)]

    new_cache = None
    for l in range(depth):
        last = l == depth - 1
        xp = _ffn(xp, mods, norm_g3, l, 0, 0, ctx_row, wg, wu, wd)
        outs = _qkv(xp, mods, norm_g3, l, ctx_row, w_in_p, wts, seq_ctx, cache=(nb_ctx, depth, new_cache))
        q_mla, k_mla, v_mla, gq, gk, gv, na = outs[:N_QKV_OUT]
        new_cache = outs[N_QKV_OUT:]
        o_a, o_b, o_c = _context_attention(q_mla, k_mla, v_mla, gq, gk, gv, na, seq_ctx)
        xp = _outproj(xp, mods, l, ctx_row, o_a, o_b, o_c, w_out_b)
        xp = _ffn(xp, mods, norm_g3, l, 2, 1, ctx_row, wg, wu, wd, final_g if last else None)
        xs = _ffn(xs, mods, norm_g3, l, 0, 0, lat_row, wg, wu, wd)
        q_mla, k_mla, v_mla, gq, gk, gv, na = _qkv(xs, mods, norm_g3, l, lat_row, w_in_p, wts, 512, tables=tables)
        kc_mla, vc_mla = _cache_kv(cache_mla_ckv, kr_cache, l, wts)
        o_a, o_b, o_c = _latent_attention(q_mla, k_mla, v_mla, gq, gk, gv, na, kc_mla, vc_mla, caches, biases[l], l,
                                          seq_lat)
        xs = _outproj(xs, mods, l, lat_row, o_a, o_b, o_c, w_out_b)
        xs = _ffn(xs, mods, norm_g3, l, 2, 1, lat_row, wg, wu, wd, final_g if last else None)

    y_prompt = xp.reshape(nb_ctx, seq_ctx, d)
    y_sample = xs.reshape(nb_lat, seq_lat, d)
    ckv_c, kr_c, nak_c, nav_c, gk_c, gv_c = new_cache
    na_shape = (nb_ctx, depth, seq_ctx, NA_HEADS, HEAD_DIM)
    gqa_shape = (nb_ctx, depth, seq_ctx, GQA_KV_HEADS, HEAD_DIM)
    return (y_prompt, y_sample, ckv_c, kr_c, nak_c.reshape(na_shape), nav_c.reshape(na_shape),
            gk_c.reshape(gqa_shape), gv_c.reshape(gqa_shape))
```

```python
import functools

import jax
import jax.numpy as jnp
import numpy as np
from jax import lax
from jax.experimental import pallas as pl
from jax.experimental.pallas import tpu as pltpu

F32 = jnp.float32
BF16 = jnp.bfloat16

D_MODEL = 2048
D_FF = 5632
N_MOD = 9
GRID_W = 64
ROPE_THETA = 10000.0
HEAD_DIM = 128
MLA_HEADS = 8
MLA_NOPE = 128
MLA_ROPE = 64
MLA_QK = MLA_NOPE + MLA_ROPE
MLA_PAD = 256
Q_LORA = 512
KV_LORA = 256
NA_HEADS = 4
NA_WIN_H = 8
NA_WIN_W = 16
GQA_HEADS = 4
GQA_KV_HEADS = 2
EPS = 1e-6
NEG_INF = -1e30
MOD_ROWS = 16

LANES = 128
COL_CQ = 0
COL_CKV = COL_CQ + Q_LORA
COL_KR = COL_CKV + KV_LORA
COL_NA = COL_KR + LANES
COL_GQ = COL_NA + 3 * NA_HEADS * HEAD_DIM
COL_GK = COL_GQ + GQA_HEADS * HEAD_DIM
COL_GV = COL_GK + GQA_KV_HEADS * HEAD_DIM
PROJ_W = COL_GV + GQA_KV_HEADS * HEAD_DIM

VMEM_LIMIT = 60 * 1024 * 1024


def _params(*sem):
    return pltpu.CompilerParams(dimension_semantics=sem, vmem_limit_bytes=VMEM_LIMIT)


def _resident(shape, index_map):
    return pl.BlockSpec(shape, index_map, pipeline_mode=pl.Buffered(1))


def _as_bf16(ref):
    v = ref[...]
    return v if v.dtype == BF16 else v.astype(BF16)


def _rms(x, g):
    return x * lax.rsqrt(jnp.mean(x * x, axis=-1, keepdims=True) + EPS) * g


NORM_ROWS = 256


def _norm_modulate_store(x_ref, ng_ref, sc_ref, sh_ref, h_scr):
    gain = ng_ref[0] * (1.0 + sc_ref[0])
    shift = sh_ref[0]

    def body(i, carry):
        rows = pl.ds(pl.multiple_of(i * NORM_ROWS, NORM_ROWS), NORM_ROWS)
        x = x_ref[rows, :]
        r = lax.rsqrt(jnp.mean(x * x, axis=-1, keepdims=True) + EPS)
        h_scr[rows, :] = (x * r * gain + shift).astype(BF16)
        return carry

    lax.fori_loop(0, x_ref.shape[0] // NORM_ROWS, body, 0)


def _rope(x, cos, sin, quarter):
    n = x.shape[-1]
    lane = lax.broadcasted_iota(jnp.int32, x.shape, 1)
    first = (lane % (2 * quarter)) < quarter
    sw = jnp.where(first, pltpu.roll(x, n - quarter, 1), pltpu.roll(x, quarter, 1))
    return x * cos + sw * sin


def _mod_kernel(c_ref, w_ref, b_ref, o_ref):
    c = c_ref[...]
    s = (c * jax.nn.sigmoid(c)).astype(BF16)
    o_ref[0] = jnp.dot(s, w_ref[0].astype(BF16), preferred_element_type=F32) + b_ref[0]


def _modulation(cond, ada_w, ada_b):
    depth, d, n = ada_w.shape
    tn = 512
    return pl.pallas_call(
        _mod_kernel,
        grid=(depth, n // tn),
        in_specs=[
            pl.BlockSpec((MOD_ROWS, d), lambda l, j: (0, 0)),
            pl.BlockSpec((1, d, tn), lambda l, j: (l, 0, j)),
            pl.BlockSpec((1, 1, tn), lambda l, j: (l, 0, j)),
        ],
        out_specs=pl.BlockSpec((1, MOD_ROWS, tn), lambda l, j: (l, 0, j)),
        out_shape=jax.ShapeDtypeStruct((depth, MOD_ROWS, n), F32),
        compiler_params=_params("parallel", "parallel"),
        name="modulation",
    )(cond, ada_w, ada_b.reshape(depth, 1, n))


def _mod_spec(l, chunk, stream, tm):
    base, tokens = stream
    return pl.BlockSpec((1, 1, D_MODEL), lambda i, *_: (l * MOD_ROWS + base + (i * tm) // tokens, 0, chunk))


def _ffn_kernel(x_ref, ng_ref, sh_ref, sc_ref, gt_ref, wg_ref, wu_ref, wd_ref, *rest, final):
    if final:
        fg_ref, o_ref, h_scr = rest
    else:
        o_ref, h_scr = rest
    f = pl.program_id(1)

    @pl.when(f == 0)
    def _():
        _norm_modulate_store(x_ref, ng_ref, sc_ref, sh_ref, h_scr)
        o_ref[...] = jnp.zeros_like(o_ref)

    h = h_scr[...]
    g = jnp.dot(h, _as_bf16(wg_ref), preferred_element_type=F32)
    u = jnp.dot(h, _as_bf16(wu_ref), preferred_element_type=F32)
    a = (g * jax.nn.sigmoid(g) * u).astype(BF16)
    o_ref[...] += jnp.dot(a, _as_bf16(wd_ref), preferred_element_type=F32)

    @pl.when(f == pl.num_programs(1) - 1)
    def _():
        gate = 0.5 * gt_ref[0]

        def body(i, carry):
            rows = pl.ds(pl.multiple_of(i * NORM_ROWS, NORM_ROWS), NORM_ROWS)
            y = x_ref[rows, :] + gate * o_ref[rows, :]
            if final:
                y = _rms(y, fg_ref[0])
            o_ref[rows, :] = y
            return carry

        lax.fori_loop(0, o_ref.shape[0] // NORM_ROWS, body, 0)


def _ffn(x, mods, norm_g, l, sub, k, stream, wg, wu, wd, final_g=None):
    t, d = x.shape
    tm, tf = 1024, 512
    chunk = 3 * sub
    in_specs = [
        pl.BlockSpec((tm, d), lambda i, j: (i, 0), pipeline_mode=pl.Buffered(1)),
        pl.BlockSpec((1, 1, d), lambda i, j: (l * 3 + sub, 0, 0)),
        _mod_spec(l, chunk, stream, tm),
        _mod_spec(l, chunk + 1, stream, tm),
        _mod_spec(l, chunk + 2, stream, tm),
        pl.BlockSpec((None, None, d, tf), lambda i, j: (l, k, 0, j)),
        pl.BlockSpec((None, None, d, tf), lambda i, j: (l, k, 0, j)),
        pl.BlockSpec((None, None, tf, d), lambda i, j: (l, k, j, 0)),
    ]
    args = [x, norm_g, mods, mods, mods, wg, wu, wd]
    if final_g is not None:
        in_specs.append(pl.BlockSpec((1, 1, d), lambda i, j: (0, 0, 0)))
        args.append(final_g)
    return pl.pallas_call(
        functools.partial(_ffn_kernel, final=final_g is not None),
        grid=(t // tm, D_FF // tf),
        in_specs=in_specs,
        out_specs=pl.BlockSpec((tm, d), lambda i, j: (i, 0)),
        out_shape=jax.ShapeDtypeStruct((t, d), F32),
        scratch_shapes=[pltpu.VMEM((tm, d), BF16)],
        compiler_params=_params("parallel", "arbitrary"),
        name="ffn",
    )(*args)


def _expand_kv(ckv_bf, kr, wk_ref, wv_ref, k_out, v_out):
    kn = jnp.dot(ckv_bf, wk_ref[...], preferred_element_type=F32)
    v_out[...] = jnp.dot(ckv_bf, wv_ref[...], preferred_element_type=F32).astype(BF16)
    kr_bf = kr.astype(BF16)
    for h in range(MLA_HEADS):
        k_out[:, h * MLA_PAD:h * MLA_PAD + MLA_NOPE] = kn[:, h * MLA_NOPE:(h + 1) * MLA_NOPE].astype(BF16)
        k_out[:, h * MLA_PAD + MLA_NOPE:(h + 1) * MLA_PAD] = kr_bf


N_QKV_OUT = 7
N_CACHE_OUT = 6


def _qkv_kernel(x_ref, ng_ref, sh_ref, sc_ref, w_ref, qn_ref, wqb_ref, kvn_ref, wk_ref, wv_ref, gqn_ref, gkn_ref,
                *rest, rope, n_prev, cache_out):
    if rope:
        c64_ref, s64_ref, c128_ref, s128_ref = rest[:4]
        rest = rest[4:]
        rot64 = lambda v: _rope(v, c64_ref[...], s64_ref[...], MLA_ROPE // 4)
        rot128 = lambda v: _rope(v, c128_ref[...], s128_ref[...], HEAD_DIM // 4)
    else:
        rot64 = rot128 = lambda v: v
    rest = rest[n_prev:]
    q_out, k_out, v_out, gq_out, gk_out, gv_out, na_out = rest[:N_QKV_OUT]
    rest = rest[N_QKV_OUT:]
    if cache_out:
        ckv_c, kr_c, nak_c, nav_c, gk_c, gv_c = rest[:N_CACHE_OUT]
        rest = rest[N_CACHE_OUT:]
    (h_scr,) = rest
    tm = x_ref.shape[0]

    _norm_modulate_store(x_ref, ng_ref, sc_ref, sh_ref, h_scr)
    h = h_scr[...]
    low = jnp.dot(h, w_ref[:, COL_CQ:COL_NA], preferred_element_type=F32)
    na = jnp.dot(h, w_ref[:, COL_NA:COL_GQ], preferred_element_type=F32)
    gg = jnp.dot(h, w_ref[:, COL_GQ:PROJ_W], preferred_element_type=F32)

    qn = _rms(low[:, COL_CQ:COL_CKV], qn_ref[0]).astype(BF16)
    q = jnp.dot(qn, wqb_ref[...], preferred_element_type=F32) * MLA_QK ** -0.5
    for hh in range(MLA_HEADS):
        lo = hh * MLA_PAD
        q_out[:, lo:lo + MLA_NOPE] = q[:, lo:lo + MLA_NOPE].astype(BF16)
        q_out[:, lo + MLA_NOPE:lo + MLA_PAD] = rot64(q[:, lo + MLA_NOPE:lo + MLA_PAD]).astype(BF16)

    ckv = _rms(low[:, COL_CKV:COL_KR], kvn_ref[0])
    kr = rot64(low[:, COL_KR:COL_NA])
    _expand_kv(ckv.astype(BF16), kr, wk_ref, wv_ref, k_out, v_out)

    nw = NA_HEADS * HEAD_DIM
    gw = GQA_HEADS * HEAD_DIM
    kw = GQA_KV_HEADS * HEAD_DIM
    na_out[:, :nw] = (na[:, :nw] * HEAD_DIM ** -0.5).astype(BF16)
    na_out[:, nw:] = na[:, nw:].astype(BF16)
    gq_gain = gqn_ref[0] * HEAD_DIM ** -0.5
    for hh in range(GQA_HEADS):
        sl = slice(hh * HEAD_DIM, (hh + 1) * HEAD_DIM)
        gq_out[:, sl] = rot128(_rms(gg[:, sl], gq_gain)).astype(BF16)
    gv = gg[:, gw + kw:]
    gv_out[...] = gv.astype(BF16)
    for hh in range(GQA_KV_HEADS):
        sl = slice(hh * HEAD_DIM, (hh + 1) * HEAD_DIM)
        gk = rot128(_rms(gg[:, gw + hh * HEAD_DIM:gw + (hh + 1) * HEAD_DIM], gkn_ref[0]))
        gk_out[:, sl] = gk.astype(BF16)
        if cache_out:
            gk_c[pl.ds(hh, tm, stride=GQA_KV_HEADS), :] = gk
            gv_c[pl.ds(hh, tm, stride=GQA_KV_HEADS), :] = gv[:, sl]
    if cache_out:
        ckv_c[...] = ckv
        kr_c[...] = kr[:, :MLA_ROPE]
        for hh in range(NA_HEADS):
            sl = slice(hh * HEAD_DIM, (hh + 1) * HEAD_DIM)
            nak_c[pl.ds(hh, tm, stride=NA_HEADS), :] = na[:, nw + hh * HEAD_DIM:nw + (hh + 1) * HEAD_DIM]
            nav_c[pl.ds(hh, tm, stride=NA_HEADS), :] = na[:, 2 * nw + hh * HEAD_DIM:2 * nw + (hh + 1) * HEAD_DIM]


def _cache_shapes(nb, depth, seq):
    return [
        (nb, depth, seq, KV_LORA), (nb, depth, seq, MLA_ROPE),
        (nb, depth, seq * NA_HEADS, HEAD_DIM), (nb, depth, seq * NA_HEADS, HEAD_DIM),
        (nb, depth, seq * GQA_KV_HEADS, HEAD_DIM), (nb, depth, seq * GQA_KV_HEADS, HEAD_DIM),
    ]


def _qkv(x, mods, norm_g, l, stream, w_in, wts, tm, tables=None, cache=None):
    t, d = x.shape
    rope = tables is not None
    qw = MLA_HEADS * MLA_PAD
    vw = MLA_HEADS * HEAD_DIM
    in_specs = [
        pl.BlockSpec((tm, d), lambda i: (i, 0)),
        pl.BlockSpec((1, 1, d), lambda i: (l * 3 + 1, 0, 0)),
        _mod_spec(l, 3, stream, tm),
        _mod_spec(l, 4, stream, tm),
        _resident((None, d, PROJ_W), lambda i: (l, 0, 0)),
        pl.BlockSpec((1, 1, Q_LORA), lambda i: (l, 0, 0)),
        _resident((None, Q_LORA, qw), lambda i: (l, 0, 0)),
        pl.BlockSpec((1, 1, KV_LORA), lambda i: (l, 0, 0)),
        _resident((None, KV_LORA, vw), lambda i: (l, 0, 0)),
        _resident((None, KV_LORA, vw), lambda i: (l, 0, 0)),
        pl.BlockSpec((1, 1, HEAD_DIM), lambda i: (l, 0, 0)),
        pl.BlockSpec((1, 1, HEAD_DIM), lambda i: (l, 0, 0)),
    ]
    args = [x, norm_g, mods, mods, w_in, wts["q_norm"], wts["wqb"], wts["kv_norm"], wts["wk"], wts["wv"],
            wts["gq_norm"], wts["gk_norm"]]
    if rope:
        per = tables[0].shape[0] // tm
        in_specs += [pl.BlockSpec((tm, LANES), lambda i: (i % per, 0))] * 4
        args += list(tables)
    row = lambda w: pl.BlockSpec((tm, w), lambda i: (i, 0))
    widths = [qw, qw, vw, GQA_HEADS * HEAD_DIM, GQA_KV_HEADS * HEAD_DIM, GQA_KV_HEADS * HEAD_DIM,
              3 * NA_HEADS * HEAD_DIM]
    out_specs = [row(w) for w in widths]
    out_shape = [jax.ShapeDtypeStruct((t, w), BF16) for w in widths]
    aliases = {}
    n_prev = 0
    if cache is not None:
        nb, depth, prev = cache
        assert tm * nb == t
        shapes = _cache_shapes(nb, depth, tm)
        if prev is not None:
            n_prev = len(prev)
            aliases = {len(args) + k: N_QKV_OUT + k for k in range(n_prev)}
            in_specs += [pl.BlockSpec(memory_space=pl.ANY)] * n_prev
            args += list(prev)
        out_specs += [pl.BlockSpec((None, None) + s[2:], lambda i: (i, l, 0, 0)) for s in shapes]
        out_shape += [jax.ShapeDtypeStruct(s, F32) for s in shapes]
    return pl.pallas_call(
        functools.partial(_qkv_kernel, rope=rope, n_prev=n_prev, cache_out=cache is not None),
        grid=(t // tm,),
        in_specs=in_specs,
        out_specs=out_specs,
        out_shape=out_shape,
        input_output_aliases=aliases,
        scratch_shapes=[pltpu.VMEM((tm, d), BF16)],
        compiler_params=_params("parallel"),
        name="qkv",
    )(*args)


def _cache_kv_kernel(ckv_ref, kr_ref, wk_ref, wv_ref, k_out, v_out):
    _expand_kv(ckv_ref[...].astype(BF16), kr_ref[...], wk_ref, wv_ref, k_out, v_out)


def _cache_kv(cache_ckv, cache_kr_pad, l, wts):
    nb, _, past, _ = cache_ckv.shape
    qw = MLA_HEADS * MLA_PAD
    vw = MLA_HEADS * HEAD_DIM
    return pl.pallas_call(
        _cache_kv_kernel,
        grid=(nb,),
        in_specs=[
            pl.BlockSpec((None, None, past, KV_LORA), lambda b: (b, l, 0, 0)),
            pl.BlockSpec((None, None, past, LANES), lambda b: (b, l, 0, 0)),
            pl.BlockSpec((None, KV_LORA, vw), lambda b: (l, 0, 0)),
            pl.BlockSpec((None, KV_LORA, vw), lambda b: (l, 0, 0)),
        ],
        out_specs=[pl.BlockSpec((past, qw), lambda b: (b, 0)), pl.BlockSpec((past, vw), lambda b: (b, 0))],
        out_shape=[jax.ShapeDtypeStruct((nb * past, qw), BF16), jax.ShapeDtypeStruct((nb * past, vw), BF16)],
        compiler_params=_params("parallel"),
        name="cache_kv",
    )(cache_ckv, cache_kr_pad, wts["wk"], wts["wv"])


def _attn_kernel(*refs, n_heads, group, dk, dv, seg_kinds, n_sub):
    q_ref = refs[0]
    pos = 1
    segs = []
    for has_bias, inter in seg_kinds:
        b_ref = refs[pos + 2] if has_bias else None
        segs.append((refs[pos], refs[pos + 1], b_ref, inter))
        pos += 3 if has_bias else 2
    o_ref = refs[pos]

    def head_rows(ref, rows, hk, width, inter):
        if inter:
            return ref[pl.ds(hk, ref.shape[0] // inter, stride=inter), :]
        return ref[rows, hk * width:(hk + 1) * width]

    def attend(q_rows, seg_rows):
        for h in range(n_heads):
            hk = h // group
            q = q_ref[q_rows, h * dk:(h + 1) * dk].astype(BF16)
            scores = []
            for (k_ref, _, b_ref, inter), rows in zip(segs, seg_rows):
                k = head_rows(k_ref, rows, hk, dk, inter).astype(BF16)
                s = lax.dot_general(q, k, (((1,), (1,)), ((), ())), preferred_element_type=F32)
                if b_ref is not None:
                    s = s + b_ref[h]
                scores.append(s)
            m = functools.reduce(jnp.maximum, [s.max(axis=-1, keepdims=True) for s in scores])
            probs = [jnp.exp(s - m) for s in scores]
            denom = functools.reduce(jnp.add, [p.sum(axis=-1, keepdims=True) for p in probs])
            o = None
            for p, (_, v_ref, _, inter), rows in zip(probs, segs, seg_rows):
                v = head_rows(v_ref, rows, hk, dv, inter).astype(BF16)
                pv = jnp.dot(p.astype(BF16), v, preferred_element_type=F32)
                o = pv if o is None else o + pv
            o_ref[q_rows, h * dv:(h + 1) * dv] = (o / denom).astype(o_ref.dtype)

    if n_sub == 1:
        attend(slice(None), [slice(None)] * len(segs))
    else:
        def body(s, carry):
            rows = lambda ref: pl.ds(pl.multiple_of(s * (ref.shape[0] // n_sub), ref.shape[0] // n_sub),
                                     ref.shape[0] // n_sub)
            attend(rows(q_ref), [rows(seg[0]) for seg in segs])
            return carry

        lax.fori_loop(0, n_sub, body, 0)


def _attention(q, segs, *, t, grid, tq, q_index, n_heads, group, dk, dv, name, n_sub=1):
    assert n_sub == 1 or all(s[4] is None and not s[6] for s in segs)
    in_specs = [pl.BlockSpec((tq, n_heads * dk), q_index)]
    args = [q]
    seg_kinds = []
    for k_arr, k_spec, v_arr, v_spec, b_arr, b_spec, inter in segs:
        in_specs += [k_spec, v_spec]
        args += [k_arr, v_arr]
        seg_kinds.append((b_arr is not None, inter))
        if b_arr is not None:
            in_specs.append(b_spec)
            args.append(b_arr)
    out_index = lambda *g: (q_index(*g)[0], 0)
    return pl.pallas_call(
        functools.partial(_attn_kernel, n_heads=n_heads, group=group, dk=dk, dv=dv,
                          seg_kinds=tuple(seg_kinds), n_sub=n_sub),
        grid=grid,
        in_specs=in_specs,
        out_specs=pl.BlockSpec((tq, n_heads * dv), out_index),
        out_shape=jax.ShapeDtypeStruct((t, n_heads * dv), BF16),
        compiler_params=_params(*(("parallel",) * len(grid))),
        name=name,
    )(*args)


def _context_attention(q_mla, k_mla, v_mla, gq, gk, gv, na, seq):
    t = na.shape[0]
    n_sub = 4
    rows = seq * n_sub
    grid = (t // rows,)
    blk = lambda w, c: pl.BlockSpec((rows, w), lambda b: (b, c))
    common = dict(t=t, grid=grid, tq=rows, n_sub=n_sub)
    nw = NA_HEADS * HEAD_DIM
    kw = GQA_KV_HEADS * HEAD_DIM
    o_a = _attention(q_mla, [(k_mla, blk(MLA_HEADS * MLA_PAD, 0), v_mla, blk(MLA_HEADS * HEAD_DIM, 0), None, None, 0)],
                     q_index=lambda b: (b, 0), n_heads=MLA_HEADS, group=1, dk=MLA_PAD, dv=HEAD_DIM,
                     name="ctx_mla", **common)
    o_b = _attention(na, [(na, blk(nw, 1), na, blk(nw, 2), None, None, 0)],
                     q_index=lambda b: (b, 0), n_heads=NA_HEADS, group=1, dk=HEAD_DIM, dv=HEAD_DIM,
                     name="ctx_na", **common)
    o_c = _attention(gq, [(gk, blk(kw, 0), gv, blk(kw, 0), None, None, 0)],
                     q_index=lambda b: (b, 0), n_heads=GQA_HEADS, group=GQA_HEADS // GQA_KV_HEADS,
                     dk=HEAD_DIM, dv=HEAD_DIM, name="ctx_gqa", **common)
    return o_a, o_b, o_c


def _latent_attention(q_mla, k_mla, v_mla, gq, gk, gv, na, kc_mla, vc_mla, caches, bias, l, seq):
    cache_na_k, cache_na_v, cache_gqa_k, cache_gqa_v = caches
    t = na.shape[0]
    past = kc_mla.shape[0] // (t // seq)
    own = lambda w, c: pl.BlockSpec((seq, w), lambda b, i: (b, c))
    flat = lambda w: pl.BlockSpec((past, w), lambda b, i: (b, 0))
    cached = lambda heads: pl.BlockSpec((None, None, past * heads, HEAD_DIM), lambda b, i: (b, l, 0, 0))
    qw, vw = MLA_HEADS * MLA_PAD, MLA_HEADS * HEAD_DIM
    nw = NA_HEADS * HEAD_DIM
    kw = GQA_KV_HEADS * HEAD_DIM
    o_c = _attention(gq, [(cache_gqa_k, cached(GQA_KV_HEADS), cache_gqa_v, cached(GQA_KV_HEADS), None, None,
                           GQA_KV_HEADS),
                          (gk, own(kw, 0), gv, own(kw, 0), None, None, 0)],
                     q_index=lambda b, i: (b, 0), n_heads=GQA_HEADS, group=GQA_HEADS // GQA_KV_HEADS,
                     dk=HEAD_DIM, dv=HEAD_DIM, name="lat_gqa",
                     t=t, grid=(t // seq, 1), tq=seq)
    tq = 512
    nq = seq // tq
    common = dict(t=t, grid=(t // seq, nq), tq=tq)
    o_a = _attention(q_mla, [(kc_mla, flat(qw), vc_mla, flat(vw), None, None, 0),
                             (k_mla, own(qw, 0), v_mla, own(vw, 0), None, None, 0)],
                     q_index=lambda b, i: (b * nq + i, 0), n_heads=MLA_HEADS, group=1, dk=MLA_PAD, dv=HEAD_DIM,
                     name="lat_mla", **common)
    bias_spec = pl.BlockSpec((NA_HEADS, tq, seq), lambda b, i: (0, i, 0))
    o_b = _attention(na, [(cache_na_k, cached(NA_HEADS), cache_na_v, cached(NA_HEADS), None, None, NA_HEADS),
                          (na, own(nw, 1), na, own(nw, 2), bias, bias_spec, 0)],
                     q_index=lambda b, i: (b * nq + i, 0), n_heads=NA_HEADS, group=1,
                     dk=HEAD_DIM, dv=HEAD_DIM, name="lat_na", **common)
    return o_a, o_b, o_c


def _outproj_kernel(x_ref, gt_ref, oa_ref, ob_ref, oc_ref, w_ref, o_ref):
    wa = oa_ref.shape[1]
    wb = ob_ref.shape[1]
    acc = jnp.dot(oa_ref[...], w_ref[0:wa, :], preferred_element_type=F32)
    acc += jnp.dot(ob_ref[...], w_ref[wa:wa + wb, :], preferred_element_type=F32)
    acc += jnp.dot(oc_ref[...], w_ref[wa + wb:, :], preferred_element_type=F32)
    o_ref[...] = x_ref[...] + gt_ref[0] * acc


def _outproj(x, mods, l, stream, o_a, o_b, o_c, w_out):
    t, d = x.shape
    tm = 512
    row = lambda a: pl.BlockSpec((tm, a.shape[1]), lambda i: (i, 0))
    return pl.pallas_call(
        _outproj_kernel,
        grid=(t // tm,),
        in_specs=[
            row(x),
            _mod_spec(l, 5, stream, tm),
            row(o_a), row(o_b), row(o_c),
            _resident((None, d, d), lambda i: (l, 0, 0)),
        ],
        out_specs=row(x),
        out_shape=jax.ShapeDtypeStruct((t, d), F32),
        compiler_params=_params("parallel"),
        name="outproj",
    )(x, mods, o_a, o_b, o_c, w_out)


def _rope_tables(seq, d):
    quarter = d // 4
    tt = np.arange(seq)
    pos = np.stack([tt // GRID_W, tt % GRID_W], axis=-1).astype(np.float64)
    inv = ROPE_THETA ** (-np.arange(quarter, dtype=np.float64) / quarter)
    ang = pos[:, :, None] * inv
    cos = np.cos(ang)
    sin = np.sin(ang)
    cos_t = np.stack([cos, cos], axis=2).reshape(seq, d)
    sin_t = np.stack([-sin, sin], axis=2).reshape(seq, d)
    pad = LANES - d
    if pad:
        cos_t = np.concatenate([cos_t, np.ones((seq, pad))], axis=-1)
        sin_t = np.concatenate([sin_t, np.zeros((seq, pad))], axis=-1)
    return jnp.asarray(cos_t, F32), jnp.asarray(sin_t, F32)


def _na_bias_kernel(t_ref, o_ref, *, rows, wh):
    r = pl.program_id(1)
    shape = (GRID_W, 2 * GRID_W)
    c = lax.broadcasted_iota(jnp.int32, shape, 0)
    lane = lax.broadcasted_iota(jnp.int32, shape, 1)
    kc = lane % GRID_W
    rs = jnp.clip(r - wh // 2, 0, rows - wh)
    cs = jnp.clip(c - NA_WIN_W // 2, 0, GRID_W - NA_WIN_W)
    col_ok = (kc >= cs) & (kc < cs + NA_WIN_W)
    for j in range(rows // 2):
        kr = 2 * j + lane // GRID_W
        ok = col_ok & (kr >= rs) & (kr < rs + wh)
        d = jnp.clip(2 * j - r + NA_WIN_H, 0, 2 * NA_WIN_H - 1)
        pair = jnp.broadcast_to(t_ref[0, pl.ds(d, 1), :], shape)
        toeplitz = pltpu.roll(pair, 2 * GRID_W - (NA_WIN_W - 1), 1, stride=1, stride_axis=0)
        o_ref[0, :, j * 2 * GRID_W:(j + 1) * 2 * GRID_W] = jnp.where(ok, toeplitz, NEG_INF)


def _na_bias(rpb, seq):
    nh, nr, nc = rpb.shape
    rows = seq // GRID_W
    wh = min(NA_WIN_H, rows)
    padded = jnp.pad(rpb.astype(F32), ((0, 0), (1, 1), (0, GRID_W - nc)))
    table = jnp.concatenate([padded[:, :nr + 1], padded[:, 1:]], axis=-1)
    return pl.pallas_call(
        functools.partial(_na_bias_kernel, rows=rows, wh=wh),
        grid=(nh, rows),
        in_specs=[pl.BlockSpec((1, nr + 1, 2 * GRID_W), lambda h, r: (h, 0, 0))],
        out_specs=pl.BlockSpec((1, GRID_W, seq), lambda h, r: (h, r, 0)),
        out_shape=jax.ShapeDtypeStruct((nh, seq, seq), F32),
        compiler_params=_params("parallel", "parallel"),
        name="na_bias",
    )(table)


def kernel(x_prompt, x_sample, cache_mla_ckv, cache_mla_krope, cache_na_k, cache_na_v, cache_gqa_k, cache_gqa_v, c, c_ctx, ada_w, ada_b, norm_g, ffn_wg, ffn_wu, ffn_wd, w_in, mla_q_norm, mla_wqb, mla_kv_norm, mla_wkvb, na_rpb, gqa_q_norm, gqa_k_norm, w_out, final_norm):
    depth = ada_w.shape[0]
    nb_ctx, seq_ctx, d = x_prompt.shape
    nb_lat, seq_lat, _ = x_sample.shape
    past = cache_mla_ckv.shape[2]

    wg = ffn_wg.astype(BF16)
    wu = ffn_wu
    wd = ffn_wd
    split = Q_LORA + KV_LORA + MLA_ROPE
    w_in_p = jnp.concatenate(
        [w_in[..., :split], jnp.zeros((depth, d, LANES - MLA_ROPE), w_in.dtype), w_in[..., split:]],
        axis=-1).astype(BF16)
    wqb = jnp.pad(mla_wqb.reshape(depth, Q_LORA, MLA_HEADS, MLA_QK),
                  ((0, 0), (0, 0), (0, 0), (0, MLA_PAD - MLA_QK))).reshape(depth, Q_LORA, MLA_HEADS * MLA_PAD)
    wkv = mla_wkvb.reshape(depth, KV_LORA, MLA_HEADS, 2 * HEAD_DIM)
    wts = {
        "q_norm": mla_q_norm.reshape(depth, 1, Q_LORA),
        "wqb": wqb.astype(BF16),
        "kv_norm": mla_kv_norm.reshape(depth, 1, KV_LORA),
        "wk": wkv[..., :MLA_NOPE].reshape(depth, KV_LORA, MLA_HEADS * MLA_NOPE).astype(BF16),
        "wv": wkv[..., MLA_NOPE:].reshape(depth, KV_LORA, MLA_HEADS * HEAD_DIM).astype(BF16),
        "gq_norm": gqa_q_norm.reshape(depth, 1, HEAD_DIM),
        "gk_norm": gqa_k_norm.reshape(depth, 1, HEAD_DIM),
    }
    w_out_b = w_out.astype(BF16)
    norm_g3 = norm_g.reshape(depth * 3, 1, d)
    final_g = final_norm.reshape(1, 1, d)
    tables = _rope_tables(seq_lat, MLA_ROPE) + _rope_tables(seq_lat, HEAD_DIM)
    kr_cache = jnp.pad(cache_mla_krope, ((0, 0), (0, 0), (0, 0), (0, LANES - MLA_ROPE)))
    caches = (cache_na_k.reshape(nb_lat, depth, past * NA_HEADS, HEAD_DIM),
              cache_na_v.reshape(nb_lat, depth, past * NA_HEADS, HEAD_DIM),
              cache_gqa_k.reshape(nb_lat, depth, past * GQA_KV_HEADS, HEAD_DIM),
              cache_gqa_v.reshape(nb_lat, depth, past * GQA_KV_HEADS, HEAD_DIM))

    cond = jnp.concatenate([c_ctx[None, :], c, jnp.zeros((MOD_ROWS - 1 - nb_lat, d), c.dtype)], axis=0)
    mods = _modulation(cond, ada_w, ada_b).reshape(depth * MOD_ROWS, 1, N_MOD * d)

    xp = x_prompt.reshape(nb_ctx * seq_ctx, d)
    xs = x_sample.reshape(nb_lat * seq_lat, d)
    ctx_row = (0, nb_ctx * seq_ctx)
    lat_row = (1, seq_lat)
    biases = [_na_bias(na_rpb[l], seq_lat) for l in range(depth)]

    new_cache = None
    for l in range(depth):
        last = l == depth - 1
        xp = _ffn(xp, mods, norm_g3, l, 0, 0, ctx_row, wg, wu, wd)
        outs = _qkv(xp, mods, norm_g3, l, ctx_row, w_in_p, wts, seq_ctx, cache=(nb_ctx, depth, new_cache))
        q_mla, k_mla, v_mla, gq, gk, gv, na = outs[:N_QKV_OUT]
        new_cache = outs[N_QKV_OUT:]
        o_a, o_b, o_c = _context_attention(q_mla, k_mla, v_mla, gq, gk, gv, na, seq_ctx)
        xp = _outproj(xp, mods, l, ctx_row, o_a, o_b, o_c, w_out_b)
        xp = _ffn(xp, mods, norm_g3, l, 2, 1, ctx_row, wg, wu, wd, final_g if last else None)
        xs = _ffn(xs, mods, norm_g3, l, 0, 0, lat_row, wg, wu, wd)
        q_mla, k_mla, v_mla, gq, gk, gv, na = _qkv(xs, mods, norm_g3, l, lat_row, w_in_p, wts, 512, tables=tables)
        kc_mla, vc_mla = _cache_kv(cache_mla_ckv, kr_cache, l, wts)
        o_a, o_b, o_c = _latent_attention(q_mla, k_mla, v_mla, gq, gk, gv, na, kc_mla, vc_mla, caches, biases[l], l,
                                          seq_lat)
        xs = _outproj(xs, mods, l, lat_row, o_a, o_b, o_c, w_out_b)
        xs = _ffn(xs, mods, norm_g3, l, 2, 1, lat_row, wg, wu, wd, final_g if last else None)

    y_prompt = xp.reshape(nb_ctx, seq_ctx, d)
    y_sample = xs.reshape(nb_lat, seq_lat, d)
    ckv_c, kr_c, nak_c, nav_c, gk_c, gv_c = new_cache
    na_shape = (nb_ctx, depth, seq_ctx, NA_HEADS, HEAD_DIM)
    gqa_shape = (nb_ctx, depth, seq_ctx, GQA_KV_HEADS, HEAD_DIM)
    return (y_prompt, y_sample, ckv_c, kr_c, nak_c.reshape(na_shape), nav_c.reshape(na_shape),
            gk_c.reshape(gqa_shape), gv_c.reshape(gqa_shape))
```

```python
import functools

import jax
import jax.numpy as jnp
import numpy as np
from jax import lax
from jax.experimental import pallas as pl
from jax.experimental.pallas import tpu as pltpu

F32 = jnp.float32
BF16 = jnp.bfloat16

D_MODEL = 2048
D_FF = 5632
N_MOD = 9
GRID_W = 64
ROPE_THETA = 10000.0
HEAD_DIM = 128
MLA_HEADS = 8
MLA_NOPE = 128
MLA_ROPE = 64
MLA_QK = MLA_NOPE + MLA_ROPE
MLA_PAD = 256
Q_LORA = 512
KV_LORA = 256
NA_HEADS = 4
NA_WIN_H = 8
NA_WIN_W = 16
GQA_HEADS = 4
GQA_KV_HEADS = 2
EPS = 1e-6
NEG_INF = -1e30
LOG2E = 1.4426950408889634
MOD_ROWS = 16

LANES = 128
COL_CQ = 0
COL_CKV = COL_CQ + Q_LORA
COL_KR = COL_CKV + KV_LORA
COL_NA = COL_KR + LANES
COL_GQ = COL_NA + 3 * NA_HEADS * HEAD_DIM
COL_GK = COL_GQ + GQA_HEADS * HEAD_DIM
COL_GV = COL_GK + GQA_KV_HEADS * HEAD_DIM
PROJ_W = COL_GV + GQA_KV_HEADS * HEAD_DIM

VMEM_LIMIT = 62 * 1024 * 1024


def _params(*sem):
    return pltpu.CompilerParams(dimension_semantics=sem, vmem_limit_bytes=VMEM_LIMIT)


def _resident(shape, index_map):
    return pl.BlockSpec(shape, index_map, pipeline_mode=pl.Buffered(1))


def _as_bf16(ref):
    v = ref[...]
    return v if v.dtype == BF16 else v.astype(BF16)


def _rms(x, g):
    return x * lax.rsqrt(jnp.mean(x * x, axis=-1, keepdims=True) + EPS) * g


NORM_ROWS = 256


def _norm_modulate_store(x_ref, ng_ref, sc_ref, sh_ref, h_scr):
    gain = ng_ref[0] * (1.0 + sc_ref[0])
    shift = sh_ref[0]

    def body(i, carry):
        rows = pl.ds(pl.multiple_of(i * NORM_ROWS, NORM_ROWS), NORM_ROWS)
        x = x_ref[rows, :]
        r = lax.rsqrt(jnp.mean(x * x, axis=-1, keepdims=True) + EPS)
        h_scr[rows, :] = (x * r * gain + shift).astype(BF16)
        return carry

    lax.fori_loop(0, x_ref.shape[0] // NORM_ROWS, body, 0)


def _rope(x, cos, sin, quarter):
    n = x.shape[-1]
    lane = lax.broadcasted_iota(jnp.int32, x.shape, 1)
    first = (lane % (2 * quarter)) < quarter
    sw = jnp.where(first, pltpu.roll(x, n - quarter, 1), pltpu.roll(x, quarter, 1))
    return x * cos + sw * sin


def _mod_kernel(c_ref, w_ref, b_ref, o_ref):
    c = c_ref[...]
    s = (c * jax.nn.sigmoid(c)).astype(BF16)
    o_ref[0] = jnp.dot(s, w_ref[0].astype(BF16), preferred_element_type=F32) + b_ref[0]


def _modulation(cond, ada_w, ada_b):
    depth, d, n = ada_w.shape
    tn = 512
    return pl.pallas_call(
        _mod_kernel,
        grid=(depth, n // tn),
        in_specs=[
            pl.BlockSpec((MOD_ROWS, d), lambda l, j: (0, 0)),
            pl.BlockSpec((1, d, tn), lambda l, j: (l, 0, j)),
            pl.BlockSpec((1, 1, tn), lambda l, j: (l, 0, j)),
        ],
        out_specs=pl.BlockSpec((1, MOD_ROWS, tn), lambda l, j: (l, 0, j)),
        out_shape=jax.ShapeDtypeStruct((depth, MOD_ROWS, n), F32),
        compiler_params=_params("parallel", "parallel"),
        name="modulation",
    )(cond, ada_w, ada_b.reshape(depth, 1, n))


def _mod_spec(l, chunk, stream, tm):
    base, tokens = stream
    return pl.BlockSpec((1, 1, D_MODEL), lambda i, *_: (l * MOD_ROWS + base + (i * tm) // tokens, 0, chunk))


def _ffn_kernel(x_ref, ng_ref, sh_ref, sc_ref, gt_ref, wg_ref, wu_ref, wd_ref, *rest, final):
    if final:
        fg_ref, o_ref, h_scr = rest
    else:
        o_ref, h_scr = rest
    f = pl.program_id(1)

    @pl.when(f == 0)
    def _():
        _norm_modulate_store(x_ref, ng_ref, sc_ref, sh_ref, h_scr)
        o_ref[...] = jnp.zeros_like(o_ref)

    h = h_scr[...]
    g = jnp.dot(h, _as_bf16(wg_ref), preferred_element_type=F32)
    u = jnp.dot(h, _as_bf16(wu_ref), preferred_element_type=F32)
    a = (g * jax.nn.sigmoid(g) * u).astype(BF16)
    o_ref[...] += jnp.dot(a, _as_bf16(wd_ref), preferred_element_type=F32)

    @pl.when(f == pl.num_programs(1) - 1)
    def _():
        gate = 0.5 * gt_ref[0]

        def body(i, carry):
            rows = pl.ds(pl.multiple_of(i * NORM_ROWS, NORM_ROWS), NORM_ROWS)
            y = x_ref[rows, :] + gate * o_ref[rows, :]
            if final:
                y = _rms(y, fg_ref[0])
            o_ref[rows, :] = y
            return carry

        lax.fori_loop(0, o_ref.shape[0] // NORM_ROWS, body, 0)


def _ffn(x, mods, norm_g, l, sub, k, stream, wg, wu, wd, final_g=None):
    t, d = x.shape
    tm, tf = 1024, 512
    chunk = 3 * sub
    in_specs = [
        pl.BlockSpec((tm, d), lambda i, j: (i, 0), pipeline_mode=pl.Buffered(1)),
        pl.BlockSpec((1, 1, d), lambda i, j: (l * 3 + sub, 0, 0)),
        _mod_spec(l, chunk, stream, tm),
        _mod_spec(l, chunk + 1, stream, tm),
        _mod_spec(l, chunk + 2, stream, tm),
        pl.BlockSpec((None, None, d, tf), lambda i, j: (l, k, 0, j)),
        pl.BlockSpec((None, None, d, tf), lambda i, j: (l, k, 0, j)),
        pl.BlockSpec((None, None, tf, d), lambda i, j: (l, k, j, 0)),
    ]
    args = [x, norm_g, mods, mods, mods, wg, wu, wd]
    if final_g is not None:
        in_specs.append(pl.BlockSpec((1, 1, d), lambda i, j: (0, 0, 0)))
        args.append(final_g)
    return pl.pallas_call(
        functools.partial(_ffn_kernel, final=final_g is not None),
        grid=(t // tm, D_FF // tf),
        in_specs=in_specs,
        out_specs=pl.BlockSpec((tm, d), lambda i, j: (i, 0)),
        out_shape=jax.ShapeDtypeStruct((t, d), F32),
        scratch_shapes=[pltpu.VMEM((tm, d), BF16)],
        compiler_params=_params("parallel", "arbitrary"),
        name="ffn",
    )(*args)


def _expand_kv(ckv_bf, kr, wk_ref, wv_ref, k_out, v_out):
    kn = jnp.dot(ckv_bf, wk_ref[...], preferred_element_type=F32)
    v_out[...] = jnp.dot(ckv_bf, wv_ref[...], preferred_element_type=F32).astype(BF16)
    kr_bf = kr.astype(BF16)
    for h in range(MLA_HEADS):
        k_out[:, h * MLA_PAD:h * MLA_PAD + MLA_NOPE] = kn[:, h * MLA_NOPE:(h + 1) * MLA_NOPE].astype(BF16)
        k_out[:, h * MLA_PAD + MLA_NOPE:(h + 1) * MLA_PAD] = kr_bf


N_QKV_OUT = 7
N_CACHE_OUT = 6


def _qkv_kernel(x_ref, ng_ref, sh_ref, sc_ref, w_ref, qn_ref, wqb_ref, kvn_ref, wk_ref, wv_ref, gqn_ref, gkn_ref,
                *rest, rope, n_prev, cache_out, layer):
    if rope:
        c64_ref, s64_ref, c128_ref, s128_ref = rest[:4]
        rest = rest[4:]
        rot64 = lambda v: _rope(v, c64_ref[...], s64_ref[...], MLA_ROPE // 4)
        rot128 = lambda v: _rope(v, c128_ref[...], s128_ref[...], HEAD_DIM // 4)
    else:
        rot64 = rot128 = lambda v: v
    rest = rest[n_prev:]
    q_out, k_out, v_out, gq_out, gk_out, gv_out, na_out = rest[:N_QKV_OUT]
    rest = rest[N_QKV_OUT:]
    if cache_out:
        cache_refs = rest[:N_CACHE_OUT]
        rest = rest[N_CACHE_OUT:]
        if n_prev == 0:
            for ref in cache_refs:
                for other in range(ref.shape[0]):
                    if other != layer:
                        ref[other] = jnp.zeros(ref.shape[1:], ref.dtype)
            cache_refs = [ref.at[layer] for ref in cache_refs]
        ckv_c, kr_c, nak_c, nav_c, gk_c, gv_c = cache_refs
    (h_scr,) = rest
    tm = x_ref.shape[0]

    _norm_modulate_store(x_ref, ng_ref, sc_ref, sh_ref, h_scr)
    h = h_scr[...]
    low = jnp.dot(h, w_ref[:, COL_CQ:COL_NA], preferred_element_type=F32)
    na = jnp.dot(h, w_ref[:, COL_NA:COL_GQ], preferred_element_type=F32)
    gg = jnp.dot(h, w_ref[:, COL_GQ:PROJ_W], preferred_element_type=F32)

    qn = _rms(low[:, COL_CQ:COL_CKV], qn_ref[0]).astype(BF16)
    q = jnp.dot(qn, wqb_ref[...], preferred_element_type=F32) * (MLA_QK ** -0.5 * LOG2E)
    for hh in range(MLA_HEADS):
        lo = hh * MLA_PAD
        q_out[:, lo:lo + MLA_NOPE] = q[:, lo:lo + MLA_NOPE].astype(BF16)
        q_out[:, lo + MLA_NOPE:lo + MLA_PAD] = rot64(q[:, lo + MLA_NOPE:lo + MLA_PAD]).astype(BF16)

    ckv = _rms(low[:, COL_CKV:COL_KR], kvn_ref[0])
    kr = rot64(low[:, COL_KR:COL_NA])
    _expand_kv(ckv.astype(BF16), kr, wk_ref, wv_ref, k_out, v_out)

    nw = NA_HEADS * HEAD_DIM
    gw = GQA_HEADS * HEAD_DIM
    kw = GQA_KV_HEADS * HEAD_DIM
    na_out[:, :nw] = (na[:, :nw] * (HEAD_DIM ** -0.5 * LOG2E)).astype(BF16)
    na_out[:, nw:] = na[:, nw:].astype(BF16)
    gq_gain = gqn_ref[0] * (HEAD_DIM ** -0.5 * LOG2E)
    for hh in range(GQA_HEADS):
        sl = slice(hh * HEAD_DIM, (hh + 1) * HEAD_DIM)
        gq_out[:, sl] = rot128(_rms(gg[:, sl], gq_gain)).astype(BF16)
    gv = gg[:, gw + kw:]
    gv_out[...] = gv.astype(BF16)
    for hh in range(GQA_KV_HEADS):
        sl = slice(hh * HEAD_DIM, (hh + 1) * HEAD_DIM)
        gk = rot128(_rms(gg[:, gw + hh * HEAD_DIM:gw + (hh + 1) * HEAD_DIM], gkn_ref[0]))
        gk_out[:, sl] = gk.astype(BF16)
        if cache_out:
            gk_c[pl.ds(hh, tm, stride=GQA_KV_HEADS), :] = gk
            gv_c[pl.ds(hh, tm, stride=GQA_KV_HEADS), :] = gv[:, sl]
    if cache_out:
        ckv_c[...] = ckv
        kr_c[...] = kr[:, :MLA_ROPE]
        for hh in range(NA_HEADS):
            sl = slice(hh * HEAD_DIM, (hh + 1) * HEAD_DIM)
            nak_c[pl.ds(hh, tm, stride=NA_HEADS), :] = na[:, nw + hh * HEAD_DIM:nw + (hh + 1) * HEAD_DIM]
            nav_c[pl.ds(hh, tm, stride=NA_HEADS), :] = na[:, 2 * nw + hh * HEAD_DIM:2 * nw + (hh + 1) * HEAD_DIM]


def _cache_shapes(nb, depth, seq):
    return [
        (nb, depth, seq, KV_LORA), (nb, depth, seq, MLA_ROPE),
        (nb, depth, seq * NA_HEADS, HEAD_DIM), (nb, depth, seq * NA_HEADS, HEAD_DIM),
        (nb, depth, seq * GQA_KV_HEADS, HEAD_DIM), (nb, depth, seq * GQA_KV_HEADS, HEAD_DIM),
    ]


def _qkv(x, mods, norm_g, l, stream, w_in, wts, tm, tables=None, cache=None):
    t, d = x.shape
    rope = tables is not None
    qw = MLA_HEADS * MLA_PAD
    vw = MLA_HEADS * HEAD_DIM
    in_specs = [
        pl.BlockSpec((tm, d), lambda i: (i, 0)),
        pl.BlockSpec((1, 1, d), lambda i: (l * 3 + 1, 0, 0)),
        _mod_spec(l, 3, stream, tm),
        _mod_spec(l, 4, stream, tm),
        _resident((None, d, PROJ_W), lambda i: (l, 0, 0)),
        pl.BlockSpec((1, 1, Q_LORA), lambda i: (l, 0, 0)),
        _resident((None, Q_LORA, qw), lambda i: (l, 0, 0)),
        pl.BlockSpec((1, 1, KV_LORA), lambda i: (l, 0, 0)),
        _resident((None, KV_LORA, vw), lambda i: (l, 0, 0)),
        _resident((None, KV_LORA, vw), lambda i: (l, 0, 0)),
        pl.BlockSpec((1, 1, HEAD_DIM), lambda i: (l, 0, 0)),
        pl.BlockSpec((1, 1, HEAD_DIM), lambda i: (l, 0, 0)),
    ]
    args = [x, norm_g, mods, mods, w_in, wts["q_norm"], wts["wqb"], wts["kv_norm"], wts["wk"], wts["wv"],
            wts["gq_norm"], wts["gk_norm"]]
    if rope:
        per = tables[0].shape[0] // tm
        in_specs += [pl.BlockSpec((tm, LANES), lambda i: (i % per, 0))] * 4
        args += list(tables)
    row = lambda w: pl.BlockSpec((tm, w), lambda i: (i, 0))
    widths = [qw, qw, vw, GQA_HEADS * HEAD_DIM, GQA_KV_HEADS * HEAD_DIM, GQA_KV_HEADS * HEAD_DIM,
              3 * NA_HEADS * HEAD_DIM]
    out_specs = [row(w) for w in widths]
    out_shape = [jax.ShapeDtypeStruct((t, w), BF16) for w in widths]
    aliases = {}
    n_prev = 0
    if cache is not None:
        nb, depth, prev = cache
        assert tm * nb == t
        shapes = _cache_shapes(nb, depth, tm)
        if prev is not None:
            n_prev = len(prev)
            aliases = {len(args) + k: N_QKV_OUT + k for k in range(n_prev)}
            in_specs += [pl.BlockSpec(memory_space=pl.ANY)] * n_prev
            args += list(prev)
            out_specs += [pl.BlockSpec((None, None) + s[2:], lambda i: (i, l, 0, 0)) for s in shapes]
        else:
            out_specs += [pl.BlockSpec((None,) + s[1:], lambda i: (i, 0, 0, 0)) for s in shapes]
        out_shape += [jax.ShapeDtypeStruct(s, F32) for s in shapes]
    return pl.pallas_call(
        functools.partial(_qkv_kernel, rope=rope, n_prev=n_prev, cache_out=cache is not None, layer=l),
        grid=(t // tm,),
        in_specs=in_specs,
        out_specs=out_specs,
        out_shape=out_shape,
        input_output_aliases=aliases,
        scratch_shapes=[pltpu.VMEM((tm, d), BF16)],
        compiler_params=_params("parallel"),
        name="qkv",
    )(*args)


def _cache_kv_kernel(ckv_ref, kr_ref, wk_ref, wv_ref, k_out, v_out):
    _expand_kv(ckv_ref[...].astype(BF16), kr_ref[...], wk_ref, wv_ref, k_out, v_out)


def _cache_kv(cache_ckv, cache_kr_pad, l, wts):
    nb, _, past, _ = cache_ckv.shape
    qw = MLA_HEADS * MLA_PAD
    vw = MLA_HEADS * HEAD_DIM
    return pl.pallas_call(
        _cache_kv_kernel,
        grid=(nb,),
        in_specs=[
            pl.BlockSpec((None, None, past, KV_LORA), lambda b: (b, l, 0, 0)),
            pl.BlockSpec((None, None, past, LANES), lambda b: (b, l, 0, 0)),
            pl.BlockSpec((None, KV_LORA, vw), lambda b: (l, 0, 0)),
            pl.BlockSpec((None, KV_LORA, vw), lambda b: (l, 0, 0)),
        ],
        out_specs=[pl.BlockSpec((past, qw), lambda b: (b, 0)), pl.BlockSpec((past, vw), lambda b: (b, 0))],
        out_shape=[jax.ShapeDtypeStruct((nb * past, qw), BF16), jax.ShapeDtypeStruct((nb * past, vw), BF16)],
        compiler_params=_params("parallel"),
        name="cache_kv",
    )(cache_ckv, cache_kr_pad, wts["wk"], wts["wv"])


def _attn_kernel(*refs, n_heads, group, dk, dv, seg_kinds, n_sub):
    q_ref = refs[0]
    pos = 1
    segs = []
    for has_bias, inter in seg_kinds:
        b_ref = refs[pos + 2] if has_bias else None
        segs.append((refs[pos], refs[pos + 1], b_ref, inter))
        pos += 3 if has_bias else 2
    o_ref = refs[pos]

    def head_rows(ref, rows, hk, width, inter):
        if inter:
            return ref[pl.ds(hk, ref.shape[0] // inter, stride=inter), :]
        return ref[rows, hk * width:(hk + 1) * width]

    def attend(q_rows, seg_rows):
        for h in range(n_heads):
            hk = h // group
            q = q_ref[q_rows, h * dk:(h + 1) * dk].astype(BF16)
            scores = []
            for (k_ref, _, b_ref, inter), rows in zip(segs, seg_rows):
                k = head_rows(k_ref, rows, hk, dk, inter).astype(BF16)
                s = lax.dot_general(q, k, (((1,), (1,)), ((), ())), preferred_element_type=F32)
                if b_ref is not None:
                    s = s + b_ref[h]
                scores.append(s)
            m = functools.reduce(jnp.maximum, [s.max(axis=-1, keepdims=True) for s in scores])
            o = None
            for s, (_, v_ref, _, inter), rows in zip(scores, segs, seg_rows):
                p = jnp.exp2(s - m).astype(BF16)
                v = head_rows(v_ref, rows, hk, dv, inter).astype(BF16)
                pv = jnp.dot(p, jnp.concatenate([v, jnp.ones_like(v)], axis=1), preferred_element_type=F32)
                o = pv if o is None else o + pv
            o_ref[q_rows, h * dv:(h + 1) * dv] = (o[:, :dv] / o[:, dv:]).astype(o_ref.dtype)

    if n_sub == 1:
        attend(slice(None), [slice(None)] * len(segs))
    else:
        def body(s, carry):
            rows = lambda ref: pl.ds(pl.multiple_of(s * (ref.shape[0] // n_sub), ref.shape[0] // n_sub),
                                     ref.shape[0] // n_sub)
            attend(rows(q_ref), [rows(seg[0]) for seg in segs])
            return carry

        lax.fori_loop(0, n_sub, body, 0)


def _attention(q, segs, *, t, grid, tq, q_index, n_heads, group, dk, dv, name, n_sub=1):
    assert n_sub == 1 or all(s[4] is None and not s[6] for s in segs)
    in_specs = [pl.BlockSpec((tq, n_heads * dk), q_index)]
    args = [q]
    seg_kinds = []
    for k_arr, k_spec, v_arr, v_spec, b_arr, b_spec, inter in segs:
        in_specs += [k_spec, v_spec]
        args += [k_arr, v_arr]
        seg_kinds.append((b_arr is not None, inter))
        if b_arr is not None:
            in_specs.append(b_spec)
            args.append(b_arr)
    out_index = lambda *g: (q_index(*g)[0], 0)
    return pl.pallas_call(
        functools.partial(_attn_kernel, n_heads=n_heads, group=group, dk=dk, dv=dv,
                          seg_kinds=tuple(seg_kinds), n_sub=n_sub),
        grid=grid,
        in_specs=in_specs,
        out_specs=pl.BlockSpec((tq, n_heads * dv), out_index),
        out_shape=jax.ShapeDtypeStruct((t, n_heads * dv), BF16),
        compiler_params=_params(*(("parallel",) * len(grid))),
        name=name,
    )(*args)


def _context_attention(q_mla, k_mla, v_mla, gq, gk, gv, na, seq):
    t = na.shape[0]
    n_sub = 4
    rows = seq * n_sub
    grid = (t // rows,)
    blk = lambda w, c: pl.BlockSpec((rows, w), lambda b: (b, c))
    common = dict(t=t, grid=grid, tq=rows, n_sub=n_sub)
    nw = NA_HEADS * HEAD_DIM
    kw = GQA_KV_HEADS * HEAD_DIM
    o_a = _attention(q_mla, [(k_mla, blk(MLA_HEADS * MLA_PAD, 0), v_mla, blk(MLA_HEADS * HEAD_DIM, 0), None, None, 0)],
                     q_index=lambda b: (b, 0), n_heads=MLA_HEADS, group=1, dk=MLA_PAD, dv=HEAD_DIM,
                     name="ctx_mla", **common)
    o_b = _attention(na, [(na, blk(nw, 1), na, blk(nw, 2), None, None, 0)],
                     q_index=lambda b: (b, 0), n_heads=NA_HEADS, group=1, dk=HEAD_DIM, dv=HEAD_DIM,
                     name="ctx_na", **common)
    o_c = _attention(gq, [(gk, blk(kw, 0), gv, blk(kw, 0), None, None, 0)],
                     q_index=lambda b: (b, 0), n_heads=GQA_HEADS, group=GQA_HEADS // GQA_KV_HEADS,
                     dk=HEAD_DIM, dv=HEAD_DIM, name="ctx_gqa", **common)
    return o_a, o_b, o_c


def _latent_attention(q_mla, k_mla, v_mla, gq, gk, gv, na, kc_mla, vc_mla, caches, bias, l, seq):
    cache_na_k, cache_na_v, cache_gqa_k, cache_gqa_v = caches
    t = na.shape[0]
    past = kc_mla.shape[0] // (t // seq)
    own = lambda w, c: pl.BlockSpec((seq, w), lambda b, i: (b, c))
    flat = lambda w: pl.BlockSpec((past, w), lambda b, i: (b, 0))
    cached = lambda heads: pl.BlockSpec((None, None, past * heads, HEAD_DIM), lambda b, i: (b, l, 0, 0))
    qw, vw = MLA_HEADS * MLA_PAD, MLA_HEADS * HEAD_DIM
    nw = NA_HEADS * HEAD_DIM
    kw = GQA_KV_HEADS * HEAD_DIM
    o_c = _attention(gq, [(cache_gqa_k, cached(GQA_KV_HEADS), cache_gqa_v, cached(GQA_KV_HEADS), None, None,
                           GQA_KV_HEADS),
                          (gk, own(kw, 0), gv, own(kw, 0), None, None, 0)],
                     q_index=lambda b, i: (b, 0), n_heads=GQA_HEADS, group=GQA_HEADS // GQA_KV_HEADS,
                     dk=HEAD_DIM, dv=HEAD_DIM, name="lat_gqa",
                     t=t, grid=(t // seq, 1), tq=seq)
    tq = 512
    nq = seq // tq
    common = dict(t=t, grid=(t // seq, nq), tq=tq)
    o_a = _attention(q_mla, [(kc_mla, flat(qw), vc_mla, flat(vw), None, None, 0),
                             (k_mla, own(qw, 0), v_mla, own(vw, 0), None, None, 0)],
                     q_index=lambda b, i: (b * nq + i, 0), n_heads=MLA_HEADS, group=1, dk=MLA_PAD, dv=HEAD_DIM,
                     name="lat_mla", **common)
    bias_spec = pl.BlockSpec((NA_HEADS, tq, seq), lambda b, i: (0, i, 0))
    o_b = _attention(na, [(cache_na_k, cached(NA_HEADS), cache_na_v, cached(NA_HEADS), None, None, NA_HEADS),
                          (na, own(nw, 1), na, own(nw, 2), bias, bias_spec, 0)],
                     q_index=lambda b, i: (b * nq + i, 0), n_heads=NA_HEADS, group=1,
                     dk=HEAD_DIM, dv=HEAD_DIM, name="lat_na", **common)
    return o_a, o_b, o_c


def _outproj_kernel(x_ref, gt_ref, oa_ref, ob_ref, oc_ref, w_ref, o_ref):
    wa = oa_ref.shape[1]
    wb = ob_ref.shape[1]
    acc = jnp.dot(oa_ref[...], w_ref[0:wa, :], preferred_element_type=F32)
    acc += jnp.dot(ob_ref[...], w_ref[wa:wa + wb, :], preferred_element_type=F32)
    acc += jnp.dot(oc_ref[...], w_ref[wa + wb:, :], preferred_element_type=F32)
    o_ref[...] = x_ref[...] + gt_ref[0] * acc


def _outproj(x, mods, l, stream, o_a, o_b, o_c, w_out):
    t, d = x.shape
    tm = 512
    row = lambda a: pl.BlockSpec((tm, a.shape[1]), lambda i: (i, 0))
    return pl.pallas_call(
        _outproj_kernel,
        grid=(t // tm,),
        in_specs=[
            row(x),
            _mod_spec(l, 5, stream, tm),
            row(o_a), row(o_b), row(o_c),
            _resident((None, d, d), lambda i: (l, 0, 0)),
        ],
        out_specs=row(x),
        out_shape=jax.ShapeDtypeStruct((t, d), F32),
        compiler_params=_params("parallel"),
        name="outproj",
    )(x, mods, o_a, o_b, o_c, w_out)


def _rope_tables(seq, d):
    quarter = d // 4
    tt = np.arange(seq)
    pos = np.stack([tt // GRID_W, tt % GRID_W], axis=-1).astype(np.float64)
    inv = ROPE_THETA ** (-np.arange(quarter, dtype=np.float64) / quarter)
    ang = pos[:, :, None] * inv
    cos = np.cos(ang)
    sin = np.sin(ang)
    cos_t = np.stack([cos, cos], axis=2).reshape(seq, d)
    sin_t = np.stack([-sin, sin], axis=2).reshape(seq, d)
    pad = LANES - d
    if pad:
        cos_t = np.concatenate([cos_t, np.ones((seq, pad))], axis=-1)
        sin_t = np.concatenate([sin_t, np.zeros((seq, pad))], axis=-1)
    return jnp.asarray(cos_t, F32), jnp.asarray(sin_t, F32)


def _na_bias_kernel(t_ref, o_ref, *, rows, wh):
    r = pl.program_id(1)
    shape = (GRID_W, 2 * GRID_W)
    c = lax.broadcasted_iota(jnp.int32, shape, 0)
    lane = lax.broadcasted_iota(jnp.int32, shape, 1)
    kc = lane % GRID_W
    rs = jnp.clip(r - wh // 2, 0, rows - wh)
    cs = jnp.clip(c - NA_WIN_W // 2, 0, GRID_W - NA_WIN_W)
    col_ok = (kc >= cs) & (kc < cs + NA_WIN_W)
    for j in range(rows // 2):
        kr = 2 * j + lane // GRID_W
        ok = col_ok & (kr >= rs) & (kr < rs + wh)
        d = jnp.clip(2 * j - r + NA_WIN_H, 0, 2 * NA_WIN_H - 1)
        pair = jnp.broadcast_to(t_ref[0, pl.ds(d, 1), :], shape)
        toeplitz = pltpu.roll(pair, 2 * GRID_W - (NA_WIN_W - 1), 1, stride=1, stride_axis=0)
        o_ref[0, :, j * 2 * GRID_W:(j + 1) * 2 * GRID_W] = jnp.where(ok, toeplitz * LOG2E, NEG_INF)


def _na_bias(rpb, seq):
    nh, nr, nc = rpb.shape
    rows = seq // GRID_W
    wh = min(NA_WIN_H, rows)
    padded = jnp.pad(rpb.astype(F32), ((0, 0), (1, 1), (0, GRID_W - nc)))
    table = jnp.concatenate([padded[:, :nr + 1], padded[:, 1:]], axis=-1)
    return pl.pallas_call(
        functools.partial(_na_bias_kernel, rows=rows, wh=wh),
        grid=(nh, rows),
        in_specs=[pl.BlockSpec((1, nr + 1, 2 * GRID_W), lambda h, r: (h, 0, 0))],
        out_specs=pl.BlockSpec((1, GRID_W, seq), lambda h, r: (h, r, 0)),
        out_shape=jax.ShapeDtypeStruct((nh, seq, seq), F32),
        compiler_params=_params("parallel", "parallel"),
        name="na_bias",
    )(table)


def kernel(x_prompt, x_sample, cache_mla_ckv, cache_mla_krope, cache_na_k, cache_na_v, cache_gqa_k, cache_gqa_v, c, c_ctx, ada_w, ada_b, norm_g, ffn_wg, ffn_wu, ffn_wd, w_in, mla_q_norm, mla_wqb, mla_kv_norm, mla_wkvb, na_rpb, gqa_q_norm, gqa_k_norm, w_out, final_norm):
    depth = ada_w.shape[0]
    nb_ctx, seq_ctx, d = x_prompt.shape
    nb_lat, seq_lat, _ = x_sample.shape
    past = cache_mla_ckv.shape[2]

    wg = ffn_wg
    wu = ffn_wu
    wd = ffn_wd
    split = Q_LORA + KV_LORA + MLA_ROPE
    w_in_p = jnp.concatenate(
        [w_in[..., :split], jnp.zeros((depth, d, LANES - MLA_ROPE), w_in.dtype), w_in[..., split:]],
        axis=-1).astype(BF16)
    wqb = jnp.pad(mla_wqb.reshape(depth, Q_LORA, MLA_HEADS, MLA_QK),
                  ((0, 0), (0, 0), (0, 0), (0, MLA_PAD - MLA_QK))).reshape(depth, Q_LORA, MLA_HEADS * MLA_PAD)
    wkv = mla_wkvb.reshape(depth, KV_LORA, MLA_HEADS, 2 * HEAD_DIM)
    wts = {
        "q_norm": mla_q_norm.reshape(depth, 1, Q_LORA),
        "wqb": wqb.astype(BF16),
        "kv_norm": mla_kv_norm.reshape(depth, 1, KV_LORA),
        "wk": wkv[..., :MLA_NOPE].reshape(depth, KV_LORA, MLA_HEADS * MLA_NOPE).astype(BF16),
        "wv": wkv[..., MLA_NOPE:].reshape(depth, KV_LORA, MLA_HEADS * HEAD_DIM).astype(BF16),
        "gq_norm": gqa_q_norm.reshape(depth, 1, HEAD_DIM),
        "gk_norm": gqa_k_norm.reshape(depth, 1, HEAD_DIM),
    }
    w_out_b = w_out.astype(BF16)
    norm_g3 = norm_g.reshape(depth * 3, 1, d)
    final_g = final_norm.reshape(1, 1, d)
    tables = _rope_tables(seq_lat, MLA_ROPE) + _rope_tables(seq_lat, HEAD_DIM)
    kr_cache = jnp.pad(cache_mla_krope, ((0, 0), (0, 0), (0, 0), (0, LANES - MLA_ROPE)))
    caches = (cache_na_k.reshape(nb_lat, depth, past * NA_HEADS, HEAD_DIM),
              cache_na_v.reshape(nb_lat, depth, past * NA_HEADS, HEAD_DIM),
              cache_gqa_k.reshape(nb_lat, depth, past * GQA_KV_HEADS, HEAD_DIM),
              cache_gqa_v.reshape(nb_lat, depth, past * GQA_KV_HEADS, HEAD_DIM))

    cond = jnp.concatenate([c_ctx[None, :], c, jnp.zeros((MOD_ROWS - 1 - nb_lat, d), c.dtype)], axis=0)
    mods = _modulation(cond, ada_w, ada_b).reshape(depth * MOD_ROWS, 1, N_MOD * d)

    xp = x_prompt.reshape(nb_ctx * seq_ctx, d)
    xs = x_sample.reshape(nb_lat * seq_lat, d)
    ctx_row = (0, nb_ctx * seq_ctx)
    lat_row = (1, seq_lat)
    biases = [_na_bias(na_rpb[l], seq_lat) for l in range(depth)]

    new_cache = None
    for l in range(depth):
        last = l == depth - 1
        xp = _ffn(xp, mods, norm_g3, l, 0, 0, ctx_row, wg, wu, wd)
        outs = _qkv(xp, mods, norm_g3, l, ctx_row, w_in_p, wts, seq_ctx, cache=(nb_ctx, depth, new_cache))
        q_mla, k_mla, v_mla, gq, gk, gv, na = outs[:N_QKV_OUT]
        new_cache = outs[N_QKV_OUT:]
        o_a, o_b, o_c = _context_attention(q_mla, k_mla, v_mla, gq, gk, gv, na, seq_ctx)
        xp = _outproj(xp, mods, l, ctx_row, o_a, o_b, o_c, w_out_b)
        xp = _ffn(xp, mods, norm_g3, l, 2, 1, ctx_row, wg, wu, wd, final_g if last else None)
        xs = _ffn(xs, mods, norm_g3, l, 0, 0, lat_row, wg, wu, wd)
        q_mla, k_mla, v_mla, gq, gk, gv, na = _qkv(xs, mods, norm_g3, l, lat_row, w_in_p, wts, 512, tables=tables)
        kc_mla, vc_mla = _cache_kv(cache_mla_ckv, kr_cache, l, wts)
        o_a, o_b, o_c = _latent_attention(q_mla, k_mla, v_mla, gq, gk, gv, na, kc_mla, vc_mla, caches, biases[l], l,
                                          seq_lat)
        xs = _outproj(xs, mods, l, lat_row, o_a, o_b, o_c, w_out_b)
        xs = _ffn(xs, mods, norm_g3, l, 2, 1, lat_row, wg, wu, wd, final_g if last else None)

    y_prompt = xp.reshape(nb_ctx, seq_ctx, d)
    y_sample = xs.reshape(nb_lat, seq_lat, d)
    ckv_c, kr_c, nak_c, nav_c, gk_c, gv_c = new_cache
    na_shape = (nb_ctx, depth, seq_ctx, NA_HEADS, HEAD_DIM)
    gqa_shape = (nb_ctx, depth, seq_ctx, GQA_KV_HEADS, HEAD_DIM)
    return (y_prompt, y_sample, ckv_c, kr_c, nak_c.reshape(na_shape), nav_c.reshape(na_shape),
            gk_c.reshape(gqa_shape), gv_c.reshape(gqa_shape))
```

```python
import functools

import jax
import jax.numpy as jnp
import numpy as np
from jax import lax
from jax.experimental import pallas as pl
from jax.experimental.pallas import tpu as pltpu

F32 = jnp.float32
BF16 = jnp.bfloat16

D_MODEL = 2048
D_FF = 5632
N_MOD = 9
GRID_W = 64
ROPE_THETA = 10000.0
HEAD_DIM = 128
MLA_HEADS = 8
MLA_NOPE = 128
MLA_ROPE = 64
MLA_QK = MLA_NOPE + MLA_ROPE
MLA_PAD = 256
Q_LORA = 512
KV_LORA = 256
NA_HEADS = 4
NA_WIN_H = 8
NA_WIN_W = 16
NA_TQ = 512
GQA_HEADS = 4
GQA_KV_HEADS = 2
EPS = 1e-6
NEG_INF = -1e30
LOG2E = 1.4426950408889634
MOD_ROWS = 16

LANES = 128
COL_CQ = 0
COL_CKV = COL_CQ + Q_LORA
COL_KR = COL_CKV + KV_LORA
COL_NA = COL_KR + LANES
COL_GQ = COL_NA + 3 * NA_HEADS * HEAD_DIM
COL_GK = COL_GQ + GQA_HEADS * HEAD_DIM
COL_GV = COL_GK + GQA_KV_HEADS * HEAD_DIM
PROJ_W = COL_GV + GQA_KV_HEADS * HEAD_DIM

VMEM_LIMIT = 62 * 1024 * 1024


def _params(*sem):
    return pltpu.CompilerParams(dimension_semantics=sem, vmem_limit_bytes=VMEM_LIMIT)


def _resident(shape, index_map):
    return pl.BlockSpec(shape, index_map, pipeline_mode=pl.Buffered(1))


def _as_bf16(ref):
    v = ref[...]
    return v if v.dtype == BF16 else v.astype(BF16)


def _rms(x, g):
    return x * lax.rsqrt(jnp.mean(x * x, axis=-1, keepdims=True) + EPS) * g


NORM_ROWS = 256


def _norm_modulate_store(x_ref, ng_ref, sc_ref, sh_ref, h_scr):
    gain = ng_ref[0] * (1.0 + sc_ref[0])
    shift = sh_ref[0]

    def body(i, carry):
        rows = pl.ds(pl.multiple_of(i * NORM_ROWS, NORM_ROWS), NORM_ROWS)
        x = x_ref[rows, :]
        r = lax.rsqrt(jnp.mean(x * x, axis=-1, keepdims=True) + EPS)
        h_scr[rows, :] = (x * r * gain + shift).astype(BF16)
        return carry

    lax.fori_loop(0, x_ref.shape[0] // NORM_ROWS, body, 0)


def _rope(x, cos, sin, quarter):
    n = x.shape[-1]
    lane = lax.broadcasted_iota(jnp.int32, x.shape, 1)
    first = (lane % (2 * quarter)) < quarter
    sw = jnp.where(first, pltpu.roll(x, n - quarter, 1), pltpu.roll(x, quarter, 1))
    return x * cos + sw * sin


def _mod_kernel(c_ref, w_ref, b_ref, o_ref):
    c = c_ref[...]
    s = (c * jax.nn.sigmoid(c)).astype(BF16)
    o_ref[0] = jnp.dot(s, w_ref[0].astype(BF16), preferred_element_type=F32) + b_ref[0]


def _modulation(cond, ada_w, ada_b):
    depth, d, n = ada_w.shape
    tn = 1024
    return pl.pallas_call(
        _mod_kernel,
        grid=(depth, n // tn),
        in_specs=[
            pl.BlockSpec((MOD_ROWS, d), lambda l, j: (0, 0)),
            pl.BlockSpec((1, d, tn), lambda l, j: (l, 0, j)),
            pl.BlockSpec((1, 1, tn), lambda l, j: (l, 0, j)),
        ],
        out_specs=pl.BlockSpec((1, MOD_ROWS, tn), lambda l, j: (l, 0, j)),
        out_shape=jax.ShapeDtypeStruct((depth, MOD_ROWS, n), F32),
        compiler_params=_params("parallel", "parallel"),
        name="modulation",
    )(cond, ada_w, ada_b.reshape(depth, 1, n))


def _mod_spec(l, chunk, stream, tm):
    base, tokens = stream
    return pl.BlockSpec((1, 1, D_MODEL), lambda i, *_: (l * MOD_ROWS + base + (i * tm) // tokens, 0, chunk))


def _ffn_kernel(x_ref, ng_ref, sh_ref, sc_ref, gt_ref, wg_ref, wu_ref, wd_ref, *rest, final):
    if final:
        fg_ref, o_ref, h_scr = rest
    else:
        o_ref, h_scr = rest
    f = pl.program_id(1)

    @pl.when(f == 0)
    def _():
        _norm_modulate_store(x_ref, ng_ref, sc_ref, sh_ref, h_scr)
        o_ref[...] = jnp.zeros_like(o_ref)

    h = h_scr[...]
    g = jnp.dot(h, _as_bf16(wg_ref), preferred_element_type=F32)
    u = jnp.dot(h, _as_bf16(wu_ref), preferred_element_type=F32)
    a = (g * jax.nn.sigmoid(g) * u).astype(BF16)
    o_ref[...] += jnp.dot(a, _as_bf16(wd_ref), preferred_element_type=F32)

    @pl.when(f == pl.num_programs(1) - 1)
    def _():
        gate = 0.5 * gt_ref[0]

        def body(i, carry):
            rows = pl.ds(pl.multiple_of(i * NORM_ROWS, NORM_ROWS), NORM_ROWS)
            y = x_ref[rows, :] + gate * o_ref[rows, :]
            if final:
                y = _rms(y, fg_ref[0])
            o_ref[rows, :] = y
            return carry

        lax.fori_loop(0, o_ref.shape[0] // NORM_ROWS, body, 0)


def _ffn(x, mods, norm_g, l, sub, k, stream, wg, wu, wd, final_g=None):
    t, d = x.shape
    tm, tf = 1024, 512
    chunk = 3 * sub
    in_specs = [
        pl.BlockSpec((tm, d), lambda i, j: (i, 0), pipeline_mode=pl.Buffered(1)),
        pl.BlockSpec((1, 1, d), lambda i, j: (l * 3 + sub, 0, 0)),
        _mod_spec(l, chunk, stream, tm),
        _mod_spec(l, chunk + 1, stream, tm),
        _mod_spec(l, chunk + 2, stream, tm),
        pl.BlockSpec((None, None, d, tf), lambda i, j: (l, k, 0, j)),
        pl.BlockSpec((None, None, d, tf), lambda i, j: (l, k, 0, j)),
        pl.BlockSpec((None, None, tf, d), lambda i, j: (l, k, j, 0)),
    ]
    args = [x, norm_g, mods, mods, mods, wg, wu, wd]
    if final_g is not None:
        in_specs.append(pl.BlockSpec((1, 1, d), lambda i, j: (0, 0, 0)))
        args.append(final_g)
    return pl.pallas_call(
        functools.partial(_ffn_kernel, final=final_g is not None),
        grid=(t // tm, D_FF // tf),
        in_specs=in_specs,
        out_specs=pl.BlockSpec((tm, d), lambda i, j: (i, 0)),
        out_shape=jax.ShapeDtypeStruct((t, d), F32),
        scratch_shapes=[pltpu.VMEM((tm, d), BF16)],
        compiler_params=_params("parallel", "arbitrary"),
        name="ffn",
    )(*args)


def _expand_kv(ckv_bf, kr, wk_ref, wv_ref, k_out, v_out):
    kn = jnp.dot(ckv_bf, wk_ref[...], preferred_element_type=F32)
    v_out[...] = jnp.dot(ckv_bf, wv_ref[...], preferred_element_type=F32).astype(BF16)
    kr_bf = kr.astype(BF16)
    for h in range(MLA_HEADS):
        k_out[:, h * MLA_PAD:h * MLA_PAD + MLA_NOPE] = kn[:, h * MLA_NOPE:(h + 1) * MLA_NOPE].astype(BF16)
        k_out[:, h * MLA_PAD + MLA_NOPE:(h + 1) * MLA_PAD] = kr_bf


N_QKV_OUT = 7
N_CACHE_OUT = 6


def _qkv_kernel(x_ref, ng_ref, sh_ref, sc_ref, w_ref, qn_ref, wqb_ref, kvn_ref, wk_ref, wv_ref, gqn_ref, gkn_ref,
                *rest, rope, n_prev, cache_out, layer):
    if rope:
        c64_ref, s64_ref, c128_ref, s128_ref = rest[:4]
        rest = rest[4:]
        rot64 = lambda v: _rope(v, c64_ref[...], s64_ref[...], MLA_ROPE // 4)
        rot128 = lambda v: _rope(v, c128_ref[...], s128_ref[...], HEAD_DIM // 4)
    else:
        rot64 = rot128 = lambda v: v
    rest = rest[n_prev:]
    q_out, k_out, v_out, gq_out, gk_out, gv_out, na_out = rest[:N_QKV_OUT]
    rest = rest[N_QKV_OUT:]
    if cache_out:
        cache_refs = rest[:N_CACHE_OUT]
        rest = rest[N_CACHE_OUT:]
        if n_prev == 0:
            for ref in cache_refs:
                for other in range(ref.shape[0]):
                    if other != layer:
                        ref[other] = jnp.zeros(ref.shape[1:], ref.dtype)
            cache_refs = [ref.at[layer] for ref in cache_refs]
        ckv_c, kr_c, nak_c, nav_c, gk_c, gv_c = cache_refs
    (h_scr,) = rest
    tm = x_ref.shape[0]

    _norm_modulate_store(x_ref, ng_ref, sc_ref, sh_ref, h_scr)
    h = h_scr[...]
    low = jnp.dot(h, w_ref[:, COL_CQ:COL_NA], preferred_element_type=F32)
    na = jnp.dot(h, w_ref[:, COL_NA:COL_GQ], preferred_element_type=F32)
    gg = jnp.dot(h, w_ref[:, COL_GQ:PROJ_W], preferred_element_type=F32)

    qn = _rms(low[:, COL_CQ:COL_CKV], qn_ref[0]).astype(BF16)
    q = jnp.dot(qn, wqb_ref[...], preferred_element_type=F32) * (MLA_QK ** -0.5 * LOG2E)
    for hh in range(MLA_HEADS):
        lo = hh * MLA_PAD
        q_out[:, lo:lo + MLA_NOPE] = q[:, lo:lo + MLA_NOPE].astype(BF16)
        q_out[:, lo + MLA_NOPE:lo + MLA_PAD] = rot64(q[:, lo + MLA_NOPE:lo + MLA_PAD]).astype(BF16)

    ckv = _rms(low[:, COL_CKV:COL_KR], kvn_ref[0])
    kr = rot64(low[:, COL_KR:COL_NA])
    _expand_kv(ckv.astype(BF16), kr, wk_ref, wv_ref, k_out, v_out)

    nw = NA_HEADS * HEAD_DIM
    gw = GQA_HEADS * HEAD_DIM
    kw = GQA_KV_HEADS * HEAD_DIM
    na_out[:, :nw] = (na[:, :nw] * (HEAD_DIM ** -0.5 * LOG2E)).astype(BF16)
    na_out[:, nw:] = na[:, nw:].astype(BF16)
    gq_gain = gqn_ref[0] * (HEAD_DIM ** -0.5 * LOG2E)
    for hh in range(GQA_HEADS):
        sl = slice(hh * HEAD_DIM, (hh + 1) * HEAD_DIM)
        gq_out[:, sl] = rot128(_rms(gg[:, sl], gq_gain)).astype(BF16)
    gv = gg[:, gw + kw:]
    gv_out[...] = gv.astype(BF16)
    for hh in range(GQA_KV_HEADS):
        sl = slice(hh * HEAD_DIM, (hh + 1) * HEAD_DIM)
        gk = rot128(_rms(gg[:, gw + hh * HEAD_DIM:gw + (hh + 1) * HEAD_DIM], gkn_ref[0]))
        gk_out[:, sl] = gk.astype(BF16)
        if cache_out:
            gk_c[pl.ds(hh, tm, stride=GQA_KV_HEADS), :] = gk
            gv_c[pl.ds(hh, tm, stride=GQA_KV_HEADS), :] = gv[:, sl]
    if cache_out:
        ckv_c[...] = ckv
        kr_c[...] = kr[:, :MLA_ROPE]
        for hh in range(NA_HEADS):
            sl = slice(hh * HEAD_DIM, (hh + 1) * HEAD_DIM)
            nak_c[pl.ds(hh, tm, stride=NA_HEADS), :] = na[:, nw + hh * HEAD_DIM:nw + (hh + 1) * HEAD_DIM]
            nav_c[pl.ds(hh, tm, stride=NA_HEADS), :] = na[:, 2 * nw + hh * HEAD_DIM:2 * nw + (hh + 1) * HEAD_DIM]


def _cache_shapes(nb, depth, seq):
    return [
        (nb, depth, seq, KV_LORA), (nb, depth, seq, MLA_ROPE),
        (nb, depth, seq * NA_HEADS, HEAD_DIM), (nb, depth, seq * NA_HEADS, HEAD_DIM),
        (nb, depth, seq * GQA_KV_HEADS, HEAD_DIM), (nb, depth, seq * GQA_KV_HEADS, HEAD_DIM),
    ]


def _qkv(x, mods, norm_g, l, stream, w_in, wts, tm, tables=None, cache=None):
    t, d = x.shape
    rope = tables is not None
    qw = MLA_HEADS * MLA_PAD
    vw = MLA_HEADS * HEAD_DIM
    in_specs = [
        pl.BlockSpec((tm, d), lambda i: (i, 0)),
        pl.BlockSpec((1, 1, d), lambda i: (l * 3 + 1, 0, 0)),
        _mod_spec(l, 3, stream, tm),
        _mod_spec(l, 4, stream, tm),
        _resident((None, d, PROJ_W), lambda i: (l, 0, 0)),
        pl.BlockSpec((1, 1, Q_LORA), lambda i: (l, 0, 0)),
        _resident((None, Q_LORA, qw), lambda i: (l, 0, 0)),
        pl.BlockSpec((1, 1, KV_LORA), lambda i: (l, 0, 0)),
        _resident((None, KV_LORA, vw), lambda i: (l, 0, 0)),
        _resident((None, KV_LORA, vw), lambda i: (l, 0, 0)),
        pl.BlockSpec((1, 1, HEAD_DIM), lambda i: (l, 0, 0)),
        pl.BlockSpec((1, 1, HEAD_DIM), lambda i: (l, 0, 0)),
    ]
    args = [x, norm_g, mods, mods, w_in, wts["q_norm"], wts["wqb"], wts["kv_norm"], wts["wk"], wts["wv"],
            wts["gq_norm"], wts["gk_norm"]]
    if rope:
        per = tables[0].shape[0] // tm
        in_specs += [pl.BlockSpec((tm, LANES), lambda i: (i % per, 0))] * 4
        args += list(tables)
    row = lambda w: pl.BlockSpec((tm, w), lambda i: (i, 0))
    widths = [qw, qw, vw, GQA_HEADS * HEAD_DIM, GQA_KV_HEADS * HEAD_DIM, GQA_KV_HEADS * HEAD_DIM,
              3 * NA_HEADS * HEAD_DIM]
    out_specs = [row(w) for w in widths]
    out_shape = [jax.ShapeDtypeStruct((t, w), BF16) for w in widths]
    aliases = {}
    n_prev = 0
    if cache is not None:
        nb, depth, prev = cache
        assert tm * nb == t
        shapes = _cache_shapes(nb, depth, tm)
        if prev is not None:
            n_prev = len(prev)
            aliases = {len(args) + k: N_QKV_OUT + k for k in range(n_prev)}
            in_specs += [pl.BlockSpec(memory_space=pl.ANY)] * n_prev
            args += list(prev)
            out_specs += [pl.BlockSpec((None, None) + s[2:], lambda i: (i, l, 0, 0)) for s in shapes]
        else:
            out_specs += [pl.BlockSpec((None,) + s[1:], lambda i: (i, 0, 0, 0)) for s in shapes]
        out_shape += [jax.ShapeDtypeStruct(s, F32) for s in shapes]
    return pl.pallas_call(
        functools.partial(_qkv_kernel, rope=rope, n_prev=n_prev, cache_out=cache is not None, layer=l),
        grid=(t // tm,),
        in_specs=in_specs,
        out_specs=out_specs,
        out_shape=out_shape,
        input_output_aliases=aliases,
        scratch_shapes=[pltpu.VMEM((tm, d), BF16)],
        compiler_params=_params("parallel"),
        name="qkv",
    )(*args)


def _cache_kv_kernel(ckv_ref, kr_ref, wk_ref, wv_ref, k_out, v_out):
    _expand_kv(ckv_ref[...].astype(BF16), kr_ref[...], wk_ref, wv_ref, k_out, v_out)


def _cache_kv(cache_ckv, cache_kr_pad, l, wts):
    nb, _, past, _ = cache_ckv.shape
    qw = MLA_HEADS * MLA_PAD
    vw = MLA_HEADS * HEAD_DIM
    return pl.pallas_call(
        _cache_kv_kernel,
        grid=(nb,),
        in_specs=[
            pl.BlockSpec((None, None, past, KV_LORA), lambda b: (b, l, 0, 0)),
            pl.BlockSpec((None, None, past, LANES), lambda b: (b, l, 0, 0)),
            pl.BlockSpec((None, KV_LORA, vw), lambda b: (l, 0, 0)),
            pl.BlockSpec((None, KV_LORA, vw), lambda b: (l, 0, 0)),
        ],
        out_specs=[pl.BlockSpec((past, qw), lambda b: (b, 0)), pl.BlockSpec((past, vw), lambda b: (b, 0))],
        out_shape=[jax.ShapeDtypeStruct((nb * past, qw), BF16), jax.ShapeDtypeStruct((nb * past, vw), BF16)],
        compiler_params=_params("parallel"),
        name="cache_kv",
    )(cache_ckv, cache_kr_pad, wts["wk"], wts["wv"])


def _attn_kernel(*refs, n_heads, group, dk, dv, seg_kinds, n_sub):
    q_ref = refs[0]
    pos = 1
    segs = []
    windows = [w for _, _, w in seg_kinds]
    for has_bias, inter, _ in seg_kinds:
        b_ref = refs[pos + 2] if has_bias else None
        segs.append((refs[pos], refs[pos + 1], b_ref, inter))
        pos += 3 if has_bias else 2
    o_ref = refs[pos]

    def head_rows(ref, rows, hk, width, inter):
        if inter:
            return ref[pl.ds(hk, ref.shape[0] // inter, stride=inter), :]
        return ref[rows, hk * width:(hk + 1) * width]

    def attend(q_rows, seg_rows):
        for h in range(n_heads):
            hk = h // group
            q = q_ref[q_rows, h * dk:(h + 1) * dk].astype(BF16)
            scores = []
            for (k_ref, _, b_ref, inter), rows in zip(segs, seg_rows):
                k = head_rows(k_ref, rows, hk, dk, inter).astype(BF16)
                s = lax.dot_general(q, k, (((1,), (1,)), ((), ())), preferred_element_type=F32)
                if b_ref is not None:
                    s = s + b_ref[h]
                scores.append(s)
            m = functools.reduce(jnp.maximum, [s.max(axis=-1, keepdims=True) for s in scores])
            o = None
            for s, (_, v_ref, _, inter), rows in zip(scores, segs, seg_rows):
                p = jnp.exp2(s - m).astype(BF16)
                v = head_rows(v_ref, rows, hk, dv, inter).astype(BF16)
                pv = jnp.dot(p, jnp.concatenate([v, jnp.ones_like(v)], axis=1), preferred_element_type=F32)
                o = pv if o is None else o + pv
            o_ref[q_rows, h * dv:(h + 1) * dv] = (o[:, :dv] / o[:, dv:]).astype(o_ref.dtype)

    if n_sub == 1:
        attend(slice(None), [slice(None) if w is None else
                             pl.ds(pl.multiple_of(pl.program_id(1) * w[0], w[0]), w[1]) for w in windows])
    else:
        def body(s, carry):
            rows = lambda ref: pl.ds(pl.multiple_of(s * (ref.shape[0] // n_sub), ref.shape[0] // n_sub),
                                     ref.shape[0] // n_sub)
            attend(rows(q_ref), [rows(seg[0]) for seg in segs])
            return carry

        lax.fori_loop(0, n_sub, body, 0)


def _attention(q, segs, *, t, grid, tq, q_index, n_heads, group, dk, dv, name, n_sub=1):
    assert n_sub == 1 or all(s[4] is None and not s[6] and len(s) == 7 for s in segs)
    in_specs = [pl.BlockSpec((tq, n_heads * dk), q_index)]
    args = [q]
    seg_kinds = []
    for k_arr, k_spec, v_arr, v_spec, b_arr, b_spec, inter, *window in segs:
        in_specs += [k_spec, v_spec]
        args += [k_arr, v_arr]
        seg_kinds.append((b_arr is not None, inter, window[0] if window else None))
        if b_arr is not None:
            in_specs.append(b_spec)
            args.append(b_arr)
    out_index = lambda *g: (q_index(*g)[0], 0)
    return pl.pallas_call(
        functools.partial(_attn_kernel, n_heads=n_heads, group=group, dk=dk, dv=dv,
                          seg_kinds=tuple(seg_kinds), n_sub=n_sub),
        grid=grid,
        in_specs=in_specs,
        out_specs=pl.BlockSpec((tq, n_heads * dv), out_index),
        out_shape=jax.ShapeDtypeStruct((t, n_heads * dv), BF16),
        compiler_params=_params(*(("parallel",) * len(grid))),
        name=name,
    )(*args)


def _context_attention(q_mla, k_mla, v_mla, gq, gk, gv, na, seq):
    t = na.shape[0]
    n_sub = 4
    rows = seq * n_sub
    grid = (t // rows,)
    blk = lambda w, c: pl.BlockSpec((rows, w), lambda b: (b, c))
    common = dict(t=t, grid=grid, tq=rows, n_sub=n_sub)
    nw = NA_HEADS * HEAD_DIM
    kw = GQA_KV_HEADS * HEAD_DIM
    o_a = _attention(q_mla, [(k_mla, blk(MLA_HEADS * MLA_PAD, 0), v_mla, blk(MLA_HEADS * HEAD_DIM, 0), None, None, 0)],
                     q_index=lambda b: (b, 0), n_heads=MLA_HEADS, group=1, dk=MLA_PAD, dv=HEAD_DIM,
                     name="ctx_mla", **common)
    o_b = _attention(na, [(na, blk(nw, 1), na, blk(nw, 2), None, None, 0)],
                     q_index=lambda b: (b, 0), n_heads=NA_HEADS, group=1, dk=HEAD_DIM, dv=HEAD_DIM,
                     name="ctx_na", **common)
    o_c = _attention(gq, [(gk, blk(kw, 0), gv, blk(kw, 0), None, None, 0)],
                     q_index=lambda b: (b, 0), n_heads=GQA_HEADS, group=GQA_HEADS // GQA_KV_HEADS,
                     dk=HEAD_DIM, dv=HEAD_DIM, name="ctx_gqa", **common)
    return o_a, o_b, o_c


def _latent_attention(q_mla, k_mla, v_mla, gq, gk, gv, na, kc_mla, vc_mla, caches, bias, l, seq):
    cache_na_k, cache_na_v, cache_gqa_k, cache_gqa_v = caches
    t = na.shape[0]
    past = kc_mla.shape[0] // (t // seq)
    own = lambda w, c: pl.BlockSpec((seq, w), lambda b, i: (b, c))
    flat = lambda w: pl.BlockSpec((past, w), lambda b, i: (b, 0))
    cached = lambda heads: pl.BlockSpec((None, None, past * heads, HEAD_DIM), lambda b, i: (b, l, 0, 0))
    qw, vw = MLA_HEADS * MLA_PAD, MLA_HEADS * HEAD_DIM
    nw = NA_HEADS * HEAD_DIM
    kw = GQA_KV_HEADS * HEAD_DIM
    o_c = _attention(gq, [(cache_gqa_k, cached(GQA_KV_HEADS), cache_gqa_v, cached(GQA_KV_HEADS), None, None,
                           GQA_KV_HEADS),
                          (gk, own(kw, 0), gv, own(kw, 0), None, None, 0)],
                     q_index=lambda b, i: (b, 0), n_heads=GQA_HEADS, group=GQA_HEADS // GQA_KV_HEADS,
                     dk=HEAD_DIM, dv=HEAD_DIM, name="lat_gqa",
                     t=t, grid=(t // seq, 1), tq=seq)
    tq = NA_TQ
    nq = seq // tq
    common = dict(t=t, grid=(t // seq, nq), tq=tq)
    o_a = _attention(q_mla, [(kc_mla, flat(qw), vc_mla, flat(vw), None, None, 0),
                             (k_mla, own(qw, 0), v_mla, own(vw, 0), None, None, 0)],
                     q_index=lambda b, i: (b * nq + i, 0), n_heads=MLA_HEADS, group=1, dk=MLA_PAD, dv=HEAD_DIM,
                     name="lat_mla", **common)
    win_step, win_size = _na_key_window(seq, tq)
    window = (win_step * GRID_W, win_size * GRID_W)
    bias_spec = pl.BlockSpec((NA_HEADS, tq, window[1]), lambda b, i: (0, i, 0))
    o_b = _attention(na, [(cache_na_k, cached(NA_HEADS), cache_na_v, cached(NA_HEADS), None, None, NA_HEADS),
                          (na, own(nw, 1), na, own(nw, 2), bias, bias_spec, 0, window)],
                     q_index=lambda b, i: (b * nq + i, 0), n_heads=NA_HEADS, group=1,
                     dk=HEAD_DIM, dv=HEAD_DIM, name="lat_na", **common)
    return o_a, o_b, o_c


def _outproj_kernel(x_ref, gt_ref, oa_ref, ob_ref, oc_ref, w_ref, o_ref):
    wa = oa_ref.shape[1]
    wb = ob_ref.shape[1]
    acc = jnp.dot(oa_ref[...], w_ref[0:wa, :], preferred_element_type=F32)
    acc += jnp.dot(ob_ref[...], w_ref[wa:wa + wb, :], preferred_element_type=F32)
    acc += jnp.dot(oc_ref[...], w_ref[wa + wb:, :], preferred_element_type=F32)
    o_ref[...] = x_ref[...] + gt_ref[0] * acc


def _outproj(x, mods, l, stream, o_a, o_b, o_c, w_out):
    t, d = x.shape
    tm = 512
    row = lambda a: pl.BlockSpec((tm, a.shape[1]), lambda i: (i, 0))
    return pl.pallas_call(
        _outproj_kernel,
        grid=(t // tm,),
        in_specs=[
            row(x),
            _mod_spec(l, 5, stream, tm),
            row(o_a), row(o_b), row(o_c),
            _resident((None, d, d), lambda i: (l, 0, 0)),
        ],
        out_specs=row(x),
        out_shape=jax.ShapeDtypeStruct((t, d), F32),
        compiler_params=_params("parallel"),
        name="outproj",
    )(x, mods, o_a, o_b, o_c, w_out)


def _project_layout_kernel(w_ref, o_ref):
    split = COL_KR + MLA_ROPE
    o_ref[:, :split] = w_ref[:, :split].astype(BF16)
    o_ref[:, split:COL_NA] = jnp.zeros((o_ref.shape[0], COL_NA - split), BF16)
    o_ref[:, COL_NA:] = w_ref[:, split:].astype(BF16)


def _project_layout(w_in):
    depth, d, n = w_in.shape
    tk = 256
    return pl.pallas_call(
        _project_layout_kernel,
        grid=(depth, d // tk),
        in_specs=[pl.BlockSpec((None, tk, n), lambda l, i: (l, i, 0))],
        out_specs=pl.BlockSpec((None, tk, PROJ_W), lambda l, i: (l, i, 0)),
        out_shape=jax.ShapeDtypeStruct((depth, d, PROJ_W), BF16),
        compiler_params=_params("parallel", "parallel"),
        name="project_layout",
    )(w_in)


def _rope_tables(seq, d):
    quarter = d // 4
    tt = np.arange(seq)
    pos = np.stack([tt // GRID_W, tt % GRID_W], axis=-1).astype(np.float64)
    inv = ROPE_THETA ** (-np.arange(quarter, dtype=np.float64) / quarter)
    ang = pos[:, :, None] * inv
    cos = np.cos(ang)
    sin = np.sin(ang)
    cos_t = np.stack([cos, cos], axis=2).reshape(seq, d)
    sin_t = np.stack([-sin, sin], axis=2).reshape(seq, d)
    pad = LANES - d
    if pad:
        cos_t = np.concatenate([cos_t, np.ones((seq, pad))], axis=-1)
        sin_t = np.concatenate([sin_t, np.zeros((seq, pad))], axis=-1)
    return jnp.asarray(cos_t, F32), jnp.asarray(sin_t, F32)


def _na_key_window(seq, tq):
    rows = seq // GRID_W
    wh = min(NA_WIN_H, rows)
    qr = tq // GRID_W
    first = lambda r: min(max(r - wh // 2, 0), rows - wh)
    spans = [(first(i * qr), first(i * qr + qr - 1) + wh) for i in range(rows // qr)]
    size = max(hi - lo for lo, hi in spans)
    size += size % 2
    starts = [min(lo, rows - size) for lo, _ in spans]
    step = starts[1] - starts[0] if len(starts) > 1 else 0
    assert all(s == i * step and s <= lo and hi <= s + size for i, (s, (lo, hi)) in enumerate(zip(starts, spans)))
    return step, size


def _na_bias_kernel(t_ref, o_ref, *, rows, wh, qr, win_step, win_size):
    r = pl.program_id(1)
    k0 = (r // qr) * win_step
    shape = (GRID_W, 2 * GRID_W)
    c = lax.broadcasted_iota(jnp.int32, shape, 0)
    lane = lax.broadcasted_iota(jnp.int32, shape, 1)
    kc = lane % GRID_W
    rs = jnp.clip(r - wh // 2, 0, rows - wh)
    cs = jnp.clip(c - NA_WIN_W // 2, 0, GRID_W - NA_WIN_W)
    col_ok = (kc >= cs) & (kc < cs + NA_WIN_W)
    for j in range(win_size // 2):
        kr = k0 + 2 * j + lane // GRID_W
        ok = col_ok & (kr >= rs) & (kr < rs + wh)
        d = jnp.clip(k0 + 2 * j - r + NA_WIN_H, 0, 2 * NA_WIN_H - 1)
        pair = jnp.broadcast_to(t_ref[0, pl.ds(d, 1), :], shape)
        toeplitz = pltpu.roll(pair, 2 * GRID_W - (NA_WIN_W - 1), 1, stride=1, stride_axis=0)
        o_ref[0, :, j * 2 * GRID_W:(j + 1) * 2 * GRID_W] = jnp.where(ok, toeplitz * LOG2E, NEG_INF)


def _na_bias(rpb, seq, tq):
    nh, nr, nc = rpb.shape
    rows = seq // GRID_W
    wh = min(NA_WIN_H, rows)
    win_step, win_size = _na_key_window(seq, tq)
    padded = jnp.pad(rpb.astype(F32), ((0, 0), (1, 1), (0, GRID_W - nc)))
    table = jnp.concatenate([padded[:, :nr + 1], padded[:, 1:]], axis=-1)
    return pl.pallas_call(
        functools.partial(_na_bias_kernel, rows=rows, wh=wh, qr=tq // GRID_W, win_step=win_step,
                          win_size=win_size),
        grid=(nh, rows),
        in_specs=[pl.BlockSpec((1, nr + 1, 2 * GRID_W), lambda h, r: (h, 0, 0))],
        out_specs=pl.BlockSpec((1, GRID_W, win_size * GRID_W), lambda h, r: (h, r, 0)),
        out_shape=jax.ShapeDtypeStruct((nh, seq, win_size * GRID_W), F32),
        compiler_params=_params("parallel", "parallel"),
        name="na_bias",
    )(table)


def kernel(x_prompt, x_sample, cache_mla_ckv, cache_mla_krope, cache_na_k, cache_na_v, cache_gqa_k, cache_gqa_v, c, c_ctx, ada_w, ada_b, norm_g, ffn_wg, ffn_wu, ffn_wd, w_in, mla_q_norm, mla_wqb, mla_kv_norm, mla_wkvb, na_rpb, gqa_q_norm, gqa_k_norm, w_out, final_norm):
    depth = ada_w.shape[0]
    nb_ctx, seq_ctx, d = x_prompt.shape
    nb_lat, seq_lat, _ = x_sample.shape
    past = cache_mla_ckv.shape[2]

    wg = ffn_wg
    wu = ffn_wu
    wd = ffn_wd
    w_in_p = _project_layout(w_in)
    wqb =jnp.pad(mla_wqb.reshape(depth, Q_LORA, MLA_HEADS, MLA_QK),
                  ((0, 0), (0, 0), (0, 0), (0, MLA_PAD - MLA_QK))).reshape(depth, Q_LORA, MLA_HEADS * MLA_PAD)
    wkv = mla_wkvb.reshape(depth, KV_LORA, MLA_HEADS, 2 * HEAD_DIM)
    wts = {
        "q_norm": mla_q_norm.reshape(depth, 1, Q_LORA),
        "wqb": wqb.astype(BF16),
        "kv_norm": mla_kv_norm.reshape(depth, 1, KV_LORA),
        "wk": wkv[..., :MLA_NOPE].reshape(depth, KV_LORA, MLA_HEADS * MLA_NOPE).astype(BF16),
        "wv": wkv[..., MLA_NOPE:].reshape(depth, KV_LORA, MLA_HEADS * HEAD_DIM).astype(BF16),
        "gq_norm": gqa_q_norm.reshape(depth, 1, HEAD_DIM),
        "gk_norm": gqa_k_norm.reshape(depth, 1, HEAD_DIM),
    }
    w_out_b = w_out.astype(BF16)
    norm_g3 = norm_g.reshape(depth * 3, 1, d)
    final_g = final_norm.reshape(1, 1, d)
    tables = _rope_tables(seq_lat, MLA_ROPE) + _rope_tables(seq_lat, HEAD_DIM)
    kr_cache = jnp.pad(cache_mla_krope, ((0, 0), (0, 0), (0, 0), (0, LANES - MLA_ROPE)))
    caches = (cache_na_k.reshape(nb_lat, depth, past * NA_HEADS, HEAD_DIM),
              cache_na_v.reshape(nb_lat, depth, past * NA_HEADS, HEAD_DIM),
              cache_gqa_k.reshape(nb_lat, depth, past * GQA_KV_HEADS, HEAD_DIM),
              cache_gqa_v.reshape(nb_lat, depth, past * GQA_KV_HEADS, HEAD_DIM))

    cond = jnp.concatenate([c_ctx[None, :], c, jnp.zeros((MOD_ROWS - 1 - nb_lat, d), c.dtype)], axis=0)
    mods = _modulation(cond, ada_w, ada_b).reshape(depth * MOD_ROWS, 1, N_MOD * d)

    xp = x_prompt.reshape(nb_ctx * seq_ctx, d)
    xs = x_sample.reshape(nb_lat * seq_lat, d)
    ctx_row = (0, nb_ctx * seq_ctx)
    lat_row = (1, seq_lat)
    biases = [_na_bias(na_rpb[l], seq_lat, NA_TQ) for l in range(depth)]

    new_cache = None
    for l in range(depth):
        last = l == depth - 1
        xp = _ffn(xp, mods, norm_g3, l, 0, 0, ctx_row, wg, wu, wd)
        outs = _qkv(xp, mods, norm_g3, l, ctx_row, w_in_p, wts, seq_ctx, cache=(nb_ctx, depth, new_cache))
        q_mla, k_mla, v_mla, gq, gk, gv, na = outs[:N_QKV_OUT]
        new_cache = outs[N_QKV_OUT:]
        o_a, o_b, o_c = _context_attention(q_mla, k_mla, v_mla, gq, gk, gv, na, seq_ctx)
        xp = _outproj(xp, mods, l, ctx_row, o_a, o_b, o_c, w_out_b)
        xp = _ffn(xp, mods, norm_g3, l, 2, 1, ctx_row, wg, wu, wd, final_g if last else None)
        xs = _ffn(xs, mods, norm_g3, l, 0, 0, lat_row, wg, wu, wd)
        q_mla, k_mla, v_mla, gq, gk, gv, na = _qkv(xs, mods, norm_g3, l, lat_row, w_in_p, wts, 512, tables=tables)
        kc_mla, vc_mla = _cache_kv(cache_mla_ckv, kr_cache, l, wts)
        o_a, o_b, o_c = _latent_attention(q_mla, k_mla, v_mla, gq, gk, gv, na, kc_mla, vc_mla, caches, biases[l], l,
                                          seq_lat)
        xs = _outproj(xs, mods, l, lat_row, o_a, o_b, o_c, w_out_b)
        xs = _ffn(xs, mods, norm_g3, l, 2, 1, lat_row, wg, wu, wd, final_g if last else None)

    y_prompt = xp.reshape(nb_ctx, seq_ctx, d)
    y_sample = xs.reshape(nb_lat, seq_lat, d)
    ckv_c, kr_c, nak_c, nav_c, gk_c, gv_c = new_cache
    na_shape = (nb_ctx, depth, seq_ctx, NA_HEADS, HEAD_DIM)
    gqa_shape = (nb_ctx, depth, seq_ctx, GQA_KV_HEADS, HEAD_DIM)
    return (y_prompt, y_sample, ckv_c, kr_c, nak_c.reshape(na_shape), nav_c.reshape(na_shape),
            gk_c.reshape(gqa_shape), gv_c.reshape(gqa_shape))
```

```python
import functools

import jax
import jax.numpy as jnp
import numpy as np
from jax import lax
from jax.experimental import pallas as pl
from jax.experimental.pallas import tpu as pltpu

F32 = jnp.float32
BF16 = jnp.bfloat16

D_MODEL = 2048
D_FF = 5632
N_MOD = 9
GRID_W = 64
ROPE_THETA = 10000.0
HEAD_DIM = 128
MLA_HEADS = 8
MLA_NOPE = 128
MLA_ROPE = 64
MLA_QK = MLA_NOPE + MLA_ROPE
MLA_PAD = 256
Q_LORA = 512
KV_LORA = 256
NA_HEADS = 4
NA_WIN_H = 8
NA_WIN_W = 16
NA_TQ = 512
GQA_HEADS = 4
GQA_KV_HEADS = 2
EPS = 1e-6
NEG_INF = -1e30
LOG2E = 1.4426950408889634
MOD_ROWS = 16

LANES = 128
COL_CQ = 0
COL_CKV = COL_CQ + Q_LORA
COL_KR = COL_CKV + KV_LORA
COL_NA = COL_KR + LANES
COL_GQ = COL_NA + 3 * NA_HEADS * HEAD_DIM
COL_GK = COL_GQ + GQA_HEADS * HEAD_DIM
COL_GV = COL_GK + GQA_KV_HEADS * HEAD_DIM
PROJ_W = COL_GV + GQA_KV_HEADS * HEAD_DIM

VMEM_LIMIT = 62 * 1024 * 1024


def _params(*sem):
    return pltpu.CompilerParams(dimension_semantics=sem, vmem_limit_bytes=VMEM_LIMIT)


def _resident(shape, index_map):
    return pl.BlockSpec(shape, index_map, pipeline_mode=pl.Buffered(1))


def _as_bf16(ref):
    v = ref[...]
    return v if v.dtype == BF16 else v.astype(BF16)


def _rms(x, g):
    return x * lax.rsqrt(jnp.mean(x * x, axis=-1, keepdims=True) + EPS) * g


NORM_ROWS = 256


def _norm_modulate_store(x_ref, ng_ref, sc_ref, sh_ref, h_scr):
    gain = ng_ref[0] * (1.0 + sc_ref[0])
    shift = sh_ref[0]

    def body(i, carry):
        rows = pl.ds(pl.multiple_of(i * NORM_ROWS, NORM_ROWS), NORM_ROWS)
        x = x_ref[rows, :]
        r = lax.rsqrt(jnp.mean(x * x, axis=-1, keepdims=True) + EPS)
        h_scr[rows, :] = (x * r * gain + shift).astype(BF16)
        return carry

    lax.fori_loop(0, x_ref.shape[0] // NORM_ROWS, body, 0)


def _rope(x, cos, sin, quarter):
    n = x.shape[-1]
    lane = lax.broadcasted_iota(jnp.int32, x.shape, 1)
    first = (lane % (2 * quarter)) < quarter
    sw = jnp.where(first, pltpu.roll(x, n - quarter, 1), pltpu.roll(x, quarter, 1))
    return x * cos + sw * sin


def _mod_kernel(c_ref, w_ref, b_ref, o_ref):
    c = c_ref[...]
    s = (c * jax.nn.sigmoid(c)).astype(BF16)
    o_ref[0] = jnp.dot(s, w_ref[0].astype(BF16), preferred_element_type=F32) + b_ref[0]


def _modulation(cond, ada_w, ada_b):
    depth, d, n = ada_w.shape
    tn = 1024
    return pl.pallas_call(
        _mod_kernel,
        grid=(depth, n // tn),
        in_specs=[
            pl.BlockSpec((MOD_ROWS, d), lambda l, j: (0, 0)),
            pl.BlockSpec((1, d, tn), lambda l, j: (l, 0, j)),
            pl.BlockSpec((1, 1, tn), lambda l, j: (l, 0, j)),
        ],
        out_specs=pl.BlockSpec((1, MOD_ROWS, tn), lambda l, j: (l, 0, j)),
        out_shape=jax.ShapeDtypeStruct((depth, MOD_ROWS, n), F32),
        compiler_params=_params("parallel", "parallel"),
        name="modulation",
    )(cond, ada_w, ada_b.reshape(depth, 1, n))


def _mod_spec(l, chunk, stream, tm):
    base, tokens = stream
    return pl.BlockSpec((1, 1, D_MODEL), lambda i, *_: (l * MOD_ROWS + base + (i * tm) // tokens, 0, chunk))


def _ffn_kernel(x_ref, ng_ref, sh_ref, sc_ref, gt_ref, wg_ref, wu_ref, wd_ref, *rest, final):
    if final:
        fg_ref, o_ref, h_scr = rest
    else:
        o_ref, h_scr = rest
    f = pl.program_id(1)

    @pl.when(f == 0)
    def _():
        _norm_modulate_store(x_ref, ng_ref, sc_ref, sh_ref, h_scr)
        o_ref[...] = jnp.zeros_like(o_ref)

    h = h_scr[...]
    g = jnp.dot(h, _as_bf16(wg_ref), preferred_element_type=F32)
    u = jnp.dot(h, _as_bf16(wu_ref), preferred_element_type=F32)
    a = (g * jax.nn.sigmoid(g) * u).astype(BF16)
    o_ref[...] += jnp.dot(a, _as_bf16(wd_ref), preferred_element_type=F32)

    @pl.when(f == pl.num_programs(1) - 1)
    def _():
        gate = 0.5 * gt_ref[0]

        def body(i, carry):
            rows = pl.ds(pl.multiple_of(i * NORM_ROWS, NORM_ROWS), NORM_ROWS)
            y = x_ref[rows, :] + gate * o_ref[rows, :]
            if final:
                y = _rms(y, fg_ref[0])
            o_ref[rows, :] = y
            return carry

        lax.fori_loop(0, o_ref.shape[0] // NORM_ROWS, body, 0)


def _ffn(x, mods, norm_g, l, sub, k, stream, wg, wu, wd, final_g=None):
    t, d = x.shape
    tm, tf = 1024, 512
    chunk = 3 * sub
    in_specs = [
        pl.BlockSpec((tm, d), lambda i, j: (i, 0), pipeline_mode=pl.Buffered(1)),
        pl.BlockSpec((1, 1, d), lambda i, j: (l * 3 + sub, 0, 0)),
        _mod_spec(l, chunk, stream, tm),
        _mod_spec(l, chunk + 1, stream, tm),
        _mod_spec(l, chunk + 2, stream, tm),
        pl.BlockSpec((None, None, d, tf), lambda i, j: (l, k, 0, j)),
        pl.BlockSpec((None, None, d, tf), lambda i, j: (l, k, 0, j)),
        pl.BlockSpec((None, None, tf, d), lambda i, j: (l, k, j, 0)),
    ]
    args = [x, norm_g, mods, mods, mods, wg, wu, wd]
    if final_g is not None:
        in_specs.append(pl.BlockSpec((1, 1, d), lambda i, j: (0, 0, 0)))
        args.append(final_g)
    return pl.pallas_call(
        functools.partial(_ffn_kernel, final=final_g is not None),
        grid=(t // tm, D_FF // tf),
        in_specs=in_specs,
        out_specs=pl.BlockSpec((tm, d), lambda i, j: (i, 0)),
        out_shape=jax.ShapeDtypeStruct((t, d), F32),
        scratch_shapes=[pltpu.VMEM((tm, d), BF16)],
        compiler_params=_params("parallel", "arbitrary"),
        name="ffn",
    )(*args)


def _expand_kv(ckv_bf, kr, wk_ref, wv_ref, k_out, v_out):
    kn = jnp.dot(ckv_bf, wk_ref[...], preferred_element_type=F32)
    v_out[...] = jnp.dot(ckv_bf, wv_ref[...], preferred_element_type=F32).astype(BF16)
    kr_bf = kr.astype(BF16)
    for h in range(MLA_HEADS):
        k_out[:, h * MLA_PAD:h * MLA_PAD + MLA_NOPE] = kn[:, h * MLA_NOPE:(h + 1) * MLA_NOPE].astype(BF16)
        k_out[:, h * MLA_PAD + MLA_NOPE:(h + 1) * MLA_PAD] = kr_bf


N_QKV_OUT = 7
N_CACHE_OUT = 6


def _qkv_kernel(x_ref, ng_ref, sh_ref, sc_ref, w_ref, qn_ref, wqb_ref, kvn_ref, wk_ref, wv_ref, gqn_ref, gkn_ref,
                *rest, rope, n_prev, cache_out, layer):
    if rope:
        c64_ref, s64_ref, c128_ref, s128_ref = rest[:4]
        rest = rest[4:]
        rot64 = lambda v: _rope(v, c64_ref[...], s64_ref[...], MLA_ROPE // 4)
        rot128 = lambda v: _rope(v, c128_ref[...], s128_ref[...], HEAD_DIM // 4)
    else:
        rot64 = rot128 = lambda v: v
    rest = rest[n_prev:]
    q_out, k_out, v_out, gq_out, gk_out, gv_out, na_out = rest[:N_QKV_OUT]
    rest = rest[N_QKV_OUT:]
    if cache_out:
        cache_refs = rest[:N_CACHE_OUT]
        rest = rest[N_CACHE_OUT:]
        if n_prev == 0:
            for ref in cache_refs:
                for other in range(ref.shape[0]):
                    if other != layer:
                        ref[other] = jnp.zeros(ref.shape[1:], ref.dtype)
            cache_refs = [ref.at[layer] for ref in cache_refs]
        ckv_c, kr_c, nak_c, nav_c, gk_c, gv_c = cache_refs
    (h_scr,) = rest
    tm = x_ref.shape[0]

    _norm_modulate_store(x_ref, ng_ref, sc_ref, sh_ref, h_scr)
    h = h_scr[...]
    low = jnp.dot(h, w_ref[:, COL_CQ:COL_NA], preferred_element_type=F32)
    na = jnp.dot(h, w_ref[:, COL_NA:COL_GQ], preferred_element_type=F32)
    gg = jnp.dot(h, w_ref[:, COL_GQ:PROJ_W], preferred_element_type=F32)

    qn = _rms(low[:, COL_CQ:COL_CKV], qn_ref[0]).astype(BF16)
    q = jnp.dot(qn, wqb_ref[...], preferred_element_type=F32) * (MLA_QK ** -0.5 * LOG2E)
    for hh in range(MLA_HEADS):
        lo = hh * MLA_PAD
        q_out[:, lo:lo + MLA_NOPE] = q[:, lo:lo + MLA_NOPE].astype(BF16)
        q_out[:, lo + MLA_NOPE:lo + MLA_PAD] = rot64(q[:, lo + MLA_NOPE:lo + MLA_PAD]).astype(BF16)

    ckv = _rms(low[:, COL_CKV:COL_KR], kvn_ref[0])
    kr = rot64(low[:, COL_KR:COL_NA])
    _expand_kv(ckv.astype(BF16), kr, wk_ref, wv_ref, k_out, v_out)

    nw = NA_HEADS * HEAD_DIM
    gw = GQA_HEADS * HEAD_DIM
    kw = GQA_KV_HEADS * HEAD_DIM
    na_out[:, :nw] = (na[:, :nw] * (HEAD_DIM ** -0.5 * LOG2E)).astype(BF16)
    na_out[:, nw:] = na[:, nw:].astype(BF16)
    gq_gain = gqn_ref[0] * (HEAD_DIM ** -0.5 * LOG2E)
    for hh in range(GQA_HEADS):
        sl = slice(hh * HEAD_DIM, (hh + 1) * HEAD_DIM)
        gq_out[:, sl] = rot128(_rms(gg[:, sl], gq_gain)).astype(BF16)
    gv = gg[:, gw + kw:]
    gv_out[...] = gv.astype(BF16)
    for hh in range(GQA_KV_HEADS):
        sl = slice(hh * HEAD_DIM, (hh + 1) * HEAD_DIM)
        gk = rot128(_rms(gg[:, gw + hh * HEAD_DIM:gw + (hh + 1) * HEAD_DIM], gkn_ref[0]))
        gk_out[:, sl] = gk.astype(BF16)
        if cache_out:
            gk_c[pl.ds(hh, tm, stride=GQA_KV_HEADS), :] = gk
            gv_c[pl.ds(hh, tm, stride=GQA_KV_HEADS), :] = gv[:, sl]
    if cache_out:
        ckv_c[...] = ckv
        kr_c[...] = kr[:, :MLA_ROPE]
        for hh in range(NA_HEADS):
            sl = slice(hh * HEAD_DIM, (hh + 1) * HEAD_DIM)
            nak_c[pl.ds(hh, tm, stride=NA_HEADS), :] = na[:, nw + hh * HEAD_DIM:nw + (hh + 1) * HEAD_DIM]
            nav_c[pl.ds(hh, tm, stride=NA_HEADS), :] = na[:, 2 * nw + hh * HEAD_DIM:2 * nw + (hh + 1) * HEAD_DIM]


def _cache_shapes(nb, depth, seq):
    return [
        (nb, depth, seq, KV_LORA), (nb, depth, seq, MLA_ROPE),
        (nb, depth, seq * NA_HEADS, HEAD_DIM), (nb, depth, seq * NA_HEADS, HEAD_DIM),
        (nb, depth, seq * GQA_KV_HEADS, HEAD_DIM), (nb, depth, seq * GQA_KV_HEADS, HEAD_DIM),
    ]


def _qkv(x, mods, norm_g, l, stream, w_in, wts, tm, tables=None, cache=None):
    t, d = x.shape
    rope = tables is not None
    qw = MLA_HEADS * MLA_PAD
    vw = MLA_HEADS * HEAD_DIM
    in_specs = [
        pl.BlockSpec((tm, d), lambda i: (i, 0)),
        pl.BlockSpec((1, 1, d), lambda i: (l * 3 + 1, 0, 0)),
        _mod_spec(l, 3, stream, tm),
        _mod_spec(l, 4, stream, tm),
        _resident((None, d, PROJ_W), lambda i: (l, 0, 0)),
        pl.BlockSpec((1, 1, Q_LORA), lambda i: (l, 0, 0)),
        _resident((None, Q_LORA, qw), lambda i: (l, 0, 0)),
        pl.BlockSpec((1, 1, KV_LORA), lambda i: (l, 0, 0)),
        _resident((None, KV_LORA, vw), lambda i: (l, 0, 0)),
        _resident((None, KV_LORA, vw), lambda i: (l, 0, 0)),
        pl.BlockSpec((1, 1, HEAD_DIM), lambda i: (l, 0, 0)),
        pl.BlockSpec((1, 1, HEAD_DIM), lambda i: (l, 0, 0)),
    ]
    args = [x, norm_g, mods, mods, w_in, wts["q_norm"], wts["wqb"], wts["kv_norm"], wts["wk"], wts["wv"],
            wts["gq_norm"], wts["gk_norm"]]
    if rope:
        per = tables[0].shape[0] // tm
        in_specs += [pl.BlockSpec((tm, LANES), lambda i: (i % per, 0))] * 4
        args += list(tables)
    row = lambda w: pl.BlockSpec((tm, w), lambda i: (i, 0))
    widths = [qw, qw, vw, GQA_HEADS * HEAD_DIM, GQA_KV_HEADS * HEAD_DIM, GQA_KV_HEADS * HEAD_DIM,
              3 * NA_HEADS * HEAD_DIM]
    out_specs = [row(w) for w in widths]
    out_shape = [jax.ShapeDtypeStruct((t, w), BF16) for w in widths]
    aliases = {}
    n_prev = 0
    if cache is not None:
        nb, depth, prev = cache
        assert tm * nb == t
        shapes = _cache_shapes(nb, depth, tm)
        if prev is not None:
            n_prev = len(prev)
            aliases = {len(args) + k: N_QKV_OUT + k for k in range(n_prev)}
            in_specs += [pl.BlockSpec(memory_space=pl.ANY)] * n_prev
            args += list(prev)
            out_specs += [pl.BlockSpec((None, None) + s[2:], lambda i: (i, l, 0, 0)) for s in shapes]
        else:
            out_specs += [pl.BlockSpec((None,) + s[1:], lambda i: (i, 0, 0, 0)) for s in shapes]
        out_shape += [jax.ShapeDtypeStruct(s, F32) for s in shapes]
    return pl.pallas_call(
        functools.partial(_qkv_kernel, rope=rope, n_prev=n_prev, cache_out=cache is not None, layer=l),
        grid=(t // tm,),
        in_specs=in_specs,
        out_specs=out_specs,
        out_shape=out_shape,
        input_output_aliases=aliases,
        scratch_shapes=[pltpu.VMEM((tm, d), BF16)],
        compiler_params=_params("parallel"),
        name="qkv",
    )(*args)


def _cache_kv_kernel(ckv_ref, kr_ref, wk_ref, wv_ref, k_out, v_out):
    _expand_kv(ckv_ref[...].astype(BF16), kr_ref[...], wk_ref, wv_ref, k_out, v_out)


def _cache_kv(cache_ckv, cache_kr_pad, l, wts):
    nb, _, past, _ = cache_ckv.shape
    qw = MLA_HEADS * MLA_PAD
    vw = MLA_HEADS * HEAD_DIM
    return pl.pallas_call(
        _cache_kv_kernel,
        grid=(nb,),
        in_specs=[
            pl.BlockSpec((None, None, past, KV_LORA), lambda b: (b, l, 0, 0)),
            pl.BlockSpec((None, None, past, LANES), lambda b: (b, l, 0, 0)),
            pl.BlockSpec((None, KV_LORA, vw), lambda b: (l, 0, 0)),
            pl.BlockSpec((None, KV_LORA, vw), lambda b: (l, 0, 0)),
        ],
        out_specs=[pl.BlockSpec((past, qw), lambda b: (b, 0)), pl.BlockSpec((past, vw), lambda b: (b, 0))],
        out_shape=[jax.ShapeDtypeStruct((nb * past, qw), BF16), jax.ShapeDtypeStruct((nb * past, vw), BF16)],
        compiler_params=_params("parallel"),
        name="cache_kv",
    )(cache_ckv, cache_kr_pad, wts["wk"], wts["wv"])


def _attn_kernel(*refs, n_heads, group, dk, dv, seg_kinds, n_sub):
    q_ref = refs[0]
    pos = 1
    segs = []
    windows = [w for _, _, w in seg_kinds]
    for has_bias, inter, _ in seg_kinds:
        b_ref = refs[pos + 2] if has_bias else None
        segs.append((refs[pos], refs[pos + 1], b_ref, inter))
        pos += 3 if has_bias else 2
    o_ref = refs[pos]

    def head_rows(ref, rows, hk, width, inter):
        if inter:
            return ref[pl.ds(hk, ref.shape[0] // inter, stride=inter), :]
        return ref[rows, hk * width:(hk + 1) * width]

    def attend(q_rows, seg_rows):
        for h in range(n_heads):
            hk = h // group
            q = q_ref[q_rows, h * dk:(h + 1) * dk].astype(BF16)
            scores = []
            for (k_ref, _, b_ref, inter), rows in zip(segs, seg_rows):
                k = head_rows(k_ref, rows, hk, dk, inter).astype(BF16)
                s = lax.dot_general(q, k, (((1,), (1,)), ((), ())), preferred_element_type=F32)
                if b_ref is not None:
                    s = s + b_ref[h]
                scores.append(s)
            m = functools.reduce(jnp.maximum, [s.max(axis=-1, keepdims=True) for s in scores])
            o = None
            for s, (_, v_ref, _, inter), rows in zip(scores, segs, seg_rows):
                p = jnp.exp2(s - m).astype(BF16)
                v = head_rows(v_ref, rows, hk, dv, inter).astype(BF16)
                pv = jnp.dot(p, jnp.concatenate([v, jnp.ones_like(v)], axis=1), preferred_element_type=F32)
                o = pv if o is None else o + pv
            o_ref[q_rows, h * dv:(h + 1) * dv] = (o[:, :dv] / o[:, dv:]).astype(o_ref.dtype)

    if n_sub == 1:
        attend(slice(None), [slice(None) if w is None else
                             pl.ds(pl.multiple_of(pl.program_id(1) * w[0], w[0]), w[1]) for w in windows])
    else:
        def body(s, carry):
            rows = lambda ref: pl.ds(pl.multiple_of(s * (ref.shape[0] // n_sub), ref.shape[0] // n_sub),
                                     ref.shape[0] // n_sub)
            attend(rows(q_ref), [rows(seg[0]) for seg in segs])
            return carry

        lax.fori_loop(0, n_sub, body, 0)


def _attention(q, segs, *, t, grid, tq, q_index, n_heads, group, dk, dv, name, n_sub=1):
    assert n_sub == 1 or all(s[4] is None and not s[6] and len(s) == 7 for s in segs)
    in_specs = [pl.BlockSpec((tq, n_heads * dk), q_index)]
    args = [q]
    seg_kinds = []
    for k_arr, k_spec, v_arr, v_spec, b_arr, b_spec, inter, *window in segs:
        in_specs += [k_spec, v_spec]
        args += [k_arr, v_arr]
        seg_kinds.append((b_arr is not None, inter, window[0] if window else None))
        if b_arr is not None:
            in_specs.append(b_spec)
            args.append(b_arr)
    out_index = lambda *g: (q_index(*g)[0], 0)
    return pl.pallas_call(
        functools.partial(_attn_kernel, n_heads=n_heads, group=group, dk=dk, dv=dv,
                          seg_kinds=tuple(seg_kinds), n_sub=n_sub),
        grid=grid,
        in_specs=in_specs,
        out_specs=pl.BlockSpec((tq, n_heads * dv), out_index),
        out_shape=jax.ShapeDtypeStruct((t, n_heads * dv), BF16),
        compiler_params=_params(*(("parallel",) * len(grid))),
        name=name,
    )(*args)


def _context_attention(q_mla, k_mla, v_mla, gq, gk, gv, na, seq):
    t = na.shape[0]
    n_sub = 4
    rows = seq * n_sub
    grid = (t // rows,)
    blk = lambda w, c: pl.BlockSpec((rows, w), lambda b: (b, c))
    common = dict(t=t, grid=grid, tq=rows, n_sub=n_sub)
    nw = NA_HEADS * HEAD_DIM
    kw = GQA_KV_HEADS * HEAD_DIM
    o_a = _attention(q_mla, [(k_mla, blk(MLA_HEADS * MLA_PAD, 0), v_mla, blk(MLA_HEADS * HEAD_DIM, 0), None, None, 0)],
                     q_index=lambda b: (b, 0), n_heads=MLA_HEADS, group=1, dk=MLA_PAD, dv=HEAD_DIM,
                     name="ctx_mla", **common)
    o_b = _attention(na, [(na, blk(nw, 1), na, blk(nw, 2), None, None, 0)],
                     q_index=lambda b: (b, 0), n_heads=NA_HEADS, group=1, dk=HEAD_DIM, dv=HEAD_DIM,
                     name="ctx_na", **common)
    o_c = _attention(gq, [(gk, blk(kw, 0), gv, blk(kw, 0), None, None, 0)],
                     q_index=lambda b: (b, 0), n_heads=GQA_HEADS, group=GQA_HEADS // GQA_KV_HEADS,
                     dk=HEAD_DIM, dv=HEAD_DIM, name="ctx_gqa", **common)
    return o_a, o_b, o_c


def _latent_attention(q_mla, k_mla, v_mla, gq, gk, gv, na, kc_mla, vc_mla, caches, bias, l, seq):
    cache_na_k, cache_na_v, cache_gqa_k, cache_gqa_v = caches
    t = na.shape[0]
    past = kc_mla.shape[0] // (t // seq)
    own = lambda w, c: pl.BlockSpec((seq, w), lambda b, i: (b, c))
    flat = lambda w: pl.BlockSpec((past, w), lambda b, i: (b, 0))
    cached = lambda heads: pl.BlockSpec((None, None, past * heads, HEAD_DIM), lambda b, i: (b, l, 0, 0))
    qw, vw = MLA_HEADS * MLA_PAD, MLA_HEADS * HEAD_DIM
    nw = NA_HEADS * HEAD_DIM
    kw = GQA_KV_HEADS * HEAD_DIM
    o_c = _attention(gq, [(cache_gqa_k, cached(GQA_KV_HEADS), cache_gqa_v, cached(GQA_KV_HEADS), None, None,
                           GQA_KV_HEADS),
                          (gk, own(kw, 0), gv, own(kw, 0), None, None, 0)],
                     q_index=lambda b, i: (b, 0), n_heads=GQA_HEADS, group=GQA_HEADS // GQA_KV_HEADS,
                     dk=HEAD_DIM, dv=HEAD_DIM, name="lat_gqa",
                     t=t, grid=(t // seq, 1), tq=seq)
    tq = NA_TQ
    nq = seq // tq
    common = dict(t=t, grid=(t // seq, nq), tq=tq)
    o_a = _attention(q_mla, [(kc_mla, flat(qw), vc_mla, flat(vw), None, None, 0),
                             (k_mla, own(qw, 0), v_mla, own(vw, 0), None, None, 0)],
                     q_index=lambda b, i: (b * nq + i, 0), n_heads=MLA_HEADS, group=1, dk=MLA_PAD, dv=HEAD_DIM,
                     name="lat_mla", **common)
    win_step, win_size = _na_key_window(seq, tq)
    window = (win_step * GRID_W, win_size * GRID_W)
    bias_spec = pl.BlockSpec((NA_HEADS, tq, window[1]), lambda b, i: (0, i, 0))
    o_b = _attention(na, [(cache_na_k, cached(NA_HEADS), cache_na_v, cached(NA_HEADS), None, None, NA_HEADS),
                          (na, own(nw, 1), na, own(nw, 2), bias, bias_spec, 0, window)],
                     q_index=lambda b, i: (b * nq + i, 0), n_heads=NA_HEADS, group=1,
                     dk=HEAD_DIM, dv=HEAD_DIM, name="lat_na", **common)
    return o_a, o_b, o_c


def _outproj_kernel(x_ref, gt_ref, oa_ref, ob_ref, oc_ref, w_ref, o_ref):
    wa = oa_ref.shape[1]
    wb = ob_ref.shape[1]
    acc = jnp.dot(oa_ref[...], w_ref[0:wa, :], preferred_element_type=F32)
    acc += jnp.dot(ob_ref[...], w_ref[wa:wa + wb, :], preferred_element_type=F32)
    acc += jnp.dot(oc_ref[...], w_ref[wa + wb:, :], preferred_element_type=F32)
    o_ref[...] = x_ref[...] + gt_ref[0] * acc


def _outproj(x, mods, l, stream, o_a, o_b, o_c, w_out):
    t, d = x.shape
    tm = 512
    row = lambda a: pl.BlockSpec((tm, a.shape[1]), lambda i: (i, 0))
    return pl.pallas_call(
        _outproj_kernel,
        grid=(t // tm,),
        in_specs=[
            row(x),
            _mod_spec(l, 5, stream, tm),
            row(o_a), row(o_b), row(o_c),
            _resident((None, d, d), lambda i: (l, 0, 0)),
        ],
        out_specs=row(x),
        out_shape=jax.ShapeDtypeStruct((t, d), F32),
        compiler_params=_params("parallel"),
        name="outproj",
    )(x, mods, o_a, o_b, o_c, w_out)


def _project_layout_kernel(w_ref, o_ref):
    split = COL_KR + MLA_ROPE
    o_ref[:, :split] = w_ref[:, :split].astype(BF16)
    o_ref[:, split:COL_NA] = jnp.zeros((o_ref.shape[0], COL_NA - split), BF16)
    o_ref[:, COL_NA:] = w_ref[:, split:].astype(BF16)


def _project_layout(w_in):
    depth, d, n = w_in.shape
    tk = 256
    return pl.pallas_call(
        _project_layout_kernel,
        grid=(depth, d // tk),
        in_specs=[pl.BlockSpec((None, tk, n), lambda l, i: (l, i, 0))],
        out_specs=pl.BlockSpec((None, tk, PROJ_W), lambda l, i: (l, i, 0)),
        out_shape=jax.ShapeDtypeStruct((depth, d, PROJ_W), BF16),
        compiler_params=_params("parallel", "parallel"),
        name="project_layout",
    )(w_in)


def _rope_tables(seq, d):
    quarter = d // 4
    tt = np.arange(seq)
    pos = np.stack([tt // GRID_W, tt % GRID_W], axis=-1).astype(np.float64)
    inv = ROPE_THETA ** (-np.arange(quarter, dtype=np.float64) / quarter)
    ang = pos[:, :, None] * inv
    cos = np.cos(ang)
    sin = np.sin(ang)
    cos_t = np.stack([cos, cos], axis=2).reshape(seq, d)
    sin_t = np.stack([-sin, sin], axis=2).reshape(seq, d)
    pad = LANES - d
    if pad:
        cos_t = np.concatenate([cos_t, np.ones((seq, pad))], axis=-1)
        sin_t = np.concatenate([sin_t, np.zeros((seq, pad))], axis=-1)
    return jnp.asarray(cos_t, F32), jnp.asarray(sin_t, F32)


def _na_key_window(seq, tq):
    rows = seq // GRID_W
    wh = min(NA_WIN_H, rows)
    qr = tq // GRID_W
    first = lambda r: min(max(r - wh // 2, 0), rows - wh)
    spans = [(first(i * qr), first(i * qr + qr - 1) + wh) for i in range(rows // qr)]
    size = max(hi - lo for lo, hi in spans)
    size += size % 2
    starts = [min(lo, rows - size) for lo, _ in spans]
    step = starts[1] - starts[0] if len(starts) > 1 else 0
    assert all(s == i * step and s <= lo and hi <= s + size for i, (s, (lo, hi)) in enumerate(zip(starts, spans)))
    return step, size


def _na_bias_kernel(t_ref, o_ref, *, rows, wh, qr, win_step, win_size):
    tile = pl.program_id(1)
    k0 = tile * win_step
    shape = (GRID_W, 2 * GRID_W)
    c = lax.broadcasted_iota(jnp.int32, shape, 0)
    lane = lax.broadcasted_iota(jnp.int32, shape, 1)
    kc = lane % GRID_W
    cs = jnp.clip(c - NA_WIN_W // 2, 0, GRID_W - NA_WIN_W)
    col_ok = (kc >= cs) & (kc < cs + NA_WIN_W)

    def one_row(rl, carry):
        r = tile * qr + rl
        rs = jnp.clip(r - wh // 2, 0, rows - wh)
        q_rows = pl.ds(pl.multiple_of(rl * GRID_W, GRID_W), GRID_W)
        for j in range(win_size // 2):
            kr = k0 + 2 * j + lane // GRID_W
            ok = col_ok & (kr >= rs) & (kr < rs + wh)
            d = jnp.clip(k0 + 2 * j - r + NA_WIN_H, 0, 2 * NA_WIN_H - 1)
            pair = jnp.broadcast_to(t_ref[0, pl.ds(d, 1), :], shape)
            toeplitz = pltpu.roll(pair, 2 * GRID_W - (NA_WIN_W - 1), 1, stride=1, stride_axis=0)
            o_ref[0, q_rows, j * 2 * GRID_W:(j + 1) * 2 * GRID_W] = jnp.where(ok, toeplitz * LOG2E, NEG_INF)
        return carry

    lax.fori_loop(0, qr, one_row, 0)


def _na_bias(rpb, seq, tq):
    nh, nr, nc = rpb.shape
    rows = seq // GRID_W
    wh = min(NA_WIN_H, rows)
    win_step, win_size = _na_key_window(seq, tq)
    padded = jnp.pad(rpb.astype(F32), ((0, 0), (1, 1), (0, GRID_W - nc)))
    table = jnp.concatenate([padded[:, :nr + 1], padded[:, 1:]], axis=-1)
    return pl.pallas_call(
        functools.partial(_na_bias_kernel, rows=rows, wh=wh, qr=tq // GRID_W, win_step=win_step,
                          win_size=win_size),
        grid=(nh, seq // tq),
        in_specs=[pl.BlockSpec((1, nr + 1, 2 * GRID_W), lambda h, i: (h, 0, 0))],
        out_specs=pl.BlockSpec((1, tq, win_size * GRID_W), lambda h, i: (h, i, 0)),
        out_shape=jax.ShapeDtypeStruct((nh, seq, win_size * GRID_W), F32),
        compiler_params=_params("parallel", "parallel"),
        name="na_bias",
    )(table)


def kernel(x_prompt, x_sample, cache_mla_ckv, cache_mla_krope, cache_na_k, cache_na_v, cache_gqa_k, cache_gqa_v, c, c_ctx, ada_w, ada_b, norm_g, ffn_wg, ffn_wu, ffn_wd, w_in, mla_q_norm, mla_wqb, mla_kv_norm, mla_wkvb, na_rpb, gqa_q_norm, gqa_k_norm, w_out, final_norm):
    depth = ada_w.shape[0]
    nb_ctx, seq_ctx, d = x_prompt.shape
    nb_lat, seq_lat, _ = x_sample.shape
    past = cache_mla_ckv.shape[2]

    wg = ffn_wg
    wu = ffn_wu
    wd = ffn_wd
    w_in_p = _project_layout(w_in)
    wqb =jnp.pad(mla_wqb.reshape(depth, Q_LORA, MLA_HEADS, MLA_QK),
                  ((0, 0), (0, 0), (0, 0), (0, MLA_PAD - MLA_QK))).reshape(depth, Q_LORA, MLA_HEADS * MLA_PAD)
    wkv = mla_wkvb.reshape(depth, KV_LORA, MLA_HEADS, 2 * HEAD_DIM)
    wts = {
        "q_norm": mla_q_norm.reshape(depth, 1, Q_LORA),
        "wqb": wqb.astype(BF16),
        "kv_norm": mla_kv_norm.reshape(depth, 1, KV_LORA),
        "wk": wkv[..., :MLA_NOPE].reshape(depth, KV_LORA, MLA_HEADS * MLA_NOPE).astype(BF16),
        "wv": wkv[..., MLA_NOPE:].reshape(depth, KV_LORA, MLA_HEADS * HEAD_DIM).astype(BF16),
        "gq_norm": gqa_q_norm.reshape(depth, 1, HEAD_DIM),
        "gk_norm": gqa_k_norm.reshape(depth, 1, HEAD_DIM),
    }
    w_out_b = w_out.astype(BF16)
    norm_g3 = norm_g.reshape(depth * 3, 1, d)
    final_g = final_norm.reshape(1, 1, d)
    tables = _rope_tables(seq_lat, MLA_ROPE) + _rope_tables(seq_lat, HEAD_DIM)
    kr_cache = jnp.pad(cache_mla_krope, ((0, 0), (0, 0), (0, 0), (0, LANES - MLA_ROPE)))
    caches = (cache_na_k.reshape(nb_lat, depth, past * NA_HEADS, HEAD_DIM),
              cache_na_v.reshape(nb_lat, depth, past * NA_HEADS, HEAD_DIM),
              cache_gqa_k.reshape(nb_lat, depth, past * GQA_KV_HEADS, HEAD_DIM),
              cache_gqa_v.reshape(nb_lat, depth, past * GQA_KV_HEADS, HEAD_DIM))

    cond = jnp.concatenate([c_ctx[None, :], c, jnp.zeros((MOD_ROWS - 1 - nb_lat, d), c.dtype)], axis=0)
    mods = _modulation(cond, ada_w, ada_b).reshape(depth * MOD_ROWS, 1, N_MOD * d)

    xp = x_prompt.reshape(nb_ctx * seq_ctx, d)
    xs = x_sample.reshape(nb_lat * seq_lat, d)
    ctx_row = (0, nb_ctx * seq_ctx)
    lat_row = (1, seq_lat)
    biases = [_na_bias(na_rpb[l], seq_lat, NA_TQ) for l in range(depth)]

    new_cache = None
    for l in range(depth):
        last = l == depth - 1
        xp = _ffn(xp, mods, norm_g3, l, 0, 0, ctx_row, wg, wu, wd)
        outs = _qkv(xp, mods, norm_g3, l, ctx_row, w_in_p, wts, seq_ctx, cache=(nb_ctx, depth, new_cache))
        q_mla, k_mla, v_mla, gq, gk, gv, na = outs[:N_QKV_OUT]
        new_cache = outs[N_QKV_OUT:]
        o_a, o_b, o_c = _context_attention(q_mla, k_mla, v_mla, gq, gk, gv, na, seq_ctx)
        xp = _outproj(xp, mods, l, ctx_row, o_a, o_b, o_c, w_out_b)
        xp = _ffn(xp, mods, norm_g3, l, 2, 1, ctx_row, wg, wu, wd, final_g if last else None)
        xs = _ffn(xs, mods, norm_g3, l, 0, 0, lat_row, wg, wu, wd)
        q_mla, k_mla, v_mla, gq, gk, gv, na = _qkv(xs, mods, norm_g3, l, lat_row, w_in_p, wts, 256, tables=tables)
        kc_mla, vc_mla = _cache_kv(cache_mla_ckv, kr_cache, l, wts)
        o_a, o_b, o_c = _latent_attention(q_mla, k_mla, v_mla, gq, gk, gv, na, kc_mla, vc_mla, caches, biases[l], l,
                                          seq_lat)
        xs = _outproj(xs, mods, l, lat_row, o_a, o_b, o_c, w_out_b)
        xs = _ffn(xs, mods, norm_g3, l, 2, 1, lat_row, wg, wu, wd, final_g if last else None)

    y_prompt = xp.reshape(nb_ctx, seq_ctx, d)
    y_sample = xs.reshape(nb_lat, seq_lat, d)
    ckv_c, kr_c, nak_c, nav_c, gk_c, gv_c = new_cache
    na_shape = (nb_ctx, depth, seq_ctx, NA_HEADS, HEAD_DIM)
    gqa_shape = (nb_ctx, depth, seq_ctx, GQA_KV_HEADS, HEAD_DIM)
    return (y_prompt, y_sample, ckv_c, kr_c, nak_c.reshape(na_shape), nav_c.reshape(na_shape),
            gk_c.reshape(gqa_shape), gv_c.reshape(gqa_shape))
```

```python
import functools

import jax
import jax.numpy as jnp
import numpy as np
from jax import lax
from jax.experimental import pallas as pl
from jax.experimental.pallas import tpu as pltpu

F32 = jnp.float32
BF16 = jnp.bfloat16

D_MODEL = 2048
D_FF = 5632
N_MOD = 9
GRID_W = 64
ROPE_THETA = 10000.0
HEAD_DIM = 128
MLA_HEADS = 8
MLA_NOPE = 128
MLA_ROPE = 64
MLA_QK = MLA_NOPE + MLA_ROPE
MLA_PAD = 256
Q_LORA = 512
KV_LORA = 256
NA_HEADS = 4
NA_WIN_H = 8
NA_WIN_W = 16
NA_TQ = 512
GQA_HEADS = 4
GQA_KV_HEADS = 2
EPS = 1e-6
NEG_INF = -1e30
LOG2E = 1.4426950408889634
MOD_ROWS = 16

LANES = 128
COL_CQ = 0
COL_CKV = COL_CQ + Q_LORA
COL_KR = COL_CKV + KV_LORA
COL_NA = COL_KR + LANES
COL_GQ = COL_NA + 3 * NA_HEADS * HEAD_DIM
COL_GK = COL_GQ + GQA_HEADS * HEAD_DIM
COL_GV = COL_GK + GQA_KV_HEADS * HEAD_DIM
PROJ_W = COL_GV + GQA_KV_HEADS * HEAD_DIM

VMEM_LIMIT = 62 * 1024 * 1024


def _params(*sem):
    return pltpu.CompilerParams(dimension_semantics=sem, vmem_limit_bytes=VMEM_LIMIT)


def _resident(shape, index_map):
    return pl.BlockSpec(shape, index_map, pipeline_mode=pl.Buffered(1))


def _as_bf16(ref):
    v = ref[...]
    return v if v.dtype == BF16 else v.astype(BF16)


def _rms(x, g):
    return x * lax.rsqrt(jnp.mean(x * x, axis=-1, keepdims=True) + EPS) * g


NORM_ROWS = 256


def _norm_modulate_store(x_ref, ng_ref, sc_ref, sh_ref, h_scr):
    gain = ng_ref[0] * (1.0 + sc_ref[0])
    shift = sh_ref[0]

    def body(i, carry):
        rows = pl.ds(pl.multiple_of(i * NORM_ROWS, NORM_ROWS), NORM_ROWS)
        x = x_ref[rows, :]
        r = lax.rsqrt(jnp.mean(x * x, axis=-1, keepdims=True) + EPS)
        h_scr[rows, :] = (x * r * gain + shift).astype(BF16)
        return carry

    lax.fori_loop(0, x_ref.shape[0] // NORM_ROWS, body, 0)


def _rope(x, cos, sin, quarter):
    n = x.shape[-1]
    lane = lax.broadcasted_iota(jnp.int32, x.shape, 1)
    first = (lane % (2 * quarter)) < quarter
    sw = jnp.where(first, pltpu.roll(x, n - quarter, 1), pltpu.roll(x, quarter, 1))
    return x * cos + sw * sin


def _mod_kernel(c_ref, w_ref, b_ref, o_ref):
    c = c_ref[...]
    s = (c * jax.nn.sigmoid(c)).astype(BF16)
    o_ref[0] = jnp.dot(s, w_ref[0].astype(BF16), preferred_element_type=F32) + b_ref[0]


def _modulation(cond, ada_w, ada_b):
    depth, d, n = ada_w.shape
    tn = 1024
    return pl.pallas_call(
        _mod_kernel,
        grid=(depth, n // tn),
        in_specs=[
            pl.BlockSpec((MOD_ROWS, d), lambda l, j: (0, 0)),
            pl.BlockSpec((1, d, tn), lambda l, j: (l, 0, j)),
            pl.BlockSpec((1, 1, tn), lambda l, j: (l, 0, j)),
        ],
        out_specs=pl.BlockSpec((1, MOD_ROWS, tn), lambda l, j: (l, 0, j)),
        out_shape=jax.ShapeDtypeStruct((depth, MOD_ROWS, n), F32),
        compiler_params=_params("parallel", "parallel"),
        name="modulation",
    )(cond, ada_w, ada_b.reshape(depth, 1, n))


def _mod_spec(l, chunk, stream, tm):
    base, tokens = stream
    return pl.BlockSpec((1, 1, D_MODEL), lambda i, *_: (l * MOD_ROWS + base + (i * tm) // tokens, 0, chunk))


def _ffn_kernel(x_ref, ng_ref, sh_ref, sc_ref, gt_ref, wg_ref, wu_ref, wd_ref, *rest, final):
    if final:
        fg_ref, o_ref, h_scr = rest
    else:
        o_ref, h_scr = rest
    f = pl.program_id(1)
    tm = x_ref.shape[0]

    def swiglu(rows, wg, wu, wd):
        h = h_scr[rows, :]
        g = jnp.dot(h, wg, preferred_element_type=F32)
        u = jnp.dot(h, wu, preferred_element_type=F32)
        a = (g * jax.nn.sigmoid(g) * u).astype(BF16)
        return jnp.dot(a, wd, preferred_element_type=F32)

    @pl.when(f == 0)
    def _():
        wg, wu, wd = _as_bf16(wg_ref), _as_bf16(wu_ref), _as_bf16(wd_ref)
        gain = ng_ref[0] * (1.0 + sc_ref[0])
        shift = sh_ref[0]
        for half in range(2):
            for r0 in range(half * tm // 2, (half + 1) * tm // 2, NORM_ROWS):
                x = x_ref[r0:r0 + NORM_ROWS, :]
                r = lax.rsqrt(jnp.mean(x * x, axis=-1, keepdims=True) + EPS)
                h_scr[r0:r0 + NORM_ROWS, :] = (x * r * gain + shift).astype(BF16)
            rows = slice(half * tm // 2, (half + 1) * tm // 2)
            o_ref[rows, :] = swiglu(rows, wg, wu, wd)

    @pl.when(f > 0)
    def _():
        o_ref[...] += swiglu(slice(None), _as_bf16(wg_ref), _as_bf16(wu_ref), _as_bf16(wd_ref))

    @pl.when(f == pl.num_programs(1) - 1)
    def _():
        gate = 0.5 * gt_ref[0]

        def body(i, carry):
            rows = pl.ds(pl.multiple_of(i * NORM_ROWS, NORM_ROWS), NORM_ROWS)
            y = x_ref[rows, :] + gate * o_ref[rows, :]
            if final:
                y = _rms(y, fg_ref[0])
            o_ref[rows, :] = y
            return carry

        lax.fori_loop(0, o_ref.shape[0] // NORM_ROWS, body, 0)


def _ffn(x, mods, norm_g, l, sub, k, stream, wg, wu, wd, final_g=None):
    t, d = x.shape
    tm, tf = 1024, 512
    chunk = 3 * sub
    in_specs = [
        pl.BlockSpec((tm, d), lambda i, j: (i, 0), pipeline_mode=pl.Buffered(1)),
        pl.BlockSpec((1, 1, d), lambda i, j: (l * 3 + sub, 0, 0)),
        _mod_spec(l, chunk, stream, tm),
        _mod_spec(l, chunk + 1, stream, tm),
        _mod_spec(l, chunk + 2, stream, tm),
        pl.BlockSpec((None, None, d, tf), lambda i, j: (l, k, 0, j)),
        pl.BlockSpec((None, None, d, tf), lambda i, j: (l, k, 0, j)),
        pl.BlockSpec((None, None, tf, d), lambda i, j: (l, k, j, 0)),
    ]
    args = [x, norm_g, mods, mods, mods, wg, wu, wd]
    if final_g is not None:
        in_specs.append(pl.BlockSpec((1, 1, d), lambda i, j: (0, 0, 0)))
        args.append(final_g)
    return pl.pallas_call(
        functools.partial(_ffn_kernel, final=final_g is not None),
        grid=(t // tm, D_FF // tf),
        in_specs=in_specs,
        out_specs=pl.BlockSpec((tm, d), lambda i, j: (i, 0)),
        out_shape=jax.ShapeDtypeStruct((t, d), F32),
        scratch_shapes=[pltpu.VMEM((tm, d), BF16)],
        compiler_params=_params("parallel", "arbitrary"),
        name="ffn",
    )(*args)


def _expand_kv(ckv_bf, kr, wk_ref, wv_ref, k_out, v_out):
    kn = jnp.dot(ckv_bf, wk_ref[...], preferred_element_type=F32)
    v_out[...] = jnp.dot(ckv_bf, wv_ref[...], preferred_element_type=F32).astype(BF16)
    kr_bf = kr.astype(BF16)
    for h in range(MLA_HEADS):
        k_out[:, h * MLA_PAD:h * MLA_PAD + MLA_NOPE] = kn[:, h * MLA_NOPE:(h + 1) * MLA_NOPE].astype(BF16)
        k_out[:, h * MLA_PAD + MLA_NOPE:(h + 1) * MLA_PAD] = kr_bf


N_QKV_OUT = 7
N_CACHE_OUT = 6


def _qkv_kernel(x_ref, ng_ref, sh_ref, sc_ref, w_ref, qn_ref, wqb_ref, kvn_ref, wk_ref, wv_ref, gqn_ref, gkn_ref,
                *rest, rope, n_prev, cache_out, layer):
    if rope:
        c64_ref, s64_ref, c128_ref, s128_ref = rest[:4]
        rest = rest[4:]
        rot64 = lambda v: _rope(v, c64_ref[...], s64_ref[...], MLA_ROPE // 4)
        rot128 = lambda v: _rope(v, c128_ref[...], s128_ref[...], HEAD_DIM // 4)
    else:
        rot64 = rot128 = lambda v: v
    rest = rest[n_prev:]
    q_out, k_out, v_out, gq_out, gk_out, gv_out, na_out = rest[:N_QKV_OUT]
    rest = rest[N_QKV_OUT:]
    if cache_out:
        cache_refs = rest[:N_CACHE_OUT]
        rest = rest[N_CACHE_OUT:]
        if n_prev == 0:
            for ref in cache_refs:
                for other in range(ref.shape[0]):
                    if other != layer:
                        ref[other] = jnp.zeros(ref.shape[1:], ref.dtype)
            cache_refs = [ref.at[layer] for ref in cache_refs]
        ckv_c, kr_c, nak_c, nav_c, gk_c, gv_c = cache_refs
    (h_scr,) = rest
    tm = x_ref.shape[0]

    _norm_modulate_store(x_ref, ng_ref, sc_ref, sh_ref, h_scr)
    h = h_scr[...]
    low = jnp.dot(h, w_ref[:, COL_CQ:COL_NA], preferred_element_type=F32)
    na = jnp.dot(h, w_ref[:, COL_NA:COL_GQ], preferred_element_type=F32)
    gg = jnp.dot(h, w_ref[:, COL_GQ:PROJ_W], preferred_element_type=F32)

    qn = _rms(low[:, COL_CQ:COL_CKV], qn_ref[0]).astype(BF16)
    q = jnp.dot(qn, wqb_ref[...], preferred_element_type=F32) * (MLA_QK ** -0.5 * LOG2E)
    for hh in range(MLA_HEADS):
        lo = hh * MLA_PAD
        q_out[:, lo:lo + MLA_NOPE] = q[:, lo:lo + MLA_NOPE].astype(BF16)
        q_out[:, lo + MLA_NOPE:lo + MLA_PAD] = rot64(q[:, lo + MLA_NOPE:lo + MLA_PAD]).astype(BF16)

    ckv = _rms(low[:, COL_CKV:COL_KR], kvn_ref[0])
    kr = rot64(low[:, COL_KR:COL_NA])
    _expand_kv(ckv.astype(BF16), kr, wk_ref, wv_ref, k_out, v_out)

    nw = NA_HEADS * HEAD_DIM
    gw = GQA_HEADS * HEAD_DIM
    kw = GQA_KV_HEADS * HEAD_DIM
    na_out[:, :nw] = (na[:, :nw] * (HEAD_DIM ** -0.5 * LOG2E)).astype(BF16)
    na_out[:, nw:] = na[:, nw:].astype(BF16)
    gq_gain = gqn_ref[0] * (HEAD_DIM ** -0.5 * LOG2E)
    for hh in range(GQA_HEADS):
        sl = slice(hh * HEAD_DIM, (hh + 1) * HEAD_DIM)
        gq_out[:, sl] = rot128(_rms(gg[:, sl], gq_gain)).astype(BF16)
    gv = gg[:, gw + kw:]
    gv_out[...] = gv.astype(BF16)
    for hh in range(GQA_KV_HEADS):
        sl = slice(hh * HEAD_DIM, (hh + 1) * HEAD_DIM)
        gk = rot128(_rms(gg[:, gw + hh * HEAD_DIM:gw + (hh + 1) * HEAD_DIM], gkn_ref[0]))
        gk_out[:, sl] = gk.astype(BF16)
        if cache_out:
            gk_c[pl.ds(hh, tm, stride=GQA_KV_HEADS), :] = gk
            gv_c[pl.ds(hh, tm, stride=GQA_KV_HEADS), :] = gv[:, sl]
    if cache_out:
        ckv_c[...] = ckv
        kr_c[...] = kr[:, :MLA_ROPE]
        for hh in range(NA_HEADS):
            sl = slice(hh * HEAD_DIM, (hh + 1) * HEAD_DIM)
            nak_c[pl.ds(hh, tm, stride=NA_HEADS), :] = na[:, nw + hh * HEAD_DIM:nw + (hh + 1) * HEAD_DIM]
            nav_c[pl.ds(hh, tm, stride=NA_HEADS), :] = na[:, 2 * nw + hh * HEAD_DIM:2 * nw + (hh + 1) * HEAD_DIM]


def _cache_shapes(nb, depth, seq):
    return [
        (nb, depth, seq, KV_LORA), (nb, depth, seq, MLA_ROPE),
        (nb, depth, seq * NA_HEADS, HEAD_DIM), (nb, depth, seq * NA_HEADS, HEAD_DIM),
        (nb, depth, seq * GQA_KV_HEADS, HEAD_DIM), (nb, depth, seq * GQA_KV_HEADS, HEAD_DIM),
    ]


def _qkv(x, mods, norm_g, l, stream, w_in, wts, tm, tables=None, cache=None):
    t, d = x.shape
    rope = tables is not None
    qw = MLA_HEADS * MLA_PAD
    vw = MLA_HEADS * HEAD_DIM
    in_specs = [
        pl.BlockSpec((tm, d), lambda i: (i, 0)),
        pl.BlockSpec((1, 1, d), lambda i: (l * 3 + 1, 0, 0)),
        _mod_spec(l, 3, stream, tm),
        _mod_spec(l, 4, stream, tm),
        _resident((None, d, PROJ_W), lambda i: (l, 0, 0)),
        pl.BlockSpec((1, 1, Q_LORA), lambda i: (l, 0, 0)),
        _resident((None, Q_LORA, qw), lambda i: (l, 0, 0)),
        pl.BlockSpec((1, 1, KV_LORA), lambda i: (l, 0, 0)),
        _resident((None, KV_LORA, vw), lambda i: (l, 0, 0)),
        _resident((None, KV_LORA, vw), lambda i: (l, 0, 0)),
        pl.BlockSpec((1, 1, HEAD_DIM), lambda i: (l, 0, 0)),
        pl.BlockSpec((1, 1, HEAD_DIM), lambda i: (l, 0, 0)),
    ]
    args = [x, norm_g, mods, mods, w_in, wts["q_norm"], wts["wqb"], wts["kv_norm"], wts["wk"], wts["wv"],
            wts["gq_norm"], wts["gk_norm"]]
    if rope:
        per = tables[0].shape[0] // tm
        in_specs += [pl.BlockSpec((tm, LANES), lambda i: (i % per, 0))] * 4
        args += list(tables)
    row = lambda w: pl.BlockSpec((tm, w), lambda i: (i, 0))
    widths = [qw, qw, vw, GQA_HEADS * HEAD_DIM, GQA_KV_HEADS * HEAD_DIM, GQA_KV_HEADS * HEAD_DIM,
              3 * NA_HEADS * HEAD_DIM]
    out_specs = [row(w) for w in widths]
    out_shape = [jax.ShapeDtypeStruct((t, w), BF16) for w in widths]
    aliases = {}
    n_prev = 0
    if cache is not None:
        nb, depth, prev = cache
        assert tm * nb == t
        shapes = _cache_shapes(nb, depth, tm)
        if prev is not None:
            n_prev = len(prev)
            aliases = {len(args) + k: N_QKV_OUT + k for k in range(n_prev)}
            in_specs += [pl.BlockSpec(memory_space=pl.ANY)] * n_prev
            args += list(prev)
            out_specs += [pl.BlockSpec((None, None) + s[2:], lambda i: (i, l, 0, 0)) for s in shapes]
        else:
            out_specs += [pl.BlockSpec((None,) + s[1:], lambda i: (i, 0, 0, 0)) for s in shapes]
        out_shape += [jax.ShapeDtypeStruct(s, F32) for s in shapes]
    return pl.pallas_call(
        functools.partial(_qkv_kernel, rope=rope, n_prev=n_prev, cache_out=cache is not None, layer=l),
        grid=(t // tm,),
        in_specs=in_specs,
        out_specs=out_specs,
        out_shape=out_shape,
        input_output_aliases=aliases,
        scratch_shapes=[pltpu.VMEM((tm, d), BF16)],
        compiler_params=_params("parallel"),
        name="qkv",
    )(*args)


def _cache_kv_kernel(ckv_ref, kr_ref, wk_ref, wv_ref, k_out, v_out):
    _expand_kv(ckv_ref[...].astype(BF16), kr_ref[...], wk_ref, wv_ref, k_out, v_out)


def _cache_kv(cache_ckv, cache_kr_pad, l, wts):
    nb, _, past, _ = cache_ckv.shape
    qw = MLA_HEADS * MLA_PAD
    vw = MLA_HEADS * HEAD_DIM
    return pl.pallas_call(
        _cache_kv_kernel,
        grid=(nb,),
        in_specs=[
            pl.BlockSpec((None, None, past, KV_LORA), lambda b: (b, l, 0, 0)),
            pl.BlockSpec((None, None, past, LANES), lambda b: (b, l, 0, 0)),
            pl.BlockSpec((None, KV_LORA, vw), lambda b: (l, 0, 0)),
            pl.BlockSpec((None, KV_LORA, vw), lambda b: (l, 0, 0)),
        ],
        out_specs=[pl.BlockSpec((past, qw), lambda b: (b, 0)), pl.BlockSpec((past, vw), lambda b: (b, 0))],
        out_shape=[jax.ShapeDtypeStruct((nb * past, qw), BF16), jax.ShapeDtypeStruct((nb * past, vw), BF16)],
        compiler_params=_params("parallel"),
        name="cache_kv",
    )(cache_ckv, cache_kr_pad, wts["wk"], wts["wv"])


def _attn_kernel(*refs, n_heads, group, dk, dv, seg_kinds, n_sub):
    q_ref = refs[0]
    pos = 1
    segs = []
    windows = [w for _, _, w in seg_kinds]
    for has_bias, inter, _ in seg_kinds:
        b_ref = refs[pos + 2] if has_bias else None
        segs.append((refs[pos], refs[pos + 1], b_ref, inter))
        pos += 3 if has_bias else 2
    o_ref = refs[pos]

    def head_rows(ref, rows, hk, width, inter):
        if inter:
            return ref[pl.ds(hk, ref.shape[0] // inter, stride=inter), :]
        return ref[rows, hk * width:(hk + 1) * width]

    def attend(q_rows, seg_rows):
        for h in range(n_heads):
            hk = h // group
            q = q_ref[q_rows, h * dk:(h + 1) * dk].astype(BF16)
            scores = []
            for (k_ref, _, b_ref, inter), rows in zip(segs, seg_rows):
                k = head_rows(k_ref, rows, hk, dk, inter).astype(BF16)
                s = lax.dot_general(q, k, (((1,), (1,)), ((), ())), preferred_element_type=F32)
                if b_ref is not None:
                    s = s + b_ref[h]
                scores.append(s)
            m = functools.reduce(jnp.maximum, [s.max(axis=-1, keepdims=True) for s in scores])
            o = None
            for s, (_, v_ref, _, inter), rows in zip(scores, segs, seg_rows):
                p = jnp.exp2(s - m).astype(BF16)
                v = head_rows(v_ref, rows, hk, dv, inter).astype(BF16)
                pv = jnp.dot(p, jnp.concatenate([v, jnp.ones_like(v)], axis=1), preferred_element_type=F32)
                o = pv if o is None else o + pv
            o_ref[q_rows, h * dv:(h + 1) * dv] = (o[:, :dv] / o[:, dv:]).astype(o_ref.dtype)

    if n_sub == 1:
        attend(slice(None), [slice(None) if w is None else
                             pl.ds(pl.multiple_of(pl.program_id(1) * w[0], w[0]), w[1]) for w in windows])
    else:
        def body(s, carry):
            rows = lambda ref: pl.ds(pl.multiple_of(s * (ref.shape[0] // n_sub), ref.shape[0] // n_sub),
                                     ref.shape[0] // n_sub)
            attend(rows(q_ref), [rows(seg[0]) for seg in segs])
            return carry

        lax.fori_loop(0, n_sub, body, 0)


def _attention(q, segs, *, t, grid, tq, q_index, n_heads, group, dk, dv, name, n_sub=1):
    assert n_sub == 1 or all(s[4] is None and not s[6] and len(s) == 7 for s in segs)
    in_specs = [pl.BlockSpec((tq, n_heads * dk), q_index)]
    args = [q]
    seg_kinds = []
    for k_arr, k_spec, v_arr, v_spec, b_arr, b_spec, inter, *window in segs:
        in_specs += [k_spec, v_spec]
        args += [k_arr, v_arr]
        seg_kinds.append((b_arr is not None, inter, window[0] if window else None))
        if b_arr is not None:
            in_specs.append(b_spec)
            args.append(b_arr)
    out_index = lambda *g: (q_index(*g)[0], 0)
    return pl.pallas_call(
        functools.partial(_attn_kernel, n_heads=n_heads, group=group, dk=dk, dv=dv,
                          seg_kinds=tuple(seg_kinds), n_sub=n_sub),
        grid=grid,
        in_specs=in_specs,
        out_specs=pl.BlockSpec((tq, n_heads * dv), out_index),
        out_shape=jax.ShapeDtypeStruct((t, n_heads * dv), BF16),
        compiler_params=_params(*(("parallel",) * len(grid))),
        name=name,
    )(*args)


def _context_attention(q_mla, k_mla, v_mla, gq, gk, gv, na, seq):
    t = na.shape[0]
    n_sub = 4
    rows = seq * n_sub
    grid = (t // rows,)
    blk = lambda w, c: pl.BlockSpec((rows, w), lambda b: (b, c))
    common = dict(t=t, grid=grid, tq=rows, n_sub=n_sub)
    nw = NA_HEADS * HEAD_DIM
    kw = GQA_KV_HEADS * HEAD_DIM
    o_a = _attention(q_mla, [(k_mla, blk(MLA_HEADS * MLA_PAD, 0), v_mla, blk(MLA_HEADS * HEAD_DIM, 0), None, None, 0)],
                     q_index=lambda b: (b, 0), n_heads=MLA_HEADS, group=1, dk=MLA_PAD, dv=HEAD_DIM,
                     name="ctx_mla", **common)
    o_b = _attention(na, [(na, blk(nw, 1), na, blk(nw, 2), None, None, 0)],
                     q_index=lambda b: (b, 0), n_heads=NA_HEADS, group=1, dk=HEAD_DIM, dv=HEAD_DIM,
                     name="ctx_na", **common)
    o_c = _attention(gq, [(gk, blk(kw, 0), gv, blk(kw, 0), None, None, 0)],
                     q_index=lambda b: (b, 0), n_heads=GQA_HEADS, group=GQA_HEADS // GQA_KV_HEADS,
                     dk=HEAD_DIM, dv=HEAD_DIM, name="ctx_gqa", **common)
    return o_a, o_b, o_c


def _latent_attention(q_mla, k_mla, v_mla, gq, gk, gv, na, kc_mla, vc_mla, caches, bias, l, seq):
    cache_na_k, cache_na_v, cache_gqa_k, cache_gqa_v = caches
    t = na.shape[0]
    past = kc_mla.shape[0] // (t // seq)
    own = lambda w, c: pl.BlockSpec((seq, w), lambda b, i: (b, c))
    flat = lambda w: pl.BlockSpec((past, w), lambda b, i: (b, 0))
    cached = lambda heads: pl.BlockSpec((None, None, past * heads, HEAD_DIM), lambda b, i: (b, l, 0, 0))
    qw, vw = MLA_HEADS * MLA_PAD, MLA_HEADS * HEAD_DIM
    nw = NA_HEADS * HEAD_DIM
    kw = GQA_KV_HEADS * HEAD_DIM
    o_c = _attention(gq, [(cache_gqa_k, cached(GQA_KV_HEADS), cache_gqa_v, cached(GQA_KV_HEADS), None, None,
                           GQA_KV_HEADS),
                          (gk, own(kw, 0), gv, own(kw, 0), None, None, 0)],
                     q_index=lambda b, i: (b, 0), n_heads=GQA_HEADS, group=GQA_HEADS // GQA_KV_HEADS,
                     dk=HEAD_DIM, dv=HEAD_DIM, name="lat_gqa",
                     t=t, grid=(t // seq, 1), tq=seq)
    tq = NA_TQ
    nq = seq // tq
    common = dict(t=t, grid=(t // seq, nq), tq=tq)
    o_a = _attention(q_mla, [(kc_mla, flat(qw), vc_mla, flat(vw), None, None, 0),
                             (k_mla, own(qw, 0), v_mla, own(vw, 0), None, None, 0)],
                     q_index=lambda b, i: (b * nq + i, 0), n_heads=MLA_HEADS, group=1, dk=MLA_PAD, dv=HEAD_DIM,
                     name="lat_mla", **common)
    win_step, win_size = _na_key_window(seq, tq)
    window = (win_step * GRID_W, win_size * GRID_W)
    bias_spec = pl.BlockSpec((NA_HEADS, tq, window[1]), lambda b, i: (0, i, 0))
    o_b = _attention(na, [(cache_na_k, cached(NA_HEADS), cache_na_v, cached(NA_HEADS), None, None, NA_HEADS),
                          (na, own(nw, 1), na, own(nw, 2), bias, bias_spec, 0, window)],
                     q_index=lambda b, i: (b * nq + i, 0), n_heads=NA_HEADS, group=1,
                     dk=HEAD_DIM, dv=HEAD_DIM, name="lat_na", **common)
    return o_a, o_b, o_c


def _outproj_kernel(x_ref, gt_ref, oa_ref, ob_ref, oc_ref, w_ref, o_ref):
    wa = oa_ref.shape[1]
    wb = ob_ref.shape[1]
    acc = jnp.dot(oa_ref[...], w_ref[0:wa, :], preferred_element_type=F32)
    acc += jnp.dot(ob_ref[...], w_ref[wa:wa + wb, :], preferred_element_type=F32)
    acc += jnp.dot(oc_ref[...], w_ref[wa + wb:, :], preferred_element_type=F32)
    o_ref[...] = x_ref[...] + gt_ref[0] * acc


def _outproj(x, mods, l, stream, o_a, o_b, o_c, w_out):
    t, d = x.shape
    tm = 512
    row = lambda a: pl.BlockSpec((tm, a.shape[1]), lambda i: (i, 0))
    return pl.pallas_call(
        _outproj_kernel,
        grid=(t // tm,),
        in_specs=[
            row(x),
            _mod_spec(l, 5, stream, tm),
            row(o_a), row(o_b), row(o_c),
            _resident((None, d, d), lambda i: (l, 0, 0)),
        ],
        out_specs=row(x),
        out_shape=jax.ShapeDtypeStruct((t, d), F32),
        compiler_params=_params("parallel"),
        name="outproj",
    )(x, mods, o_a, o_b, o_c, w_out)


def _project_layout_kernel(w_ref, o_ref):
    split = COL_KR + MLA_ROPE
    o_ref[:, :split] = w_ref[:, :split].astype(BF16)
    o_ref[:, split:COL_NA] = jnp.zeros((o_ref.shape[0], COL_NA - split), BF16)
    o_ref[:, COL_NA:] = w_ref[:, split:].astype(BF16)


def _project_layout(w_in):
    depth, d, n = w_in.shape
    tk = 256
    return pl.pallas_call(
        _project_layout_kernel,
        grid=(depth, d // tk),
        in_specs=[pl.BlockSpec((None, tk, n), lambda l, i: (l, i, 0))],
        out_specs=pl.BlockSpec((None, tk, PROJ_W), lambda l, i: (l, i, 0)),
        out_shape=jax.ShapeDtypeStruct((depth, d, PROJ_W), BF16),
        compiler_params=_params("parallel", "parallel"),
        name="project_layout",
    )(w_in)


def _rope_tables(seq, d):
    quarter = d // 4
    tt = np.arange(seq)
    pos = np.stack([tt // GRID_W, tt % GRID_W], axis=-1).astype(np.float64)
    inv = ROPE_THETA ** (-np.arange(quarter, dtype=np.float64) / quarter)
    ang = pos[:, :, None] * inv
    cos = np.cos(ang)
    sin = np.sin(ang)
    cos_t = np.stack([cos, cos], axis=2).reshape(seq, d)
    sin_t = np.stack([-sin, sin], axis=2).reshape(seq, d)
    pad = LANES - d
    if pad:
        cos_t = np.concatenate([cos_t, np.ones((seq, pad))], axis=-1)
        sin_t = np.concatenate([sin_t, np.zeros((seq, pad))], axis=-1)
    return jnp.asarray(cos_t, F32), jnp.asarray(sin_t, F32)


def _na_key_window(seq, tq):
    rows = seq // GRID_W
    wh = min(NA_WIN_H, rows)
    qr = tq // GRID_W
    first = lambda r: min(max(r - wh // 2, 0), rows - wh)
    spans = [(first(i * qr), first(i * qr + qr - 1) + wh) for i in range(rows // qr)]
    size = max(hi - lo for lo, hi in spans)
    size += size % 2
    starts = [min(lo, rows - size) for lo, _ in spans]
    step = starts[1] - starts[0] if len(starts) > 1 else 0
    assert all(s == i * step and s <= lo and hi <= s + size for i, (s, (lo, hi)) in enumerate(zip(starts, spans)))
    return step, size


def _na_bias_kernel(t_ref, o_ref, *, rows, wh, qr, win_step, win_size):
    tile = pl.program_id(1)
    k0 = tile * win_step
    shape = (GRID_W, 2 * GRID_W)
    c = lax.broadcasted_iota(jnp.int32, shape, 0)
    lane = lax.broadcasted_iota(jnp.int32, shape, 1)
    kc = lane % GRID_W
    cs = jnp.clip(c - NA_WIN_W // 2, 0, GRID_W - NA_WIN_W)
    col_ok = (kc >= cs) & (kc < cs + NA_WIN_W)

    def one_row(rl, carry):
        r = tile * qr + rl
        rs = jnp.clip(r - wh // 2, 0, rows - wh)
        q_rows = pl.ds(pl.multiple_of(rl * GRID_W, GRID_W), GRID_W)
        for j in range(win_size // 2):
            kr = k0 + 2 * j + lane // GRID_W
            ok = col_ok & (kr >= rs) & (kr < rs + wh)
            d = jnp.clip(k0 + 2 * j - r + NA_WIN_H, 0, 2 * NA_WIN_H - 1)
            pair = jnp.broadcast_to(t_ref[0, pl.ds(d, 1), :], shape)
            toeplitz = pltpu.roll(pair, 2 * GRID_W - (NA_WIN_W - 1), 1, stride=1, stride_axis=0)
            o_ref[0, q_rows, j * 2 * GRID_W:(j + 1) * 2 * GRID_W] = jnp.where(ok, toeplitz * LOG2E, NEG_INF)
        return carry

    lax.fori_loop(0, qr, one_row, 0)


def _na_bias(rpb, seq, tq):
    nh, nr, nc = rpb.shape
    rows = seq // GRID_W
    wh = min(NA_WIN_H, rows)
    win_step, win_size = _na_key_window(seq, tq)
    padded = jnp.pad(rpb.astype(F32), ((0, 0), (1, 1), (0, GRID_W - nc)))
    table = jnp.concatenate([padded[:, :nr + 1], padded[:, 1:]], axis=-1)
    return pl.pallas_call(
        functools.partial(_na_bias_kernel, rows=rows, wh=wh, qr=tq // GRID_W, win_step=win_step,
                          win_size=win_size),
        grid=(nh, seq // tq),
        in_specs=[pl.BlockSpec((1, nr + 1, 2 * GRID_W), lambda h, i: (h, 0, 0))],
        out_specs=pl.BlockSpec((1, tq, win_size * GRID_W), lambda h, i: (h, i, 0)),
        out_shape=jax.ShapeDtypeStruct((nh, seq, win_size * GRID_W), F32),
        compiler_params=_params("parallel", "parallel"),
        name="na_bias",
    )(table)


def kernel(x_prompt, x_sample, cache_mla_ckv, cache_mla_krope, cache_na_k, cache_na_v, cache_gqa_k, cache_gqa_v, c, c_ctx, ada_w, ada_b, norm_g, ffn_wg, ffn_wu, ffn_wd, w_in, mla_q_norm, mla_wqb, mla_kv_norm, mla_wkvb, na_rpb, gqa_q_norm, gqa_k_norm, w_out, final_norm):
    depth = ada_w.shape[0]
    nb_ctx, seq_ctx, d = x_prompt.shape
    nb_lat, seq_lat, _ = x_sample.shape
    past = cache_mla_ckv.shape[2]

    wg = ffn_wg
    wu = ffn_wu
    wd = ffn_wd
    w_in_p = _project_layout(w_in)
    wqb =jnp.pad(mla_wqb.reshape(depth, Q_LORA, MLA_HEADS, MLA_QK),
                  ((0, 0), (0, 0), (0, 0), (0, MLA_PAD - MLA_QK))).reshape(depth, Q_LORA, MLA_HEADS * MLA_PAD)
    wkv = mla_wkvb.reshape(depth, KV_LORA, MLA_HEADS, 2 * HEAD_DIM)
    wts = {
        "q_norm": mla_q_norm.reshape(depth, 1, Q_LORA),
        "wqb": wqb.astype(BF16),
        "kv_norm": mla_kv_norm.reshape(depth, 1, KV_LORA),
        "wk": wkv[..., :MLA_NOPE].reshape(depth, KV_LORA, MLA_HEADS * MLA_NOPE).astype(BF16),
        "wv": wkv[..., MLA_NOPE:].reshape(depth, KV_LORA, MLA_HEADS * HEAD_DIM).astype(BF16),
        "gq_norm": gqa_q_norm.reshape(depth, 1, HEAD_DIM),
        "gk_norm": gqa_k_norm.reshape(depth, 1, HEAD_DIM),
    }
    w_out_b = w_out.astype(BF16)
    norm_g3 = norm_g.reshape(depth * 3, 1, d)
    final_g = final_norm.reshape(1, 1, d)
    tables = _rope_tables(seq_lat, MLA_ROPE) + _rope_tables(seq_lat, HEAD_DIM)
    kr_cache = jnp.pad(cache_mla_krope, ((0, 0), (0, 0), (0, 0), (0, LANES - MLA_ROPE)))
    caches = (cache_na_k.reshape(nb_lat, depth, past * NA_HEADS, HEAD_DIM),
              cache_na_v.reshape(nb_lat, depth, past * NA_HEADS, HEAD_DIM),
              cache_gqa_k.reshape(nb_lat, depth, past * GQA_KV_HEADS, HEAD_DIM),
              cache_gqa_v.reshape(nb_lat, depth, past * GQA_KV_HEADS, HEAD_DIM))

    cond = jnp.concatenate([c_ctx[None, :], c, jnp.zeros((MOD_ROWS - 1 - nb_lat, d), c.dtype)], axis=0)
    mods = _modulation(cond, ada_w, ada_b).reshape(depth * MOD_ROWS, 1, N_MOD * d)

    xp = x_prompt.reshape(nb_ctx * seq_ctx, d)
    xs = x_sample.reshape(nb_lat * seq_lat, d)
    ctx_row = (0, nb_ctx * seq_ctx)
    lat_row = (1, seq_lat)
    biases = [_na_bias(na_rpb[l], seq_lat, NA_TQ) for l in range(depth)]

    new_cache = None
    for l in range(depth):
        last = l == depth - 1
        xp = _ffn(xp, mods, norm_g3, l, 0, 0, ctx_row, wg, wu, wd)
        outs = _qkv(xp, mods, norm_g3, l, ctx_row, w_in_p, wts, seq_ctx, cache=(nb_ctx, depth, new_cache))
        q_mla, k_mla, v_mla, gq, gk, gv, na = outs[:N_QKV_OUT]
        new_cache = outs[N_QKV_OUT:]
        o_a, o_b, o_c = _context_attention(q_mla, k_mla, v_mla, gq, gk, gv, na, seq_ctx)
        xp = _outproj(xp, mods, l, ctx_row, o_a, o_b, o_c, w_out_b)
        xp = _ffn(xp, mods, norm_g3, l, 2, 1, ctx_row, wg, wu, wd, final_g if last else None)
        xs = _ffn(xs, mods, norm_g3, l, 0, 0, lat_row, wg, wu, wd)
        q_mla, k_mla, v_mla, gq, gk, gv, na = _qkv(xs, mods, norm_g3, l, lat_row, w_in_p, wts, 256, tables=tables)
        kc_mla, vc_mla = _cache_kv(cache_mla_ckv, kr_cache, l, wts)
        o_a, o_b, o_c = _latent_attention(q_mla, k_mla, v_mla, gq, gk, gv, na, kc_mla, vc_mla, caches, biases[l], l,
                                          seq_lat)
        xs = _outproj(xs, mods, l, lat_row, o_a, o_b, o_c, w_out_b)
        xs = _ffn(xs, mods, norm_g3, l, 2, 1, lat_row, wg, wu, wd, final_g if last else None)

    y_prompt = xp.reshape(nb_ctx, seq_ctx, d)
    y_sample = xs.reshape(nb_lat, seq_lat, d)
    ckv_c, kr_c, nak_c, nav_c, gk_c, gv_c = new_cache
    na_shape = (nb_ctx, depth, seq_ctx, NA_HEADS, HEAD_DIM)
    gqa_shape = (nb_ctx, depth, seq_ctx, GQA_KV_HEADS, HEAD_DIM)
    return (y_prompt, y_sample, ckv_c, kr_c, nak_c.reshape(na_shape), nav_c.reshape(na_shape),
            gk_c.reshape(gqa_shape), gv_c.reshape(gqa_shape))
```

```python
import functools

import jax
import jax.numpy as jnp
import numpy as np
from jax import lax
from jax.experimental import pallas as pl
from jax.experimental.pallas import tpu as pltpu

F32 = jnp.float32
BF16 = jnp.bfloat16

D_MODEL = 2048
D_FF = 5632
N_MOD = 9
GRID_W = 64
ROPE_THETA = 10000.0
HEAD_DIM = 128
MLA_HEADS = 8
MLA_NOPE = 128
MLA_ROPE = 64
MLA_QK = MLA_NOPE + MLA_ROPE
MLA_PAD = 256
Q_LORA = 512
KV_LORA = 256
NA_HEADS = 4
NA_WIN_H = 8
NA_WIN_W = 16
NA_TQ = 512
GQA_HEADS = 4
GQA_KV_HEADS = 2
EPS = 1e-6
NEG_INF = -1e30
LOG2E = 1.4426950408889634
MOD_ROWS = 16

LANES = 128
COL_CQ = 0
COL_CKV = COL_CQ + Q_LORA
COL_KR = COL_CKV + KV_LORA
COL_NA = COL_KR + LANES
COL_GQ = COL_NA + 3 * NA_HEADS * HEAD_DIM
COL_GK = COL_GQ + GQA_HEADS * HEAD_DIM
COL_GV = COL_GK + GQA_KV_HEADS * HEAD_DIM
PROJ_W = COL_GV + GQA_KV_HEADS * HEAD_DIM

VMEM_LIMIT = 62 * 1024 * 1024


def _params(*sem):
    return pltpu.CompilerParams(dimension_semantics=sem, vmem_limit_bytes=VMEM_LIMIT)


def _resident(shape, index_map):
    return pl.BlockSpec(shape, index_map, pipeline_mode=pl.Buffered(1))


def _as_bf16(ref):
    v = ref[...]
    return v if v.dtype == BF16 else v.astype(BF16)


def _rms(x, g):
    return x * lax.rsqrt(jnp.mean(x * x, axis=-1, keepdims=True) + EPS) * g


NORM_ROWS = 256


def _norm_modulate_store(x_ref, ng_ref, sc_ref, sh_ref, h_scr):
    gain = ng_ref[0] * (1.0 + sc_ref[0])
    shift = sh_ref[0]

    def body(i, carry):
        rows = pl.ds(pl.multiple_of(i * NORM_ROWS, NORM_ROWS), NORM_ROWS)
        x = x_ref[rows, :]
        r = lax.rsqrt(jnp.mean(x * x, axis=-1, keepdims=True) + EPS)
        h_scr[rows, :] = (x * r * gain + shift).astype(BF16)
        return carry

    lax.fori_loop(0, x_ref.shape[0] // NORM_ROWS, body, 0)


def _rope(x, cos, sin, quarter):
    n = x.shape[-1]
    lane = lax.broadcasted_iota(jnp.int32, x.shape, 1)
    first = (lane % (2 * quarter)) < quarter
    sw = jnp.where(first, pltpu.roll(x, n - quarter, 1), pltpu.roll(x, quarter, 1))
    return x * cos + sw * sin


def _mod_kernel(c_ref, w_ref, b_ref, o_ref):
    c = c_ref[...]
    s = (c * jax.nn.sigmoid(c)).astype(BF16)
    o_ref[0] = jnp.dot(s, w_ref[0].astype(BF16), preferred_element_type=F32) + b_ref[0]


def _modulation(cond, ada_w, ada_b):
    depth, d, n = ada_w.shape
    tn = 1024
    return pl.pallas_call(
        _mod_kernel,
        grid=(depth, n // tn),
        in_specs=[
            pl.BlockSpec((MOD_ROWS, d), lambda l, j: (0, 0)),
            pl.BlockSpec((1, d, tn), lambda l, j: (l, 0, j)),
            pl.BlockSpec((1, 1, tn), lambda l, j: (l, 0, j)),
        ],
        out_specs=pl.BlockSpec((1, MOD_ROWS, tn), lambda l, j: (l, 0, j)),
        out_shape=jax.ShapeDtypeStruct((depth, MOD_ROWS, n), F32),
        compiler_params=_params("parallel", "parallel"),
        name="modulation",
    )(cond, ada_w, ada_b.reshape(depth, 1, n))


def _mod_spec(l, chunk, stream, tm):
    base, tokens = stream
    return pl.BlockSpec((1, 1, D_MODEL), lambda i, *_: (l * MOD_ROWS + base + (i * tm) // tokens, 0, chunk))


def _ffn_kernel(x_ref, ng_ref, sh_ref, sc_ref, gt_ref, wg_ref, wu_ref, wd_ref, *rest, final):
    if final:
        fg_ref, o_ref, h_scr = rest
    else:
        o_ref, h_scr = rest
    f = pl.program_id(1)
    tm = x_ref.shape[0]

    def swiglu(rows, wg, wu, wd):
        h = h_scr[rows, :]
        g = jnp.dot(h, wg, preferred_element_type=F32)
        u = jnp.dot(h, wu, preferred_element_type=F32)
        a = (g * jax.nn.sigmoid(g) * u).astype(BF16)
        return jnp.dot(a, wd, preferred_element_type=F32)

    @pl.when(f == 0)
    def _():
        wg, wu, wd = _as_bf16(wg_ref), _as_bf16(wu_ref), _as_bf16(wd_ref)
        gain = ng_ref[0] * (1.0 + sc_ref[0])
        shift = sh_ref[0]
        for half in range(2):
            for r0 in range(half * tm // 2, (half + 1) * tm // 2, NORM_ROWS):
                x = x_ref[r0:r0 + NORM_ROWS, :]
                r = lax.rsqrt(jnp.mean(x * x, axis=-1, keepdims=True) + EPS)
                h_scr[r0:r0 + NORM_ROWS, :] = (x * r * gain + shift).astype(BF16)
            rows = slice(half * tm // 2, (half + 1) * tm // 2)
            o_ref[rows, :] = swiglu(rows, wg, wu, wd)

    last = pl.num_programs(1) - 1

    @pl.when((f > 0) & ((f < last) | final))
    def _():
        o_ref[...] += swiglu(slice(None), _as_bf16(wg_ref), _as_bf16(wu_ref), _as_bf16(wd_ref))

    @pl.when(f == last)
    def _():
        gate = 0.5 * gt_ref[0]
        if final:
            def body(i, carry):
                rows = pl.ds(pl.multiple_of(i * NORM_ROWS, NORM_ROWS), NORM_ROWS)
                o_ref[rows, :] = _rms(x_ref[rows, :] + gate * o_ref[rows, :], fg_ref[0])
                return carry

            lax.fori_loop(0, tm // NORM_ROWS, body, 0)
        else:
            wg, wu, wd = _as_bf16(wg_ref), _as_bf16(wu_ref), _as_bf16(wd_ref)
            for half in range(2):
                rows = slice(half * tm // 2, (half + 1) * tm // 2)
                o_ref[rows, :] += swiglu(rows, wg, wu, wd)
                for r0 in range(half * tm // 2, (half + 1) * tm // 2, NORM_ROWS):
                    o_ref[r0:r0 + NORM_ROWS, :] = x_ref[r0:r0 + NORM_ROWS, :] + gate * o_ref[r0:r0 + NORM_ROWS, :]


def _ffn(x, mods, norm_g, l, sub, k, stream, wg, wu, wd, final_g=None):
    t, d = x.shape
    tm, tf = 1024, 512
    chunk = 3 * sub
    in_specs = [
        pl.BlockSpec((tm, d), lambda i, j: (i, 0), pipeline_mode=pl.Buffered(1)),
        pl.BlockSpec((1, 1, d), lambda i, j: (l * 3 + sub, 0, 0)),
        _mod_spec(l, chunk, stream, tm),
        _mod_spec(l, chunk + 1, stream, tm),
        _mod_spec(l, chunk + 2, stream, tm),
        pl.BlockSpec((None, None, d, tf), lambda i, j: (l, k, 0, j)),
        pl.BlockSpec((None, None, d, tf), lambda i, j: (l, k, 0, j)),
        pl.BlockSpec((None, None, tf, d), lambda i, j: (l, k, j, 0)),
    ]
    args = [x, norm_g, mods, mods, mods, wg, wu, wd]
    if final_g is not None:
        in_specs.append(pl.BlockSpec((1, 1, d), lambda i, j: (0, 0, 0)))
        args.append(final_g)
    return pl.pallas_call(
        functools.partial(_ffn_kernel, final=final_g is not None),
        grid=(t // tm, D_FF // tf),
        in_specs=in_specs,
        out_specs=pl.BlockSpec((tm, d), lambda i, j: (i, 0)),
        out_shape=jax.ShapeDtypeStruct((t, d), F32),
        scratch_shapes=[pltpu.VMEM((tm, d), BF16)],
        compiler_params=_params("parallel", "arbitrary"),
        name="ffn",
    )(*args)


def _expand_kv(ckv_bf, kr, wk_ref, wv_ref, k_out, v_out):
    kn = jnp.dot(ckv_bf, wk_ref[...], preferred_element_type=F32)
    v_out[...] = jnp.dot(ckv_bf, wv_ref[...], preferred_element_type=F32).astype(BF16)
    kr_bf = kr.astype(BF16)
    for h in range(MLA_HEADS):
        k_out[:, h * MLA_PAD:h * MLA_PAD + MLA_NOPE] = kn[:, h * MLA_NOPE:(h + 1) * MLA_NOPE].astype(BF16)
        k_out[:, h * MLA_PAD + MLA_NOPE:(h + 1) * MLA_PAD] = kr_bf


N_QKV_OUT = 7
N_CACHE_OUT = 6


def _qkv_kernel(x_ref, ng_ref, sh_ref, sc_ref, w_ref, qn_ref, wqb_ref, kvn_ref, wk_ref, wv_ref, gqn_ref, gkn_ref,
                *rest, rope, n_prev, cache_out, layer):
    if rope:
        c64_ref, s64_ref, c128_ref, s128_ref = rest[:4]
        rest = rest[4:]
        rot64 = lambda v: _rope(v, c64_ref[...], s64_ref[...], MLA_ROPE // 4)
        rot128 = lambda v: _rope(v, c128_ref[...], s128_ref[...], HEAD_DIM // 4)
    else:
        rot64 = rot128 = lambda v: v
    rest = rest[n_prev:]
    q_out, k_out, v_out, gq_out, gk_out, gv_out, na_out = rest[:N_QKV_OUT]
    rest = rest[N_QKV_OUT:]
    if cache_out:
        cache_refs = rest[:N_CACHE_OUT]
        rest = rest[N_CACHE_OUT:]
        if n_prev == 0:
            for ref in cache_refs:
                for other in range(ref.shape[0]):
                    if other != layer:
                        ref[other] = jnp.zeros(ref.shape[1:], ref.dtype)
            cache_refs = [ref.at[layer] for ref in cache_refs]
        ckv_c, kr_c, nak_c, nav_c, gk_c, gv_c = cache_refs
    (h_scr,) = rest
    tm = x_ref.shape[0]

    _norm_modulate_store(x_ref, ng_ref, sc_ref, sh_ref, h_scr)
    h = h_scr[...]
    low = jnp.dot(h, w_ref[:, COL_CQ:COL_NA], preferred_element_type=F32)
    na = jnp.dot(h, w_ref[:, COL_NA:COL_GQ], preferred_element_type=F32)
    gg = jnp.dot(h, w_ref[:, COL_GQ:PROJ_W], preferred_element_type=F32)

    qn = _rms(low[:, COL_CQ:COL_CKV], qn_ref[0]).astype(BF16)
    q = jnp.dot(qn, wqb_ref[...], preferred_element_type=F32) * (MLA_QK ** -0.5 * LOG2E)
    for hh in range(MLA_HEADS):
        lo = hh * MLA_PAD
        q_out[:, lo:lo + MLA_NOPE] = q[:, lo:lo + MLA_NOPE].astype(BF16)
        q_out[:, lo + MLA_NOPE:lo + MLA_PAD] = rot64(q[:, lo + MLA_NOPE:lo + MLA_PAD]).astype(BF16)

    ckv = _rms(low[:, COL_CKV:COL_KR], kvn_ref[0])
    kr = rot64(low[:, COL_KR:COL_NA])
    _expand_kv(ckv.astype(BF16), kr, wk_ref, wv_ref, k_out, v_out)

    nw = NA_HEADS * HEAD_DIM
    gw = GQA_HEADS * HEAD_DIM
    kw = GQA_KV_HEADS * HEAD_DIM
    na_out[:, :nw] = (na[:, :nw] * (HEAD_DIM ** -0.5 * LOG2E)).astype(BF16)
    na_out[:, nw:] = na[:, nw:].astype(BF16)
    gq_gain = gqn_ref[0] * (HEAD_DIM ** -0.5 * LOG2E)
    for hh in range(GQA_HEADS):
        sl = slice(hh * HEAD_DIM, (hh + 1) * HEAD_DIM)
        gq_out[:, sl] = rot128(_rms(gg[:, sl], gq_gain)).astype(BF16)
    gv = gg[:, gw + kw:]
    gv_out[...] = gv.astype(BF16)
    for hh in range(GQA_KV_HEADS):
        sl = slice(hh * HEAD_DIM, (hh + 1) * HEAD_DIM)
        gk = rot128(_rms(gg[:, gw + hh * HEAD_DIM:gw + (hh + 1) * HEAD_DIM], gkn_ref[0]))
        gk_out[:, sl] = gk.astype(BF16)
        if cache_out:
            gk_c[pl.ds(hh, tm, stride=GQA_KV_HEADS), :] = gk
            gv_c[pl.ds(hh, tm, stride=GQA_KV_HEADS), :] = gv[:, sl]
    if cache_out:
        ckv_c[...] = ckv
        kr_c[...] = kr[:, :MLA_ROPE]
        for hh in range(NA_HEADS):
            sl = slice(hh * HEAD_DIM, (hh + 1) * HEAD_DIM)
            nak_c[pl.ds(hh, tm, stride=NA_HEADS), :] = na[:, nw + hh * HEAD_DIM:nw + (hh + 1) * HEAD_DIM]
            nav_c[pl.ds(hh, tm, stride=NA_HEADS), :] = na[:, 2 * nw + hh * HEAD_DIM:2 * nw + (hh + 1) * HEAD_DIM]


def _cache_shapes(nb, depth, seq):
    return [
        (nb, depth, seq, KV_LORA), (nb, depth, seq, MLA_ROPE),
        (nb, depth, seq * NA_HEADS, HEAD_DIM), (nb, depth, seq * NA_HEADS, HEAD_DIM),
        (nb, depth, seq * GQA_KV_HEADS, HEAD_DIM), (nb, depth, seq * GQA_KV_HEADS, HEAD_DIM),
    ]


def _qkv(x, mods, norm_g, l, stream, w_in, wts, tm, tables=None, cache=None):
    t, d = x.shape
    rope = tables is not None
    qw = MLA_HEADS * MLA_PAD
    vw = MLA_HEADS * HEAD_DIM
    in_specs = [
        pl.BlockSpec((tm, d), lambda i: (i, 0)),
        pl.BlockSpec((1, 1, d), lambda i: (l * 3 + 1, 0, 0)),
        _mod_spec(l, 3, stream, tm),
        _mod_spec(l, 4, stream, tm),
        _resident((None, d, PROJ_W), lambda i: (l, 0, 0)),
        pl.BlockSpec((1, 1, Q_LORA), lambda i: (l, 0, 0)),
        _resident((None, Q_LORA, qw), lambda i: (l, 0, 0)),
        pl.BlockSpec((1, 1, KV_LORA), lambda i: (l, 0, 0)),
        _resident((None, KV_LORA, vw), lambda i: (l, 0, 0)),
        _resident((None, KV_LORA, vw), lambda i: (l, 0, 0)),
        pl.BlockSpec((1, 1, HEAD_DIM), lambda i: (l, 0, 0)),
        pl.BlockSpec((1, 1, HEAD_DIM), lambda i: (l, 0, 0)),
    ]
    args = [x, norm_g, mods, mods, w_in, wts["q_norm"], wts["wqb"], wts["kv_norm"], wts["wk"], wts["wv"],
            wts["gq_norm"], wts["gk_norm"]]
    if rope:
        per = tables[0].shape[0] // tm
        in_specs += [pl.BlockSpec((tm, LANES), lambda i: (i % per, 0))] * 4
        args += list(tables)
    row = lambda w: pl.BlockSpec((tm, w), lambda i: (i, 0))
    widths = [qw, qw, vw, GQA_HEADS * HEAD_DIM, GQA_KV_HEADS * HEAD_DIM, GQA_KV_HEADS * HEAD_DIM,
              3 * NA_HEADS * HEAD_DIM]
    out_specs = [row(w) for w in widths]
    out_shape = [jax.ShapeDtypeStruct((t, w), BF16) for w in widths]
    aliases = {}
    n_prev = 0
    if cache is not None:
        nb, depth, prev = cache
        assert tm * nb == t
        shapes = _cache_shapes(nb, depth, tm)
        if prev is not None:
            n_prev = len(prev)
            aliases = {len(args) + k: N_QKV_OUT + k for k in range(n_prev)}
            in_specs += [pl.BlockSpec(memory_space=pl.ANY)] * n_prev
            args += list(prev)
            out_specs += [pl.BlockSpec((None, None) + s[2:], lambda i: (i, l, 0, 0)) for s in shapes]
        else:
            out_specs += [pl.BlockSpec((None,) + s[1:], lambda i: (i, 0, 0, 0)) for s in shapes]
        out_shape += [jax.ShapeDtypeStruct(s, F32) for s in shapes]
    return pl.pallas_call(
        functools.partial(_qkv_kernel, rope=rope, n_prev=n_prev, cache_out=cache is not None, layer=l),
        grid=(t // tm,),
        in_specs=in_specs,
        out_specs=out_specs,
        out_shape=out_shape,
        input_output_aliases=aliases,
        scratch_shapes=[pltpu.VMEM((tm, d), BF16)],
        compiler_params=_params("parallel"),
        name="qkv",
    )(*args)


def _cache_kv_kernel(ckv_ref, kr_ref, wk_ref, wv_ref, k_out, v_out):
    _expand_kv(ckv_ref[...].astype(BF16), kr_ref[...], wk_ref, wv_ref, k_out, v_out)


def _cache_kv(cache_ckv, cache_kr_pad, l, wts):
    nb, _, past, _ = cache_ckv.shape
    qw = MLA_HEADS * MLA_PAD
    vw = MLA_HEADS * HEAD_DIM
    return pl.pallas_call(
        _cache_kv_kernel,
        grid=(nb,),
        in_specs=[
            pl.BlockSpec((None, None, past, KV_LORA), lambda b: (b, l, 0, 0)),
            pl.BlockSpec((None, None, past, LANES), lambda b: (b, l, 0, 0)),
            pl.BlockSpec((None, KV_LORA, vw), lambda b: (l, 0, 0)),
            pl.BlockSpec((None, KV_LORA, vw), lambda b: (l, 0, 0)),
        ],
        out_specs=[pl.BlockSpec((past, qw), lambda b: (b, 0)), pl.BlockSpec((past, vw), lambda b: (b, 0))],
        out_shape=[jax.ShapeDtypeStruct((nb * past, qw), BF16), jax.ShapeDtypeStruct((nb * past, vw), BF16)],
        compiler_params=_params("parallel"),
        name="cache_kv",
    )(cache_ckv, cache_kr_pad, wts["wk"], wts["wv"])


def _attn_kernel(*refs, n_heads, group, dk, dv, seg_kinds, n_sub):
    q_ref = refs[0]
    pos = 1
    segs = []
    windows = [w for _, _, w in seg_kinds]
    for has_bias, inter, _ in seg_kinds:
        b_ref = refs[pos + 2] if has_bias else None
        segs.append((refs[pos], refs[pos + 1], b_ref, inter))
        pos += 3 if has_bias else 2
    o_ref = refs[pos]

    def head_rows(ref, rows, hk, width, inter):
        if inter:
            return ref[pl.ds(hk, ref.shape[0] // inter, stride=inter), :]
        return ref[rows, hk * width:(hk + 1) * width]

    def attend(q_rows, seg_rows):
        for h in range(n_heads):
            hk = h // group
            q = q_ref[q_rows, h * dk:(h + 1) * dk].astype(BF16)
            scores = []
            for (k_ref, _, b_ref, inter), rows in zip(segs, seg_rows):
                k = head_rows(k_ref, rows, hk, dk, inter).astype(BF16)
                s = lax.dot_general(q, k, (((1,), (1,)), ((), ())), preferred_element_type=F32)
                if b_ref is not None:
                    s = s + b_ref[h]
                scores.append(s)
            m = functools.reduce(jnp.maximum, [s.max(axis=-1, keepdims=True) for s in scores])
            o = None
            for s, (_, v_ref, _, inter), rows in zip(scores, segs, seg_rows):
                p = jnp.exp2(s - m).astype(BF16)
                v = head_rows(v_ref, rows, hk, dv, inter).astype(BF16)
                pv = jnp.dot(p, jnp.concatenate([v, jnp.ones_like(v)], axis=1), preferred_element_type=F32)
                o = pv if o is None else o + pv
            o_ref[q_rows, h * dv:(h + 1) * dv] = (o[:, :dv] / o[:, dv:]).astype(o_ref.dtype)

    if n_sub == 1:
        attend(slice(None), [slice(None) if w is None else
                             pl.ds(pl.multiple_of(pl.program_id(1) * w[0], w[0]), w[1]) for w in windows])
    else:
        def body(s, carry):
            rows = lambda ref: pl.ds(pl.multiple_of(s * (ref.shape[0] // n_sub), ref.shape[0] // n_sub),
                                     ref.shape[0] // n_sub)
            attend(rows(q_ref), [rows(seg[0]) for seg in segs])
            return carry

        lax.fori_loop(0, n_sub, body, 0)


def _attention(q, segs, *, t, grid, tq, q_index, n_heads, group, dk, dv, name, n_sub=1):
    assert n_sub == 1 or all(s[4] is None and not s[6] and len(s) == 7 for s in segs)
    in_specs = [pl.BlockSpec((tq, n_heads * dk), q_index)]
    args = [q]
    seg_kinds = []
    for k_arr, k_spec, v_arr, v_spec, b_arr, b_spec, inter, *window in segs:
        in_specs += [k_spec, v_spec]
        args += [k_arr, v_arr]
        seg_kinds.append((b_arr is not None, inter, window[0] if window else None))
        if b_arr is not None:
            in_specs.append(b_spec)
            args.append(b_arr)
    out_index = lambda *g: (q_index(*g)[0], 0)
    return pl.pallas_call(
        functools.partial(_attn_kernel, n_heads=n_heads, group=group, dk=dk, dv=dv,
                          seg_kinds=tuple(seg_kinds), n_sub=n_sub),
        grid=grid,
        in_specs=in_specs,
        out_specs=pl.BlockSpec((tq, n_heads * dv), out_index),
        out_shape=jax.ShapeDtypeStruct((t, n_heads * dv), BF16),
        compiler_params=_params(*(("parallel",) * len(grid))),
        name=name,
    )(*args)


def _context_attention(q_mla, k_mla, v_mla, gq, gk, gv, na, seq):
    t = na.shape[0]
    n_sub = 4
    rows = seq * n_sub
    grid = (t // rows,)
    blk = lambda w, c: pl.BlockSpec((rows, w), lambda b: (b, c))
    common = dict(t=t, grid=grid, tq=rows, n_sub=n_sub)
    nw = NA_HEADS * HEAD_DIM
    kw = GQA_KV_HEADS * HEAD_DIM
    o_a = _attention(q_mla, [(k_mla, blk(MLA_HEADS * MLA_PAD, 0), v_mla, blk(MLA_HEADS * HEAD_DIM, 0), None, None, 0)],
                     q_index=lambda b: (b, 0), n_heads=MLA_HEADS, group=1, dk=MLA_PAD, dv=HEAD_DIM,
                     name="ctx_mla", **common)
    o_b = _attention(na, [(na, blk(nw, 1), na, blk(nw, 2), None, None, 0)],
                     q_index=lambda b: (b, 0), n_heads=NA_HEADS, group=1, dk=HEAD_DIM, dv=HEAD_DIM,
                     name="ctx_na", **common)
    o_c = _attention(gq, [(gk, blk(kw, 0), gv, blk(kw, 0), None, None, 0)],
                     q_index=lambda b: (b, 0), n_heads=GQA_HEADS, group=GQA_HEADS // GQA_KV_HEADS,
                     dk=HEAD_DIM, dv=HEAD_DIM, name="ctx_gqa", **common)
    return o_a, o_b, o_c


def _latent_attention(q_mla, k_mla, v_mla, gq, gk, gv, na, kc_mla, vc_mla, caches, bias, l, seq):
    cache_na_k, cache_na_v, cache_gqa_k, cache_gqa_v = caches
    t = na.shape[0]
    past = kc_mla.shape[0] // (t // seq)
    own = lambda w, c: pl.BlockSpec((seq, w), lambda b, i: (b, c))
    flat = lambda w: pl.BlockSpec((past, w), lambda b, i: (b, 0))
    cached = lambda heads: pl.BlockSpec((None, None, past * heads, HEAD_DIM), lambda b, i: (b, l, 0, 0))
    qw, vw = MLA_HEADS * MLA_PAD, MLA_HEADS * HEAD_DIM
    nw = NA_HEADS * HEAD_DIM
    kw = GQA_KV_HEADS * HEAD_DIM
    o_c = _attention(gq, [(cache_gqa_k, cached(GQA_KV_HEADS), cache_gqa_v, cached(GQA_KV_HEADS), None, None,
                           GQA_KV_HEADS),
                          (gk, own(kw, 0), gv, own(kw, 0), None, None, 0)],
                     q_index=lambda b, i: (b, 0), n_heads=GQA_HEADS, group=GQA_HEADS // GQA_KV_HEADS,
                     dk=HEAD_DIM, dv=HEAD_DIM, name="lat_gqa",
                     t=t, grid=(t // seq, 1), tq=seq)
    tq = NA_TQ
    nq = seq // tq
    common = dict(t=t, grid=(t // seq, nq), tq=tq)
    o_a = _attention(q_mla, [(kc_mla, flat(qw), vc_mla, flat(vw), None, None, 0),
                             (k_mla, own(qw, 0), v_mla, own(vw, 0), None, None, 0)],
                     q_index=lambda b, i: (b * nq + i, 0), n_heads=MLA_HEADS, group=1, dk=MLA_PAD, dv=HEAD_DIM,
                     name="lat_mla", **common)
    win_step, win_size = _na_key_window(seq, tq)
    window = (win_step * GRID_W, win_size * GRID_W)
    bias_spec = pl.BlockSpec((NA_HEADS, tq, window[1]), lambda b, i: (0, i, 0))
    o_b = _attention(na, [(cache_na_k, cached(NA_HEADS), cache_na_v, cached(NA_HEADS), None, None, NA_HEADS),
                          (na, own(nw, 1), na, own(nw, 2), bias, bias_spec, 0, window)],
                     q_index=lambda b, i: (b * nq + i, 0), n_heads=NA_HEADS, group=1,
                     dk=HEAD_DIM, dv=HEAD_DIM, name="lat_na", **common)
    return o_a, o_b, o_c


def _outproj_kernel(x_ref, gt_ref, oa_ref, ob_ref, oc_ref, w_ref, o_ref):
    wa = oa_ref.shape[1]
    wb = ob_ref.shape[1]
    acc = jnp.dot(oa_ref[...], w_ref[0:wa, :], preferred_element_type=F32)
    acc += jnp.dot(ob_ref[...], w_ref[wa:wa + wb, :], preferred_element_type=F32)
    acc += jnp.dot(oc_ref[...], w_ref[wa + wb:, :], preferred_element_type=F32)
    o_ref[...] = x_ref[...] + gt_ref[0] * acc


def _outproj(x, mods, l, stream, o_a, o_b, o_c, w_out):
    t, d = x.shape
    tm = 512
    row = lambda a: pl.BlockSpec((tm, a.shape[1]), lambda i: (i, 0))
    return pl.pallas_call(
        _outproj_kernel,
        grid=(t // tm,),
        in_specs=[
            row(x),
            _mod_spec(l, 5, stream, tm),
            row(o_a), row(o_b), row(o_c),
            _resident((None, d, d), lambda i: (l, 0, 0)),
        ],
        out_specs=row(x),
        out_shape=jax.ShapeDtypeStruct((t, d), F32),
        compiler_params=_params("parallel"),
        name="outproj",
    )(x, mods, o_a, o_b, o_c, w_out)


def _project_layout_kernel(w_ref, o_ref):
    split = COL_KR + MLA_ROPE
    o_ref[:, :split] = w_ref[:, :split].astype(BF16)
    o_ref[:, split:COL_NA] = jnp.zeros((o_ref.shape[0], COL_NA - split), BF16)
    o_ref[:, COL_NA:] = w_ref[:, split:].astype(BF16)


def _project_layout(w_in):
    depth, d, n = w_in.shape
    tk = 256
    return pl.pallas_call(
        _project_layout_kernel,
        grid=(depth, d // tk),
        in_specs=[pl.BlockSpec((None, tk, n), lambda l, i: (l, i, 0))],
        out_specs=pl.BlockSpec((None, tk, PROJ_W), lambda l, i: (l, i, 0)),
        out_shape=jax.ShapeDtypeStruct((depth, d, PROJ_W), BF16),
        compiler_params=_params("parallel", "parallel"),
        name="project_layout",
    )(w_in)


def _rope_tables(seq, d):
    quarter = d // 4
    tt = np.arange(seq)
    pos = np.stack([tt // GRID_W, tt % GRID_W], axis=-1).astype(np.float64)
    inv = ROPE_THETA ** (-np.arange(quarter, dtype=np.float64) / quarter)
    ang = pos[:, :, None] * inv
    cos = np.cos(ang)
    sin = np.sin(ang)
    cos_t = np.stack([cos, cos], axis=2).reshape(seq, d)
    sin_t = np.stack([-sin, sin], axis=2).reshape(seq, d)
    pad = LANES - d
    if pad:
        cos_t = np.concatenate([cos_t, np.ones((seq, pad))], axis=-1)
        sin_t = np.concatenate([sin_t, np.zeros((seq, pad))], axis=-1)
    return jnp.asarray(cos_t, F32), jnp.asarray(sin_t, F32)


def _na_key_window(seq, tq):
    rows = seq // GRID_W
    wh = min(NA_WIN_H, rows)
    qr = tq // GRID_W
    first = lambda r: min(max(r - wh // 2, 0), rows - wh)
    spans = [(first(i * qr), first(i * qr + qr - 1) + wh) for i in range(rows // qr)]
    size = max(hi - lo for lo, hi in spans)
    size += size % 2
    starts = [min(lo, rows - size) for lo, _ in spans]
    step = starts[1] - starts[0] if len(starts) > 1 else 0
    assert all(s == i * step and s <= lo and hi <= s + size for i, (s, (lo, hi)) in enumerate(zip(starts, spans)))
    return step, size


def _na_bias_kernel(t_ref, o_ref, *, rows, wh, qr, win_step, win_size):
    tile = pl.program_id(1)
    k0 = tile * win_step
    shape = (GRID_W, 2 * GRID_W)
    c = lax.broadcasted_iota(jnp.int32, shape, 0)
    lane = lax.broadcasted_iota(jnp.int32, shape, 1)
    kc = lane % GRID_W
    cs = jnp.clip(c - NA_WIN_W // 2, 0, GRID_W - NA_WIN_W)
    col_ok = (kc >= cs) & (kc < cs + NA_WIN_W)

    def one_row(rl, carry):
        r = tile * qr + rl
        rs = jnp.clip(r - wh // 2, 0, rows - wh)
        q_rows = pl.ds(pl.multiple_of(rl * GRID_W, GRID_W), GRID_W)
        for j in range(win_size // 2):
            kr = k0 + 2 * j + lane // GRID_W
            ok = col_ok & (kr >= rs) & (kr < rs + wh)
            d = jnp.clip(k0 + 2 * j - r + NA_WIN_H, 0, 2 * NA_WIN_H - 1)
            pair = jnp.broadcast_to(t_ref[0, pl.ds(d, 1), :], shape)
            toeplitz = pltpu.roll(pair, 2 * GRID_W - (NA_WIN_W - 1), 1, stride=1, stride_axis=0)
            o_ref[0, q_rows, j * 2 * GRID_W:(j + 1) * 2 * GRID_W] = jnp.where(ok, toeplitz * LOG2E, NEG_INF)
        return carry

    lax.fori_loop(0, qr, one_row, 0)


def _na_bias(rpb, seq, tq):
    nh, nr, nc = rpb.shape
    rows = seq // GRID_W
    wh = min(NA_WIN_H, rows)
    win_step, win_size = _na_key_window(seq, tq)
    padded = jnp.pad(rpb.astype(F32), ((0, 0), (1, 1), (0, GRID_W - nc)))
    table = jnp.concatenate([padded[:, :nr + 1], padded[:, 1:]], axis=-1)
    return pl.pallas_call(
        functools.partial(_na_bias_kernel, rows=rows, wh=wh, qr=tq // GRID_W, win_step=win_step,
                          win_size=win_size),
        grid=(nh, seq // tq),
        in_specs=[pl.BlockSpec((1, nr + 1, 2 * GRID_W), lambda h, i: (h, 0, 0))],
        out_specs=pl.BlockSpec((1, tq, win_size * GRID_W), lambda h, i: (h, i, 0)),
        out_shape=jax.ShapeDtypeStruct((nh, seq, win_size * GRID_W), F32),
        compiler_params=_params("parallel", "parallel"),
        name="na_bias",
    )(table)


def kernel(x_prompt, x_sample, cache_mla_ckv, cache_mla_krope, cache_na_k, cache_na_v, cache_gqa_k, cache_gqa_v, c, c_ctx, ada_w, ada_b, norm_g, ffn_wg, ffn_wu, ffn_wd, w_in, mla_q_norm, mla_wqb, mla_kv_norm, mla_wkvb, na_rpb, gqa_q_norm, gqa_k_norm, w_out, final_norm):
    depth = ada_w.shape[0]
    nb_ctx, seq_ctx, d = x_prompt.shape
    nb_lat, seq_lat, _ = x_sample.shape
    past = cache_mla_ckv.shape[2]

    wg = ffn_wg
    wu = ffn_wu
    wd = ffn_wd
    w_in_p = _project_layout(w_in)
    wqb =jnp.pad(mla_wqb.reshape(depth, Q_LORA, MLA_HEADS, MLA_QK),
                  ((0, 0), (0, 0), (0, 0), (0, MLA_PAD - MLA_QK))).reshape(depth, Q_LORA, MLA_HEADS * MLA_PAD)
    wkv = mla_wkvb.reshape(depth, KV_LORA, MLA_HEADS, 2 * HEAD_DIM)
    wts = {
        "q_norm": mla_q_norm.reshape(depth, 1, Q_LORA),
        "wqb": wqb.astype(BF16),
        "kv_norm": mla_kv_norm.reshape(depth, 1, KV_LORA),
        "wk": wkv[..., :MLA_NOPE].reshape(depth, KV_LORA, MLA_HEADS * MLA_NOPE).astype(BF16),
        "wv": wkv[..., MLA_NOPE:].reshape(depth, KV_LORA, MLA_HEADS * HEAD_DIM).astype(BF16),
        "gq_norm": gqa_q_norm.reshape(depth, 1, HEAD_DIM),
        "gk_norm": gqa_k_norm.reshape(depth, 1, HEAD_DIM),
    }
    w_out_b = w_out.astype(BF16)
    norm_g3 = norm_g.reshape(depth * 3, 1, d)
    final_g = final_norm.reshape(1, 1, d)
    tables = _rope_tables(seq_lat, MLA_ROPE) + _rope_tables(seq_lat, HEAD_DIM)
    kr_cache = jnp.pad(cache_mla_krope, ((0, 0), (0, 0), (0, 0), (0, LANES - MLA_ROPE)))
    caches = (cache_na_k.reshape(nb_lat, depth, past * NA_HEADS, HEAD_DIM),
              cache_na_v.reshape(nb_lat, depth, past * NA_HEADS, HEAD_DIM),
              cache_gqa_k.reshape(nb_lat, depth, past * GQA_KV_HEADS, HEAD_DIM),
              cache_gqa_v.reshape(nb_lat, depth, past * GQA_KV_HEADS, HEAD_DIM))

    cond = jnp.concatenate([c_ctx[None, :], c, jnp.zeros((MOD_ROWS - 1 - nb_lat, d), c.dtype)], axis=0)
    mods = _modulation(cond, ada_w, ada_b).reshape(depth * MOD_ROWS, 1, N_MOD * d)

    xp = x_prompt.reshape(nb_ctx * seq_ctx, d)
    xs = x_sample.reshape(nb_lat * seq_lat, d)
    ctx_row = (0, nb_ctx * seq_ctx)
    lat_row = (1, seq_lat)
    biases = [_na_bias(na_rpb[l], seq_lat, NA_TQ) for l in range(depth)]

    new_cache = None
    for l in range(depth):
        last = l == depth - 1
        xp = _ffn(xp, mods, norm_g3, l, 0, 0, ctx_row, wg, wu, wd)
        outs = _qkv(xp, mods, norm_g3, l, ctx_row, w_in_p, wts, seq_ctx, cache=(nb_ctx, depth, new_cache))
        q_mla, k_mla, v_mla, gq, gk, gv, na = outs[:N_QKV_OUT]
        new_cache = outs[N_QKV_OUT:]
        o_a, o_b, o_c = _context_attention(q_mla, k_mla, v_mla, gq, gk, gv, na, seq_ctx)
        xp = _outproj(xp, mods, l, ctx_row, o_a, o_b, o_c, w_out_b)
        xp = _ffn(xp, mods, norm_g3, l, 2, 1, ctx_row, wg, wu, wd, final_g if last else None)
        xs = _ffn(xs, mods, norm_g3, l, 0, 0, lat_row, wg, wu, wd)
        q_mla, k_mla, v_mla, gq, gk, gv, na = _qkv(xs, mods, norm_g3, l, lat_row, w_in_p, wts, 256, tables=tables)
        kc_mla, vc_mla = _cache_kv(cache_mla_ckv, kr_cache, l, wts)
        o_a, o_b, o_c = _latent_attention(q_mla, k_mla, v_mla, gq, gk, gv, na, kc_mla, vc_mla, caches, biases[l], l,
                                          seq_lat)
        xs = _outproj(xs, mods, l, lat_row, o_a, o_b, o_c, w_out_b)
        xs = _ffn(xs, mods, norm_g3, l, 2, 1, lat_row, wg, wu, wd, final_g if last else None)

    y_prompt = xp.reshape(nb_ctx, seq_ctx, d)
    y_sample = xs.reshape(nb_lat, seq_lat, d)
    ckv_c, kr_c, nak_c, nav_c, gk_c, gv_c = new_cache
    na_shape = (nb_ctx, depth, seq_ctx, NA_HEADS, HEAD_DIM)
    gqa_shape = (nb_ctx, depth, seq_ctx, GQA_KV_HEADS, HEAD_DIM)
    return (y_prompt, y_sample, ckv_c, kr_c, nak_c.reshape(na_shape), nav_c.reshape(na_shape),
            gk_c.reshape(gqa_shape), gv_c.reshape(gqa_shape))
```

```python
import functools

import jax
import jax.numpy as jnp
import numpy as np
from jax import lax
from jax.experimental import pallas as pl
from jax.experimental.pallas import tpu as pltpu

F32 = jnp.float32
BF16 = jnp.bfloat16

D_MODEL = 2048
D_FF = 5632
N_MOD = 9
GRID_W = 64
ROPE_THETA = 10000.0
HEAD_DIM = 128
MLA_HEADS = 8
MLA_NOPE = 128
MLA_ROPE = 64
MLA_QK = MLA_NOPE + MLA_ROPE
MLA_PAD = 256
Q_LORA = 512
KV_LORA = 256
NA_HEADS = 4
NA_WIN_H = 8
NA_WIN_W = 16
NA_TQ = 512
GQA_HEADS = 4
GQA_KV_HEADS = 2
EPS = 1e-6
NEG_INF = -1e30
LOG2E = 1.4426950408889634
MOD_ROWS = 16

LANES = 128
COL_CQ = 0
COL_CKV = COL_CQ + Q_LORA
COL_KR = COL_CKV + KV_LORA
COL_NA = COL_KR + LANES
COL_GQ = COL_NA + 3 * NA_HEADS * HEAD_DIM
COL_GK = COL_GQ + GQA_HEADS * HEAD_DIM
COL_GV = COL_GK + GQA_KV_HEADS * HEAD_DIM
PROJ_W = COL_GV + GQA_KV_HEADS * HEAD_DIM

VMEM_LIMIT = 62 * 1024 * 1024


def _params(*sem):
    return pltpu.CompilerParams(dimension_semantics=sem, vmem_limit_bytes=VMEM_LIMIT)


def _resident(shape, index_map):
    return pl.BlockSpec(shape, index_map, pipeline_mode=pl.Buffered(1))


def _as_bf16(ref):
    v = ref[...]
    return v if v.dtype == BF16 else v.astype(BF16)


def _rms(x, g):
    return x * lax.rsqrt(jnp.mean(x * x, axis=-1, keepdims=True) + EPS) * g


NORM_ROWS = 256


def _norm_modulate_store(x_ref, ng_ref, sc_ref, sh_ref, h_scr):
    gain = ng_ref[0] * (1.0 + sc_ref[0])
    shift = sh_ref[0]

    def body(i, carry):
        rows = pl.ds(pl.multiple_of(i * NORM_ROWS, NORM_ROWS), NORM_ROWS)
        x = x_ref[rows, :]
        r = lax.rsqrt(jnp.mean(x * x, axis=-1, keepdims=True) + EPS)
        h_scr[rows, :] = (x * r * gain + shift).astype(BF16)
        return carry

    lax.fori_loop(0, x_ref.shape[0] // NORM_ROWS, body, 0)


def _rope(x, cos, sin, quarter):
    n = x.shape[-1]
    lane = lax.broadcasted_iota(jnp.int32, x.shape, 1)
    first = (lane % (2 * quarter)) < quarter
    sw = jnp.where(first, pltpu.roll(x, n - quarter, 1), pltpu.roll(x, quarter, 1))
    return x * cos + sw * sin


def _mod_kernel(c_ref, w_ref, b_ref, o_ref):
    c = c_ref[...]
    s = (c * jax.nn.sigmoid(c)).astype(BF16)
    o_ref[0] = jnp.dot(s, w_ref[0].astype(BF16), preferred_element_type=F32) + b_ref[0]


def _modulation(cond, ada_w, ada_b):
    depth, d, n = ada_w.shape
    tn = 1024
    return pl.pallas_call(
        _mod_kernel,
        grid=(depth, n // tn),
        in_specs=[
            pl.BlockSpec((MOD_ROWS, d), lambda l, j: (0, 0)),
            pl.BlockSpec((1, d, tn), lambda l, j: (l, 0, j)),
            pl.BlockSpec((1, 1, tn), lambda l, j: (l, 0, j)),
        ],
        out_specs=pl.BlockSpec((1, MOD_ROWS, tn), lambda l, j: (l, 0, j)),
        out_shape=jax.ShapeDtypeStruct((depth, MOD_ROWS, n), F32),
        compiler_params=_params("parallel", "parallel"),
        name="modulation",
    )(cond, ada_w, ada_b.reshape(depth, 1, n))


def _mod_spec(l, chunk, stream, tm):
    base, tokens = stream
    return pl.BlockSpec((1, 1, D_MODEL), lambda i, *_: (l * MOD_ROWS + base + (i * tm) // tokens, 0, chunk))


def _ffn_kernel(x_ref, ng_ref, sh_ref, sc_ref, gt_ref, wg_ref, wu_ref, wd_ref, *rest, final):
    if final:
        fg_ref, o_ref, h_scr = rest
    else:
        o_ref, h_scr = rest
    f = pl.program_id(1)
    tm = x_ref.shape[0]

    def swiglu(rows, wg, wu, wd):
        h = h_scr[rows, :]
        g = jnp.dot(h, wg, preferred_element_type=F32)
        u = jnp.dot(h, wu, preferred_element_type=F32)
        a = (g * jax.nn.sigmoid(g) * u).astype(BF16)
        return jnp.dot(a, wd, preferred_element_type=F32)

    @pl.when(f == 0)
    def _():
        wg, wu, wd = _as_bf16(wg_ref), _as_bf16(wu_ref), _as_bf16(wd_ref)
        gain = ng_ref[0] * (1.0 + sc_ref[0])
        shift = sh_ref[0]
        for half in range(2):
            for r0 in range(half * tm // 2, (half + 1) * tm // 2, NORM_ROWS):
                x = x_ref[r0:r0 + NORM_ROWS, :]
                r = lax.rsqrt(jnp.mean(x * x, axis=-1, keepdims=True) + EPS)
                h_scr[r0:r0 + NORM_ROWS, :] = (x * r * gain + shift).astype(BF16)
            rows = slice(half * tm // 2, (half + 1) * tm // 2)
            o_ref[rows, :] = swiglu(rows, wg, wu, wd)

    last = pl.num_programs(1) - 1

    @pl.when((f > 0) & ((f < last) | final))
    def _():
        o_ref[...] += swiglu(slice(None), _as_bf16(wg_ref), _as_bf16(wu_ref), _as_bf16(wd_ref))

    @pl.when(f == last)
    def _():
        gate = 0.5 * gt_ref[0]
        if final:
            def body(i, carry):
                rows = pl.ds(pl.multiple_of(i * NORM_ROWS, NORM_ROWS), NORM_ROWS)
                o_ref[rows, :] = _rms(x_ref[rows, :] + gate * o_ref[rows, :], fg_ref[0])
                return carry

            lax.fori_loop(0, tm // NORM_ROWS, body, 0)
        else:
            wg, wu, wd = _as_bf16(wg_ref), _as_bf16(wu_ref), _as_bf16(wd_ref)
            for half in range(2):
                rows = slice(half * tm // 2, (half + 1) * tm // 2)
                o_ref[rows, :] += swiglu(rows, wg, wu, wd)
                for r0 in range(half * tm // 2, (half + 1) * tm // 2, NORM_ROWS):
                    o_ref[r0:r0 + NORM_ROWS, :] = x_ref[r0:r0 + NORM_ROWS, :] + gate * o_ref[r0:r0 + NORM_ROWS, :]


def _ffn(x, mods, norm_g, l, sub, k, stream, wg, wu, wd, final_g=None):
    t, d = x.shape
    tm, tf = 1024, 512
    chunk = 3 * sub
    in_specs = [
        pl.BlockSpec((tm, d), lambda i, j: (i, 0), pipeline_mode=pl.Buffered(1)),
        pl.BlockSpec((1, 1, d), lambda i, j: (l * 3 + sub, 0, 0)),
        _mod_spec(l, chunk, stream, tm),
        _mod_spec(l, chunk + 1, stream, tm),
        _mod_spec(l, chunk + 2, stream, tm),
        pl.BlockSpec((None, None, d, tf), lambda i, j: (l, k, 0, j)),
        pl.BlockSpec((None, None, d, tf), lambda i, j: (l, k, 0, j)),
        pl.BlockSpec((None, None, tf, d), lambda i, j: (l, k, j, 0)),
    ]
    args = [x, norm_g, mods, mods, mods, wg, wu, wd]
    if final_g is not None:
        in_specs.append(pl.BlockSpec((1, 1, d), lambda i, j: (0, 0, 0)))
        args.append(final_g)
    return pl.pallas_call(
        functools.partial(_ffn_kernel, final=final_g is not None),
        grid=(t // tm, D_FF // tf),
        in_specs=in_specs,
        out_specs=pl.BlockSpec((tm, d), lambda i, j: (i, 0)),
        out_shape=jax.ShapeDtypeStruct((t, d), F32),
        scratch_shapes=[pltpu.VMEM((tm, d), BF16)],
        compiler_params=_params("parallel", "arbitrary"),
        name="ffn",
    )(*args)


def _expand_kv(ckv_bf, kr, wk_ref, wv_ref, k_out, v_out):
    kn = jnp.dot(ckv_bf, wk_ref[...], preferred_element_type=F32)
    v_out[...] = jnp.dot(ckv_bf, wv_ref[...], preferred_element_type=F32).astype(BF16)
    kr_bf = kr.astype(BF16)
    for h in range(MLA_HEADS):
        k_out[:, h * MLA_PAD:h * MLA_PAD + MLA_NOPE] = kn[:, h * MLA_NOPE:(h + 1) * MLA_NOPE].astype(BF16)
        k_out[:, h * MLA_PAD + MLA_NOPE:(h + 1) * MLA_PAD] = kr_bf


N_QKV_OUT = 7
N_CACHE_OUT = 6


def _qkv_kernel(x_ref, ng_ref, sh_ref, sc_ref, w_ref, qn_ref, wqb_ref, kvn_ref, wk_ref, wv_ref, gqn_ref, gkn_ref,
                *rest, rope, n_prev, cache_out, layer):
    if rope:
        c64_ref, s64_ref, c128_ref, s128_ref = rest[:4]
        rest = rest[4:]
        rot64 = lambda v: _rope(v, c64_ref[...], s64_ref[...], MLA_ROPE // 4)
        rot128 = lambda v: _rope(v, c128_ref[...], s128_ref[...], HEAD_DIM // 4)
    else:
        rot64 = rot128 = lambda v: v
    rest = rest[n_prev:]
    q_out, k_out, v_out, gq_out, gk_out, gv_out, na_out = rest[:N_QKV_OUT]
    rest = rest[N_QKV_OUT:]
    if cache_out:
        cache_refs = rest[:N_CACHE_OUT]
        rest = rest[N_CACHE_OUT:]
        if n_prev == 0:
            for ref in cache_refs:
                for other in range(ref.shape[0]):
                    if other != layer:
                        ref[other] = jnp.zeros(ref.shape[1:], ref.dtype)
            cache_refs = [ref.at[layer] for ref in cache_refs]
        ckv_c, kr_c, nak_c, nav_c, gk_c, gv_c = cache_refs
    (h_scr,) = rest
    tm = x_ref.shape[0]

    _norm_modulate_store(x_ref, ng_ref, sc_ref, sh_ref, h_scr)
    h = h_scr[...]
    low = jnp.dot(h, w_ref[:, COL_CQ:COL_NA], preferred_element_type=F32)
    na = jnp.dot(h, w_ref[:, COL_NA:COL_GQ], preferred_element_type=F32)
    gg = jnp.dot(h, w_ref[:, COL_GQ:PROJ_W], preferred_element_type=F32)

    qn = _rms(low[:, COL_CQ:COL_CKV], qn_ref[0]).astype(BF16)
    q = jnp.dot(qn, wqb_ref[...], preferred_element_type=F32) * (MLA_QK ** -0.5 * LOG2E)
    for hh in range(MLA_HEADS):
        lo = hh * MLA_PAD
        q_out[:, lo:lo + MLA_NOPE] = q[:, lo:lo + MLA_NOPE].astype(BF16)
        q_out[:, lo + MLA_NOPE:lo + MLA_PAD] = rot64(q[:, lo + MLA_NOPE:lo + MLA_PAD]).astype(BF16)

    ckv = _rms(low[:, COL_CKV:COL_KR], kvn_ref[0])
    kr = rot64(low[:, COL_KR:COL_NA])
    _expand_kv(ckv.astype(BF16), kr, wk_ref, wv_ref, k_out, v_out)

    nw = NA_HEADS * HEAD_DIM
    gw = GQA_HEADS * HEAD_DIM
    kw = GQA_KV_HEADS * HEAD_DIM
    na_out[:, :nw] = (na[:, :nw] * (HEAD_DIM ** -0.5 * LOG2E)).astype(BF16)
    na_out[:, nw:] = na[:, nw:].astype(BF16)
    gq_gain = gqn_ref[0] * (HEAD_DIM ** -0.5 * LOG2E)
    for hh in range(GQA_HEADS):
        sl = slice(hh * HEAD_DIM, (hh + 1) * HEAD_DIM)
        gq_out[:, sl] = rot128(_rms(gg[:, sl], gq_gain)).astype(BF16)
    gv = gg[:, gw + kw:]
    gv_out[...] = gv.astype(BF16)
    for hh in range(GQA_KV_HEADS):
        sl = slice(hh * HEAD_DIM, (hh + 1) * HEAD_DIM)
        gk = rot128(_rms(gg[:, gw + hh * HEAD_DIM:gw + (hh + 1) * HEAD_DIM], gkn_ref[0]))
        gk_out[:, sl] = gk.astype(BF16)
        if cache_out:
            gk_c[pl.ds(hh, tm, stride=GQA_KV_HEADS), :] = gk
            gv_c[pl.ds(hh, tm, stride=GQA_KV_HEADS), :] = gv[:, sl]
    if cache_out:
        ckv_c[...] = ckv
        kr_c[...] = kr[:, :MLA_ROPE]
        for hh in range(NA_HEADS):
            sl = slice(hh * HEAD_DIM, (hh + 1) * HEAD_DIM)
            nak_c[pl.ds(hh, tm, stride=NA_HEADS), :] = na[:, nw + hh * HEAD_DIM:nw + (hh + 1) * HEAD_DIM]
            nav_c[pl.ds(hh, tm, stride=NA_HEADS), :] = na[:, 2 * nw + hh * HEAD_DIM:2 * nw + (hh + 1) * HEAD_DIM]


def _cache_shapes(nb, depth, seq):
    return [
        (nb, depth, seq, KV_LORA), (nb, depth, seq, MLA_ROPE),
        (nb, depth, seq * NA_HEADS, HEAD_DIM), (nb, depth, seq * NA_HEADS, HEAD_DIM),
        (nb, depth, seq * GQA_KV_HEADS, HEAD_DIM), (nb, depth, seq * GQA_KV_HEADS, HEAD_DIM),
    ]


def _qkv(x, mods, norm_g, l, stream, w_in, wts, tm, tables=None, cache=None):
    t, d = x.shape
    rope = tables is not None
    qw = MLA_HEADS * MLA_PAD
    vw = MLA_HEADS * HEAD_DIM
    in_specs = [
        pl.BlockSpec((tm, d), lambda i: (i, 0)),
        pl.BlockSpec((1, 1, d), lambda i: (l * 3 + 1, 0, 0)),
        _mod_spec(l, 3, stream, tm),
        _mod_spec(l, 4, stream, tm),
        _resident((None, d, PROJ_W), lambda i: (l, 0, 0)),
        pl.BlockSpec((1, 1, Q_LORA), lambda i: (l, 0, 0)),
        _resident((None, Q_LORA, qw), lambda i: (l, 0, 0)),
        pl.BlockSpec((1, 1, KV_LORA), lambda i: (l, 0, 0)),
        _resident((None, KV_LORA, vw), lambda i: (l, 0, 0)),
        _resident((None, KV_LORA, vw), lambda i: (l, 0, 0)),
        pl.BlockSpec((1, 1, HEAD_DIM), lambda i: (l, 0, 0)),
        pl.BlockSpec((1, 1, HEAD_DIM), lambda i: (l, 0, 0)),
    ]
    args = [x, norm_g, mods, mods, w_in, wts["q_norm"], wts["wqb"], wts["kv_norm"], wts["wk"], wts["wv"],
            wts["gq_norm"], wts["gk_norm"]]
    if rope:
        per = tables[0].shape[0] // tm
        in_specs += [pl.BlockSpec((tm, LANES), lambda i: (i % per, 0))] * 4
        args += list(tables)
    row = lambda w: pl.BlockSpec((tm, w), lambda i: (i, 0))
    widths = [qw, qw, vw, GQA_HEADS * HEAD_DIM, GQA_KV_HEADS * HEAD_DIM, GQA_KV_HEADS * HEAD_DIM,
              3 * NA_HEADS * HEAD_DIM]
    out_specs = [row(w) for w in widths]
    out_shape = [jax.ShapeDtypeStruct((t, w), BF16) for w in widths]
    aliases = {}
    n_prev = 0
    if cache is not None:
        nb, depth, prev = cache
        assert tm * nb == t
        shapes = _cache_shapes(nb, depth, tm)
        if prev is not None:
            n_prev = len(prev)
            aliases = {len(args) + k: N_QKV_OUT + k for k in range(n_prev)}
            in_specs += [pl.BlockSpec(memory_space=pl.ANY)] * n_prev
            args += list(prev)
            out_specs += [pl.BlockSpec((None, None) + s[2:], lambda i: (i, l, 0, 0)) for s in shapes]
        else:
            out_specs += [pl.BlockSpec((None,) + s[1:], lambda i: (i, 0, 0, 0)) for s in shapes]
        out_shape += [jax.ShapeDtypeStruct(s, F32) for s in shapes]
    return pl.pallas_call(
        functools.partial(_qkv_kernel, rope=rope, n_prev=n_prev, cache_out=cache is not None, layer=l),
        grid=(t // tm,),
        in_specs=in_specs,
        out_specs=out_specs,
        out_shape=out_shape,
        input_output_aliases=aliases,
        scratch_shapes=[pltpu.VMEM((tm, d), BF16)],
        compiler_params=_params("parallel"),
        name="qkv",
    )(*args)


def _cache_kv_kernel(ckv_ref, kr_ref, wk_ref, wv_ref, k_out, v_out):
    _expand_kv(ckv_ref[...].astype(BF16), kr_ref[...], wk_ref, wv_ref, k_out, v_out)


def _cache_kv(cache_ckv, cache_kr_pad, l, wts):
    nb, _, past, _ = cache_ckv.shape
    qw = MLA_HEADS * MLA_PAD
    vw = MLA_HEADS * HEAD_DIM
    return pl.pallas_call(
        _cache_kv_kernel,
        grid=(nb,),
        in_specs=[
            pl.BlockSpec((None, None, past, KV_LORA), lambda b: (b, l, 0, 0)),
            pl.BlockSpec((None, None, past, LANES), lambda b: (b, l, 0, 0)),
            pl.BlockSpec((None, KV_LORA, vw), lambda b: (l, 0, 0)),
            pl.BlockSpec((None, KV_LORA, vw), lambda b: (l, 0, 0)),
        ],
        out_specs=[pl.BlockSpec((past, qw), lambda b: (b, 0)), pl.BlockSpec((past, vw), lambda b: (b, 0))],
        out_shape=[jax.ShapeDtypeStruct((nb * past, qw), BF16), jax.ShapeDtypeStruct((nb * past, vw), BF16)],
        compiler_params=_params("parallel"),
        name="cache_kv",
    )(cache_ckv, cache_kr_pad, wts["wk"], wts["wv"])


def _attn_kernel(*refs, n_heads, group, dk, dv, seg_kinds, n_sub):
    q_ref = refs[0]
    pos = 1
    segs = []
    windows = [w for _, _, w in seg_kinds]
    for has_bias, inter, _ in seg_kinds:
        b_ref = refs[pos + 2] if has_bias else None
        segs.append((refs[pos], refs[pos + 1], b_ref, inter))
        pos += 3 if has_bias else 2
    o_ref = refs[pos]

    def head_rows(ref, rows, hk, width, inter):
        if inter:
            return ref[pl.ds(hk, ref.shape[0] // inter, stride=inter), :]
        return ref[rows, hk * width:(hk + 1) * width]

    def attend(q_rows, seg_rows):
        for h in range(n_heads):
            hk = h // group
            q = q_ref[q_rows, h * dk:(h + 1) * dk].astype(BF16)
            scores = []
            for (k_ref, _, b_ref, inter), rows in zip(segs, seg_rows):
                k = head_rows(k_ref, rows, hk, dk, inter).astype(BF16)
                s = lax.dot_general(q, k, (((1,), (1,)), ((), ())), preferred_element_type=F32)
                if b_ref is not None:
                    s = s + b_ref[h]
                scores.append(s)
            m = functools.reduce(jnp.maximum, [s.max(axis=-1, keepdims=True) for s in scores])
            o = None
            for s, (_, v_ref, _, inter), rows in zip(scores, segs, seg_rows):
                p = jnp.exp2(s - m).astype(BF16)
                v = head_rows(v_ref, rows, hk, dv, inter).astype(BF16)
                pv = jnp.dot(p, jnp.concatenate([v, jnp.ones_like(v)], axis=1), preferred_element_type=F32)
                o = pv if o is None else o + pv
            o_ref[q_rows, h * dv:(h + 1) * dv] = (o[:, :dv] / o[:, dv:]).astype(o_ref.dtype)

    if n_sub == 1:
        attend(slice(None), [slice(None) if w is None else
                             pl.ds(pl.multiple_of(pl.program_id(1) * w[0], w[0]), w[1]) for w in windows])
    else:
        def body(s, carry):
            rows = lambda ref: pl.ds(pl.multiple_of(s * (ref.shape[0] // n_sub), ref.shape[0] // n_sub),
                                     ref.shape[0] // n_sub)
            attend(rows(q_ref), [rows(seg[0]) for seg in segs])
            return carry

        lax.fori_loop(0, n_sub, body, 0, unroll=True)


def _attention(q, segs, *, t, grid, tq, q_index, n_heads, group, dk, dv, name, n_sub=1):
    assert n_sub == 1 or all(s[4] is None and not s[6] and len(s) == 7 for s in segs)
    in_specs = [pl.BlockSpec((tq, n_heads * dk), q_index)]
    args = [q]
    seg_kinds = []
    for k_arr, k_spec, v_arr, v_spec, b_arr, b_spec, inter, *window in segs:
        in_specs += [k_spec, v_spec]
        args += [k_arr, v_arr]
        seg_kinds.append((b_arr is not None, inter, window[0] if window else None))
        if b_arr is not None:
            in_specs.append(b_spec)
            args.append(b_arr)
    out_index = lambda *g: (q_index(*g)[0], 0)
    return pl.pallas_call(
        functools.partial(_attn_kernel, n_heads=n_heads, group=group, dk=dk, dv=dv,
                          seg_kinds=tuple(seg_kinds), n_sub=n_sub),
        grid=grid,
        in_specs=in_specs,
        out_specs=pl.BlockSpec((tq, n_heads * dv), out_index),
        out_shape=jax.ShapeDtypeStruct((t, n_heads * dv), BF16),
        compiler_params=_params(*(("parallel",) * len(grid))),
        name=name,
    )(*args)


def _context_attention(q_mla, k_mla, v_mla, gq, gk, gv, na, seq):
    t = na.shape[0]
    n_sub = 4
    rows = seq * n_sub
    grid = (t // rows,)
    blk = lambda w, c: pl.BlockSpec((rows, w), lambda b: (b, c))
    common = dict(t=t, grid=grid, tq=rows, n_sub=n_sub)
    nw = NA_HEADS * HEAD_DIM
    kw = GQA_KV_HEADS * HEAD_DIM
    o_a = _attention(q_mla, [(k_mla, blk(MLA_HEADS * MLA_PAD, 0), v_mla, blk(MLA_HEADS * HEAD_DIM, 0), None, None, 0)],
                     q_index=lambda b: (b, 0), n_heads=MLA_HEADS, group=1, dk=MLA_PAD, dv=HEAD_DIM,
                     name="ctx_mla", **common)
    o_b = _attention(na, [(na, blk(nw, 1), na, blk(nw, 2), None, None, 0)],
                     q_index=lambda b: (b, 0), n_heads=NA_HEADS, group=1, dk=HEAD_DIM, dv=HEAD_DIM,
                     name="ctx_na", **common)
    o_c = _attention(gq, [(gk, blk(kw, 0), gv, blk(kw, 0), None, None, 0)],
                     q_index=lambda b: (b, 0), n_heads=GQA_HEADS, group=GQA_HEADS // GQA_KV_HEADS,
                     dk=HEAD_DIM, dv=HEAD_DIM, name="ctx_gqa", **common)
    return o_a, o_b, o_c


def _latent_attention(q_mla, k_mla, v_mla, gq, gk, gv, na, kc_mla, vc_mla, caches, bias, l, seq):
    cache_na_k, cache_na_v, cache_gqa_k, cache_gqa_v = caches
    t = na.shape[0]
    past = kc_mla.shape[0] // (t // seq)
    own = lambda w, c: pl.BlockSpec((seq, w), lambda b, i: (b, c))
    flat = lambda w: pl.BlockSpec((past, w), lambda b, i: (b, 0))
    cached = lambda heads: pl.BlockSpec((None, None, past * heads, HEAD_DIM), lambda b, i: (b, l, 0, 0))
    qw, vw = MLA_HEADS * MLA_PAD, MLA_HEADS * HEAD_DIM
    nw = NA_HEADS * HEAD_DIM
    kw = GQA_KV_HEADS * HEAD_DIM
    o_c = _attention(gq, [(cache_gqa_k, cached(GQA_KV_HEADS), cache_gqa_v, cached(GQA_KV_HEADS), None, None,
                           GQA_KV_HEADS),
                          (gk, own(kw, 0), gv, own(kw, 0), None, None, 0)],
                     q_index=lambda b, i: (b, 0), n_heads=GQA_HEADS, group=GQA_HEADS // GQA_KV_HEADS,
                     dk=HEAD_DIM, dv=HEAD_DIM, name="lat_gqa",
                     t=t, grid=(t // seq, 1), tq=seq)
    tq = NA_TQ
    nq = seq // tq
    common = dict(t=t, grid=(t // seq, nq), tq=tq)
    o_a = _attention(q_mla, [(kc_mla, flat(qw), vc_mla, flat(vw), None, None, 0),
                             (k_mla, own(qw, 0), v_mla, own(vw, 0), None, None, 0)],
                     q_index=lambda b, i: (b * nq + i, 0), n_heads=MLA_HEADS, group=1, dk=MLA_PAD, dv=HEAD_DIM,
                     name="lat_mla", **common)
    win_step, win_size = _na_key_window(seq, tq)
    window = (win_step * GRID_W, win_size * GRID_W)
    bias_spec = pl.BlockSpec((NA_HEADS, tq, window[1]), lambda b, i: (0, i, 0))
    o_b = _attention(na, [(cache_na_k, cached(NA_HEADS), cache_na_v, cached(NA_HEADS), None, None, NA_HEADS),
                          (na, own(nw, 1), na, own(nw, 2), bias, bias_spec, 0, window)],
                     q_index=lambda b, i: (b * nq + i, 0), n_heads=NA_HEADS, group=1,
                     dk=HEAD_DIM, dv=HEAD_DIM, name="lat_na", **common)
    return o_a, o_b, o_c


def _outproj_kernel(x_ref, gt_ref, oa_ref, ob_ref, oc_ref, w_ref, o_ref):
    wa = oa_ref.shape[1]
    wb = ob_ref.shape[1]
    acc = jnp.dot(oa_ref[...], w_ref[0:wa, :], preferred_element_type=F32)
    acc += jnp.dot(ob_ref[...], w_ref[wa:wa + wb, :], preferred_element_type=F32)
    acc += jnp.dot(oc_ref[...], w_ref[wa + wb:, :], preferred_element_type=F32)
    o_ref[...] = x_ref[...] + gt_ref[0] * acc


def _outproj(x, mods, l, stream, o_a, o_b, o_c, w_out):
    t, d = x.shape
    tm = 512
    row = lambda a: pl.BlockSpec((tm, a.shape[1]), lambda i: (i, 0))
    return pl.pallas_call(
        _outproj_kernel,
        grid=(t // tm,),
        in_specs=[
            row(x),
            _mod_spec(l, 5, stream, tm),
            row(o_a), row(o_b), row(o_c),
            _resident((None, d, d), lambda i: (l, 0, 0)),
        ],
        out_specs=row(x),
        out_shape=jax.ShapeDtypeStruct((t, d), F32),
        compiler_params=_params("parallel"),
        name="outproj",
    )(x, mods, o_a, o_b, o_c, w_out)


def _project_layout_kernel(w_ref, o_ref):
    split = COL_KR + MLA_ROPE
    o_ref[:, :split] = w_ref[:, :split].astype(BF16)
    o_ref[:, split:COL_NA] = jnp.zeros((o_ref.shape[0], COL_NA - split), BF16)
    o_ref[:, COL_NA:] = w_ref[:, split:].astype(BF16)


def _project_layout(w_in):
    depth, d, n = w_in.shape
    tk = 256
    return pl.pallas_call(
        _project_layout_kernel,
        grid=(depth, d // tk),
        in_specs=[pl.BlockSpec((None, tk, n), lambda l, i: (l, i, 0))],
        out_specs=pl.BlockSpec((None, tk, PROJ_W), lambda l, i: (l, i, 0)),
        out_shape=jax.ShapeDtypeStruct((depth, d, PROJ_W), BF16),
        compiler_params=_params("parallel", "parallel"),
        name="project_layout",
    )(w_in)


def _rope_tables(seq, d):
    quarter = d // 4
    tt = np.arange(seq)
    pos = np.stack([tt // GRID_W, tt % GRID_W], axis=-1).astype(np.float64)
    inv = ROPE_THETA ** (-np.arange(quarter, dtype=np.float64) / quarter)
    ang = pos[:, :, None] * inv
    cos = np.cos(ang)
    sin = np.sin(ang)
    cos_t = np.stack([cos, cos], axis=2).reshape(seq, d)
    sin_t = np.stack([-sin, sin], axis=2).reshape(seq, d)
    pad = LANES - d
    if pad:
        cos_t = np.concatenate([cos_t, np.ones((seq, pad))], axis=-1)
        sin_t = np.concatenate([sin_t, np.zeros((seq, pad))], axis=-1)
    return jnp.asarray(cos_t, F32), jnp.asarray(sin_t, F32)


def _na_key_window(seq, tq):
    rows = seq // GRID_W
    wh = min(NA_WIN_H, rows)
    qr = tq // GRID_W
    first = lambda r: min(max(r - wh // 2, 0), rows - wh)
    spans = [(first(i * qr), first(i * qr + qr - 1) + wh) for i in range(rows // qr)]
    size = max(hi - lo for lo, hi in spans)
    size += size % 2
    starts = [min(lo, rows - size) for lo, _ in spans]
    step = starts[1] - starts[0] if len(starts) > 1 else 0
    assert all(s == i * step and s <= lo and hi <= s + size for i, (s, (lo, hi)) in enumerate(zip(starts, spans)))
    return step, size


def _na_bias_kernel(t_ref, o_ref, *, rows, wh, qr, win_step, win_size):
    tile = pl.program_id(1)
    k0 = tile * win_step
    shape = (GRID_W, 2 * GRID_W)
    c = lax.broadcasted_iota(jnp.int32, shape, 0)
    lane = lax.broadcasted_iota(jnp.int32, shape, 1)
    kc = lane % GRID_W
    cs = jnp.clip(c - NA_WIN_W // 2, 0, GRID_W - NA_WIN_W)
    col_ok = (kc >= cs) & (kc < cs + NA_WIN_W)

    def one_row(rl, carry):
        r = tile * qr + rl
        rs = jnp.clip(r - wh // 2, 0, rows - wh)
        q_rows = pl.ds(pl.multiple_of(rl * GRID_W, GRID_W), GRID_W)
        for j in range(win_size // 2):
            kr = k0 + 2 * j + lane // GRID_W
            ok = col_ok & (kr >= rs) & (kr < rs + wh)
            d = jnp.clip(k0 + 2 * j - r + NA_WIN_H, 0, 2 * NA_WIN_H - 1)
            pair = jnp.broadcast_to(t_ref[0, pl.ds(d, 1), :], shape)
            toeplitz = pltpu.roll(pair, 2 * GRID_W - (NA_WIN_W - 1), 1, stride=1, stride_axis=0)
            o_ref[0, q_rows, j * 2 * GRID_W:(j + 1) * 2 * GRID_W] = jnp.where(ok, toeplitz * LOG2E, NEG_INF)
        return carry

    lax.fori_loop(0, qr, one_row, 0)


def _na_bias(rpb, seq, tq):
    nh, nr, nc = rpb.shape
    rows = seq // GRID_W
    wh = min(NA_WIN_H, rows)
    win_step, win_size = _na_key_window(seq, tq)
    padded = jnp.pad(rpb.astype(F32), ((0, 0), (1, 1), (0, GRID_W - nc)))
    table = jnp.concatenate([padded[:, :nr + 1], padded[:, 1:]], axis=-1)
    return pl.pallas_call(
        functools.partial(_na_bias_kernel, rows=rows, wh=wh, qr=tq // GRID_W, win_step=win_step,
                          win_size=win_size),
        grid=(nh, seq // tq),
        in_specs=[pl.BlockSpec((1, nr + 1, 2 * GRID_W), lambda h, i: (h, 0, 0))],
        out_specs=pl.BlockSpec((1, tq, win_size * GRID_W), lambda h, i: (h, i, 0)),
        out_shape=jax.ShapeDtypeStruct((nh, seq, win_size * GRID_W), F32),
        compiler_params=_params("parallel", "parallel"),
        name="na_bias",
    )(table)


def kernel(x_prompt, x_sample, cache_mla_ckv, cache_mla_krope, cache_na_k, cache_na_v, cache_gqa_k, cache_gqa_v, c, c_ctx, ada_w, ada_b, norm_g, ffn_wg, ffn_wu, ffn_wd, w_in, mla_q_norm, mla_wqb, mla_kv_norm, mla_wkvb, na_rpb, gqa_q_norm, gqa_k_norm, w_out, final_norm):
    depth = ada_w.shape[0]
    nb_ctx, seq_ctx, d = x_prompt.shape
    nb_lat, seq_lat, _ = x_sample.shape
    past = cache_mla_ckv.shape[2]

    wg = ffn_wg
    wu = ffn_wu
    wd = ffn_wd
    w_in_p = _project_layout(w_in)
    wqb =jnp.pad(mla_wqb.reshape(depth, Q_LORA, MLA_HEADS, MLA_QK),
                  ((0, 0), (0, 0), (0, 0), (0, MLA_PAD - MLA_QK))).reshape(depth, Q_LORA, MLA_HEADS * MLA_PAD)
    wkv = mla_wkvb.reshape(depth, KV_LORA, MLA_HEADS, 2 * HEAD_DIM)
    wts = {
        "q_norm": mla_q_norm.reshape(depth, 1, Q_LORA),
        "wqb": wqb.astype(BF16),
        "kv_norm": mla_kv_norm.reshape(depth, 1, KV_LORA),
        "wk": wkv[..., :MLA_NOPE].reshape(depth, KV_LORA, MLA_HEADS * MLA_NOPE).astype(BF16),
        "wv": wkv[..., MLA_NOPE:].reshape(depth, KV_LORA, MLA_HEADS * HEAD_DIM).astype(BF16),
        "gq_norm": gqa_q_norm.reshape(depth, 1, HEAD_DIM),
        "gk_norm": gqa_k_norm.reshape(depth, 1, HEAD_DIM),
    }
    w_out_b = w_out.astype(BF16)
    norm_g3 = norm_g.reshape(depth * 3, 1, d)
    final_g = final_norm.reshape(1, 1, d)
    tables = _rope_tables(seq_lat, MLA_ROPE) + _rope_tables(seq_lat, HEAD_DIM)
    kr_cache = jnp.pad(cache_mla_krope, ((0, 0), (0, 0), (0, 0), (0, LANES - MLA_ROPE)))
    caches = (cache_na_k.reshape(nb_lat, depth, past * NA_HEADS, HEAD_DIM),
              cache_na_v.reshape(nb_lat, depth, past * NA_HEADS, HEAD_DIM),
              cache_gqa_k.reshape(nb_lat, depth, past * GQA_KV_HEADS, HEAD_DIM),
              cache_gqa_v.reshape(nb_lat, depth, past * GQA_KV_HEADS, HEAD_DIM))

    cond = jnp.concatenate([c_ctx[None, :], c, jnp.zeros((MOD_ROWS - 1 - nb_lat, d), c.dtype)], axis=0)
    mods = _modulation(cond, ada_w, ada_b).reshape(depth * MOD_ROWS, 1, N_MOD * d)

    xp = x_prompt.reshape(nb_ctx * seq_ctx, d)
    xs = x_sample.reshape(nb_lat * seq_lat, d)
    ctx_row = (0, nb_ctx * seq_ctx)
    lat_row = (1, seq_lat)
    biases = [_na_bias(na_rpb[l], seq_lat, NA_TQ) for l in range(depth)]

    new_cache = None
    for l in range(depth):
        last = l == depth - 1
        xp = _ffn(xp, mods, norm_g3, l, 0, 0, ctx_row, wg, wu, wd)
        outs = _qkv(xp, mods, norm_g3, l, ctx_row, w_in_p, wts, seq_ctx, cache=(nb_ctx, depth, new_cache))
        q_mla, k_mla, v_mla, gq, gk, gv, na = outs[:N_QKV_OUT]
        new_cache = outs[N_QKV_OUT:]
        o_a, o_b, o_c = _context_attention(q_mla, k_mla, v_mla, gq, gk, gv, na, seq_ctx)
        xp = _outproj(xp, mods, l, ctx_row, o_a, o_b, o_c, w_out_b)
        xp = _ffn(xp, mods, norm_g3, l, 2, 1, ctx_row, wg, wu, wd, final_g if last else None)
        xs = _ffn(xs, mods, norm_g3, l, 0, 0, lat_row, wg, wu, wd)
        q_mla, k_mla, v_mla, gq, gk, gv, na = _qkv(xs, mods, norm_g3, l, lat_row, w_in_p, wts, 256, tables=tables)
        kc_mla, vc_mla = _cache_kv(cache_mla_ckv, kr_cache, l, wts)
        o_a, o_b, o_c = _latent_attention(q_mla, k_mla, v_mla, gq, gk, gv, na, kc_mla, vc_mla, caches, biases[l], l,
                                          seq_lat)
        xs = _outproj(xs, mods, l, lat_row, o_a, o_b, o_c, w_out_b)
        xs = _ffn(xs, mods, norm_g3, l, 2, 1, lat_row, wg, wu, wd, final_g if last else None)

    y_prompt = xp.reshape(nb_ctx, seq_ctx, d)
    y_sample = xs.reshape(nb_lat, seq_lat, d)
    ckv_c, kr_c, nak_c, nav_c, gk_c, gv_c = new_cache
    na_shape = (nb_ctx, depth, seq_ctx, NA_HEADS, HEAD_DIM)
    gqa_shape = (nb_ctx, depth, seq_ctx, GQA_KV_HEADS, HEAD_DIM)
    return (y_prompt, y_sample, ckv_c, kr_c, nak_c.reshape(na_shape), nav_c.reshape(na_shape),
            gk_c.reshape(gqa_shape), gv_c.reshape(gqa_shape))
```

```python
import functools

import jax
import jax.numpy as jnp
import numpy as np
from jax import lax
from jax.experimental import pallas as pl
from jax.experimental.pallas import tpu as pltpu

F32 = jnp.float32
BF16 = jnp.bfloat16

D_MODEL = 2048
D_FF = 5632
N_MOD = 9
GRID_W = 64
ROPE_THETA = 10000.0
HEAD_DIM = 128
MLA_HEADS = 8
MLA_NOPE = 128
MLA_ROPE = 64
MLA_QK = MLA_NOPE + MLA_ROPE
MLA_PAD = 256
Q_LORA = 512
KV_LORA = 256
NA_HEADS = 4
NA_WIN_H = 8
NA_WIN_W = 16
NA_TQ = 512
GQA_HEADS = 4
GQA_KV_HEADS = 2
EPS = 1e-6
NEG_INF = -1e30
LOG2E = 1.4426950408889634
MOD_ROWS = 16

LANES = 128
COL_CQ = 0
COL_CKV = COL_CQ + Q_LORA
COL_KR = COL_CKV + KV_LORA
COL_NA = COL_KR + LANES
COL_GQ = COL_NA + 3 * NA_HEADS * HEAD_DIM
COL_GK = COL_GQ + GQA_HEADS * HEAD_DIM
COL_GV = COL_GK + GQA_KV_HEADS * HEAD_DIM
PROJ_W = COL_GV + GQA_KV_HEADS * HEAD_DIM

VMEM_LIMIT = 62 * 1024 * 1024


def _params(*sem):
    return pltpu.CompilerParams(dimension_semantics=sem, vmem_limit_bytes=VMEM_LIMIT)


def _resident(shape, index_map):
    return pl.BlockSpec(shape, index_map, pipeline_mode=pl.Buffered(1))


def _as_bf16(ref):
    v = ref[...]
    return v if v.dtype == BF16 else v.astype(BF16)


def _rms(x, g):
    return x * lax.rsqrt(jnp.mean(x * x, axis=-1, keepdims=True) + EPS) * g


NORM_ROWS = 256


def _norm_modulate_store(x_ref, ng_ref, sc_ref, sh_ref, h_scr):
    gain = ng_ref[0] * (1.0 + sc_ref[0])
    shift = sh_ref[0]

    def body(i, carry):
        rows = pl.ds(pl.multiple_of(i * NORM_ROWS, NORM_ROWS), NORM_ROWS)
        x = x_ref[rows, :]
        r = lax.rsqrt(jnp.mean(x * x, axis=-1, keepdims=True) + EPS)
        h_scr[rows, :] = (x * r * gain + shift).astype(BF16)
        return carry

    lax.fori_loop(0, x_ref.shape[0] // NORM_ROWS, body, 0)


def _rope(x, cos, sin, quarter):
    n = x.shape[-1]
    lane = lax.broadcasted_iota(jnp.int32, x.shape, 1)
    first = (lane % (2 * quarter)) < quarter
    sw = jnp.where(first, pltpu.roll(x, n - quarter, 1), pltpu.roll(x, quarter, 1))
    return x * cos + sw * sin


def _mod_kernel(c_ref, w_ref, b_ref, o_ref):
    c = c_ref[...]
    s = (c * jax.nn.sigmoid(c)).astype(BF16)
    o_ref[0] = jnp.dot(s, w_ref[0].astype(BF16), preferred_element_type=F32) + b_ref[0]


def _modulation(cond, ada_w, ada_b):
    depth, d, n = ada_w.shape
    tn = 1024
    return pl.pallas_call(
        _mod_kernel,
        grid=(depth, n // tn),
        in_specs=[
            pl.BlockSpec((MOD_ROWS, d), lambda l, j: (0, 0)),
            pl.BlockSpec((1, d, tn), lambda l, j: (l, 0, j)),
            pl.BlockSpec((1, 1, tn), lambda l, j: (l, 0, j)),
        ],
        out_specs=pl.BlockSpec((1, MOD_ROWS, tn), lambda l, j: (l, 0, j)),
        out_shape=jax.ShapeDtypeStruct((depth, MOD_ROWS, n), F32),
        compiler_params=_params("parallel", "parallel"),
        name="modulation",
    )(cond, ada_w, ada_b.reshape(depth, 1, n))


def _mod_spec(l, chunk, stream, tm):
    base, tokens = stream
    return pl.BlockSpec((1, 1, D_MODEL), lambda i, *_: (l * MOD_ROWS + base + (i * tm) // tokens, 0, chunk))


def _ffn_kernel(x_ref, ng_ref, sh_ref, sc_ref, gt_ref, wg_ref, wu_ref, wd_ref, *rest, final):
    if final:
        fg_ref, o_ref, h_scr = rest
    else:
        o_ref, h_scr = rest
    f = pl.program_id(1)
    tm = x_ref.shape[0]

    def swiglu(rows, wg, wu, wd):
        h = h_scr[rows, :]
        g = jnp.dot(h, wg, preferred_element_type=F32)
        u = jnp.dot(h, wu, preferred_element_type=F32)
        a = (g * jax.nn.sigmoid(g) * u).astype(BF16)
        return jnp.dot(a, wd, preferred_element_type=F32)

    @pl.when(f == 0)
    def _():
        wg, wu, wd = _as_bf16(wg_ref), _as_bf16(wu_ref), _as_bf16(wd_ref)
        gain = ng_ref[0] * (1.0 + sc_ref[0])
        shift = sh_ref[0]
        for half in range(2):
            for r0 in range(half * tm // 2, (half + 1) * tm // 2, NORM_ROWS):
                x = x_ref[r0:r0 + NORM_ROWS, :]
                r = lax.rsqrt(jnp.mean(x * x, axis=-1, keepdims=True) + EPS)
                h_scr[r0:r0 + NORM_ROWS, :] = (x * r * gain + shift).astype(BF16)
            rows = slice(half * tm // 2, (half + 1) * tm // 2)
            o_ref[rows, :] = swiglu(rows, wg, wu, wd)

    last = pl.num_programs(1) - 1

    @pl.when((f > 0) & ((f < last) | final))
    def _():
        o_ref[...] += swiglu(slice(None), _as_bf16(wg_ref), _as_bf16(wu_ref), _as_bf16(wd_ref))

    @pl.when(f == last)
    def _():
        gate = 0.5 * gt_ref[0]
        if final:
            def body(i, carry):
                rows = pl.ds(pl.multiple_of(i * NORM_ROWS, NORM_ROWS), NORM_ROWS)
                o_ref[rows, :] = _rms(x_ref[rows, :] + gate * o_ref[rows, :], fg_ref[0])
                return carry

            lax.fori_loop(0, tm // NORM_ROWS, body, 0)
        else:
            wg, wu, wd = _as_bf16(wg_ref), _as_bf16(wu_ref), _as_bf16(wd_ref)
            for half in range(2):
                rows = slice(half * tm // 2, (half + 1) * tm // 2)
                o_ref[rows, :] += swiglu(rows, wg, wu, wd)
                for r0 in range(half * tm // 2, (half + 1) * tm // 2, NORM_ROWS):
                    o_ref[r0:r0 + NORM_ROWS, :] = x_ref[r0:r0 + NORM_ROWS, :] + gate * o_ref[r0:r0 + NORM_ROWS, :]


def _ffn(x, mods, norm_g, l, sub, k, stream, wg, wu, wd, final_g=None):
    t, d = x.shape
    tm, tf = 1024, 512
    chunk = 3 * sub
    in_specs = [
        pl.BlockSpec((tm, d), lambda i, j: (i, 0), pipeline_mode=pl.Buffered(1)),
        pl.BlockSpec((1, 1, d), lambda i, j: (l * 3 + sub, 0, 0)),
        _mod_spec(l, chunk, stream, tm),
        _mod_spec(l, chunk + 1, stream, tm),
        _mod_spec(l, chunk + 2, stream, tm),
        pl.BlockSpec((None, None, d, tf), lambda i, j: (l, k, 0, j)),
        pl.BlockSpec((None, None, d, tf), lambda i, j: (l, k, 0, j)),
        pl.BlockSpec((None, None, tf, d), lambda i, j: (l, k, j, 0)),
    ]
    args = [x, norm_g, mods, mods, mods, wg, wu, wd]
    if final_g is not None:
        in_specs.append(pl.BlockSpec((1, 1, d), lambda i, j: (0, 0, 0)))
        args.append(final_g)
    return pl.pallas_call(
        functools.partial(_ffn_kernel, final=final_g is not None),
        grid=(t // tm, D_FF // tf),
        in_specs=in_specs,
        out_specs=pl.BlockSpec((tm, d), lambda i, j: (i, 0)),
        out_shape=jax.ShapeDtypeStruct((t, d), F32),
        scratch_shapes=[pltpu.VMEM((tm, d), BF16)],
        compiler_params=_params("parallel", "arbitrary"),
        name="ffn",
    )(*args)


def _expand_kv(ckv_bf, kr, wk_ref, wv_ref, k_out, v_out):
    kn = jnp.dot(ckv_bf, wk_ref[...], preferred_element_type=F32)
    v_out[...] = jnp.dot(ckv_bf, wv_ref[...], preferred_element_type=F32).astype(BF16)
    kr_bf = kr.astype(BF16)
    for h in range(MLA_HEADS):
        k_out[:, h * MLA_PAD:h * MLA_PAD + MLA_NOPE] = kn[:, h * MLA_NOPE:(h + 1) * MLA_NOPE].astype(BF16)
        k_out[:, h * MLA_PAD + MLA_NOPE:(h + 1) * MLA_PAD] = kr_bf


N_QKV_OUT = 7
N_CACHE_OUT = 6


def _qkv_kernel(x_ref, ng_ref, sh_ref, sc_ref, w_ref, qn_ref, wqb_ref, kvn_ref, wk_ref, wv_ref, gqn_ref, gkn_ref,
                *rest, rope, n_prev, cache_out, layer):
    if rope:
        c64_ref, s64_ref, c128_ref, s128_ref = rest[:4]
        rest = rest[4:]
        rot64 = lambda v: _rope(v, c64_ref[...], s64_ref[...], MLA_ROPE // 4)
        rot128 = lambda v: _rope(v, c128_ref[...], s128_ref[...], HEAD_DIM // 4)
    else:
        rot64 = rot128 = lambda v: v
    rest = rest[n_prev:]
    q_out, k_out, v_out, gq_out, gk_out, gv_out, na_out = rest[:N_QKV_OUT]
    rest = rest[N_QKV_OUT:]
    if cache_out:
        cache_refs = rest[:N_CACHE_OUT]
        rest = rest[N_CACHE_OUT:]
        if n_prev == 0:
            for ref in cache_refs:
                for other in range(ref.shape[0]):
                    if other != layer:
                        ref[other] = jnp.zeros(ref.shape[1:], ref.dtype)
            cache_refs = [ref.at[layer] for ref in cache_refs]
        ckv_c, kr_c, nak_c, nav_c, gk_c, gv_c = cache_refs
    (h_scr,) = rest
    tm = x_ref.shape[0]

    _norm_modulate_store(x_ref, ng_ref, sc_ref, sh_ref, h_scr)
    h = h_scr[...]
    low = jnp.dot(h, w_ref[:, COL_CQ:COL_NA], preferred_element_type=F32)
    na = jnp.dot(h, w_ref[:, COL_NA:COL_GQ], preferred_element_type=F32)
    gg = jnp.dot(h, w_ref[:, COL_GQ:PROJ_W], preferred_element_type=F32)

    qn = _rms(low[:, COL_CQ:COL_CKV], qn_ref[0]).astype(BF16)
    q = jnp.dot(qn, wqb_ref[...], preferred_element_type=F32) * (MLA_QK ** -0.5 * LOG2E)
    for hh in range(MLA_HEADS):
        lo = hh * MLA_PAD
        q_out[:, lo:lo + MLA_NOPE] = q[:, lo:lo + MLA_NOPE].astype(BF16)
        q_out[:, lo + MLA_NOPE:lo + MLA_PAD] = rot64(q[:, lo + MLA_NOPE:lo + MLA_PAD]).astype(BF16)

    ckv = _rms(low[:, COL_CKV:COL_KR], kvn_ref[0])
    kr = rot64(low[:, COL_KR:COL_NA])
    _expand_kv(ckv.astype(BF16), kr, wk_ref, wv_ref, k_out, v_out)

    nw = NA_HEADS * HEAD_DIM
    gw = GQA_HEADS * HEAD_DIM
    kw = GQA_KV_HEADS * HEAD_DIM
    na_out[:, :nw] = (na[:, :nw] * (HEAD_DIM ** -0.5 * LOG2E)).astype(BF16)
    na_out[:, nw:] = na[:, nw:].astype(BF16)
    gq_gain = gqn_ref[0] * (HEAD_DIM ** -0.5 * LOG2E)
    for hh in range(GQA_HEADS):
        sl = slice(hh * HEAD_DIM, (hh + 1) * HEAD_DIM)
        gq_out[:, sl] = rot128(_rms(gg[:, sl], gq_gain)).astype(BF16)
    gv = gg[:, gw + kw:]
    gv_out[...] = gv.astype(BF16)
    for hh in range(GQA_KV_HEADS):
        sl = slice(hh * HEAD_DIM, (hh + 1) * HEAD_DIM)
        gk = rot128(_rms(gg[:, gw + hh * HEAD_DIM:gw + (hh + 1) * HEAD_DIM], gkn_ref[0]))
        gk_out[:, sl] = gk.astype(BF16)
        if cache_out:
            gk_c[pl.ds(hh, tm, stride=GQA_KV_HEADS), :] = gk
            gv_c[pl.ds(hh, tm, stride=GQA_KV_HEADS), :] = gv[:, sl]
    if cache_out:
        ckv_c[...] = ckv
        kr_c[...] = kr[:, :MLA_ROPE]
        for hh in range(NA_HEADS):
            sl = slice(hh * HEAD_DIM, (hh + 1) * HEAD_DIM)
            nak_c[pl.ds(hh, tm, stride=NA_HEADS), :] = na[:, nw + hh * HEAD_DIM:nw + (hh + 1) * HEAD_DIM]
            nav_c[pl.ds(hh, tm, stride=NA_HEADS), :] = na[:, 2 * nw + hh * HEAD_DIM:2 * nw + (hh + 1) * HEAD_DIM]


def _cache_shapes(nb, depth, seq):
    return [
        (nb, depth, seq, KV_LORA), (nb, depth, seq, MLA_ROPE),
        (nb, depth, seq * NA_HEADS, HEAD_DIM), (nb, depth, seq * NA_HEADS, HEAD_DIM),
        (nb, depth, seq * GQA_KV_HEADS, HEAD_DIM), (nb, depth, seq * GQA_KV_HEADS, HEAD_DIM),
    ]


def _qkv(x, mods, norm_g, l, stream, w_in, wts, tm, tables=None, cache=None):
    t, d = x.shape
    rope = tables is not None
    qw = MLA_HEADS * MLA_PAD
    vw = MLA_HEADS * HEAD_DIM
    in_specs = [
        pl.BlockSpec((tm, d), lambda i: (i, 0)),
        pl.BlockSpec((1, 1, d), lambda i: (l * 3 + 1, 0, 0)),
        _mod_spec(l, 3, stream, tm),
        _mod_spec(l, 4, stream, tm),
        _resident((None, d, PROJ_W), lambda i: (l, 0, 0)),
        pl.BlockSpec((1, 1, Q_LORA), lambda i: (l, 0, 0)),
        _resident((None, Q_LORA, qw), lambda i: (l, 0, 0)),
        pl.BlockSpec((1, 1, KV_LORA), lambda i: (l, 0, 0)),
        _resident((None, KV_LORA, vw), lambda i: (l, 0, 0)),
        _resident((None, KV_LORA, vw), lambda i: (l, 0, 0)),
        pl.BlockSpec((1, 1, HEAD_DIM), lambda i: (l, 0, 0)),
        pl.BlockSpec((1, 1, HEAD_DIM), lambda i: (l, 0, 0)),
    ]
    args = [x, norm_g, mods, mods, w_in, wts["q_norm"], wts["wqb"], wts["kv_norm"], wts["wk"], wts["wv"],
            wts["gq_norm"], wts["gk_norm"]]
    if rope:
        per = tables[0].shape[0] // tm
        in_specs += [pl.BlockSpec((tm, LANES), lambda i: (i % per, 0))] * 4
        args += list(tables)
    row = lambda w: pl.BlockSpec((tm, w), lambda i: (i, 0))
    widths = [qw, qw, vw, GQA_HEADS * HEAD_DIM, GQA_KV_HEADS * HEAD_DIM, GQA_KV_HEADS * HEAD_DIM,
              3 * NA_HEADS * HEAD_DIM]
    out_specs = [row(w) for w in widths]
    out_shape = [jax.ShapeDtypeStruct((t, w), BF16) for w in widths]
    aliases = {}
    n_prev = 0
    if cache is not None:
        nb, depth, prev = cache
        assert tm * nb == t
        shapes = _cache_shapes(nb, depth, tm)
        if prev is not None:
            n_prev = len(prev)
            aliases = {len(args) + k: N_QKV_OUT + k for k in range(n_prev)}
            in_specs += [pl.BlockSpec(memory_space=pl.ANY)] * n_prev
            args += list(prev)
            out_specs += [pl.BlockSpec((None, None) + s[2:], lambda i: (i, l, 0, 0)) for s in shapes]
        else:
            out_specs += [pl.BlockSpec((None,) + s[1:], lambda i: (i, 0, 0, 0)) for s in shapes]
        out_shape += [jax.ShapeDtypeStruct(s, F32) for s in shapes]
    return pl.pallas_call(
        functools.partial(_qkv_kernel, rope=rope, n_prev=n_prev, cache_out=cache is not None, layer=l),
        grid=(t // tm,),
        in_specs=in_specs,
        out_specs=out_specs,
        out_shape=out_shape,
        input_output_aliases=aliases,
        scratch_shapes=[pltpu.VMEM((tm, d), BF16)],
        compiler_params=_params("parallel"),
        name="qkv",
    )(*args)


def _cache_kv_kernel(ckv_ref, kr_ref, wk_ref, wv_ref, k_out, v_out):
    _expand_kv(ckv_ref[...].astype(BF16), kr_ref[...], wk_ref, wv_ref, k_out, v_out)


def _cache_kv(cache_ckv, cache_kr_pad, l, wts):
    nb, _, past, _ = cache_ckv.shape
    qw = MLA_HEADS * MLA_PAD
    vw = MLA_HEADS * HEAD_DIM
    return pl.pallas_call(
        _cache_kv_kernel,
        grid=(nb,),
        in_specs=[
            pl.BlockSpec((None, None, past, KV_LORA), lambda b: (b, l, 0, 0)),
            pl.BlockSpec((None, None, past, LANES), lambda b: (b, l, 0, 0)),
            pl.BlockSpec((None, KV_LORA, vw), lambda b: (l, 0, 0)),
            pl.BlockSpec((None, KV_LORA, vw), lambda b: (l, 0, 0)),
        ],
        out_specs=[pl.BlockSpec((past, qw), lambda b: (b, 0)), pl.BlockSpec((past, vw), lambda b: (b, 0))],
        out_shape=[jax.ShapeDtypeStruct((nb * past, qw), BF16), jax.ShapeDtypeStruct((nb * past, vw), BF16)],
        compiler_params=_params("parallel"),
        name="cache_kv",
    )(cache_ckv, cache_kr_pad, wts["wk"], wts["wv"])


def _attn_kernel(*refs, n_heads, group, dk, dv, seg_kinds, n_sub):
    q_ref = refs[0]
    pos = 1
    segs = []
    windows = [w for _, _, w in seg_kinds]
    for has_bias, inter, _ in seg_kinds:
        b_ref = refs[pos + 2] if has_bias else None
        segs.append((refs[pos], refs[pos + 1], b_ref, inter))
        pos += 3 if has_bias else 2
    o_ref = refs[pos]

    def head_rows(ref, rows, hk, width, inter):
        if inter:
            return ref[pl.ds(hk, ref.shape[0] // inter, stride=inter), :]
        return ref[rows, hk * width:(hk + 1) * width]

    def attend(q_rows, seg_rows):
        for h in range(n_heads):
            hk = h // group
            q = q_ref[q_rows, h * dk:(h + 1) * dk].astype(BF16)
            scores = []
            for (k_ref, _, b_ref, inter), rows in zip(segs, seg_rows):
                k = head_rows(k_ref, rows, hk, dk, inter).astype(BF16)
                s = lax.dot_general(q, k, (((1,), (1,)), ((), ())), preferred_element_type=F32)
                if b_ref is not None:
                    s = s + b_ref[h]
                scores.append(s)
            m = functools.reduce(jnp.maximum, [s.max(axis=-1, keepdims=True) for s in scores])
            o = None
            for s, (_, v_ref, _, inter), rows in zip(scores, segs, seg_rows):
                p = jnp.exp2(s - m).astype(BF16)
                v = head_rows(v_ref, rows, hk, dv, inter).astype(BF16)
                pv = jnp.dot(p, jnp.concatenate([v, jnp.ones_like(v)], axis=1), preferred_element_type=F32)
                o = pv if o is None else o + pv
            o_ref[q_rows, h * dv:(h + 1) * dv] = (o[:, :dv] / o[:, dv:]).astype(o_ref.dtype)

    if n_sub == 1:
        attend(slice(None), [slice(None) if w is None else
                             pl.ds(pl.multiple_of(pl.program_id(1) * w[0], w[0]), w[1]) for w in windows])
    else:
        def body(s, carry):
            rows = lambda ref: pl.ds(pl.multiple_of(s * (ref.shape[0] // n_sub), ref.shape[0] // n_sub),
                                     ref.shape[0] // n_sub)
            attend(rows(q_ref), [rows(seg[0]) for seg in segs])
            return carry

        lax.fori_loop(0, n_sub, body, 0, unroll=True)


def _attention(q, segs, *, t, grid, tq, q_index, n_heads, group, dk, dv, name, n_sub=1):
    assert n_sub == 1 or all(s[4] is None and not s[6] and len(s) == 7 for s in segs)
    in_specs = [pl.BlockSpec((tq, n_heads * dk), q_index)]
    args = [q]
    seg_kinds = []
    for k_arr, k_spec, v_arr, v_spec, b_arr, b_spec, inter, *window in segs:
        in_specs += [k_spec, v_spec]
        args += [k_arr, v_arr]
        seg_kinds.append((b_arr is not None, inter, window[0] if window else None))
        if b_arr is not None:
            in_specs.append(b_spec)
            args.append(b_arr)
    out_index = lambda *g: (q_index(*g)[0], 0)
    return pl.pallas_call(
        functools.partial(_attn_kernel, n_heads=n_heads, group=group, dk=dk, dv=dv,
                          seg_kinds=tuple(seg_kinds), n_sub=n_sub),
        grid=grid,
        in_specs=in_specs,
        out_specs=pl.BlockSpec((tq, n_heads * dv), out_index),
        out_shape=jax.ShapeDtypeStruct((t, n_heads * dv), BF16),
        compiler_params=_params(*(("parallel",) * len(grid))),
        name=name,
    )(*args)


def _context_attention(q_mla, k_mla, v_mla, gq, gk, gv, na, seq):
    t = na.shape[0]
    n_sub = 4
    rows = seq * n_sub
    grid = (t // rows,)
    blk = lambda w, c: pl.BlockSpec((rows, w), lambda b: (b, c))
    common = dict(t=t, grid=grid, tq=rows, n_sub=n_sub)
    nw = NA_HEADS * HEAD_DIM
    kw = GQA_KV_HEADS * HEAD_DIM
    o_a = _attention(q_mla, [(k_mla, blk(MLA_HEADS * MLA_PAD, 0), v_mla, blk(MLA_HEADS * HEAD_DIM, 0), None, None, 0)],
                     q_index=lambda b: (b, 0), n_heads=MLA_HEADS, group=1, dk=MLA_PAD, dv=HEAD_DIM,
                     name="ctx_mla", **common)
    o_b = _attention(na, [(na, blk(nw, 1), na, blk(nw, 2), None, None, 0)],
                     q_index=lambda b: (b, 0), n_heads=NA_HEADS, group=1, dk=HEAD_DIM, dv=HEAD_DIM,
                     name="ctx_na", **common)
    o_c = _attention(gq, [(gk, blk(kw, 0), gv, blk(kw, 0), None, None, 0)],
                     q_index=lambda b: (b, 0), n_heads=GQA_HEADS, group=GQA_HEADS // GQA_KV_HEADS,
                     dk=HEAD_DIM, dv=HEAD_DIM, name="ctx_gqa", **common)
    return o_a, o_b, o_c


def _latent_attention(q_mla, k_mla, v_mla, gq, gk, gv, na, kc_mla, vc_mla, caches, bias, l, seq):
    cache_na_k, cache_na_v, cache_gqa_k, cache_gqa_v = caches
    t = na.shape[0]
    past = kc_mla.shape[0] // (t // seq)
    own = lambda w, c: pl.BlockSpec((seq, w), lambda b, i: (b, c))
    flat = lambda w: pl.BlockSpec((past, w), lambda b, i: (b, 0))
    cached = lambda heads: pl.BlockSpec((None, None, past * heads, HEAD_DIM), lambda b, i: (b, l, 0, 0))
    qw, vw = MLA_HEADS * MLA_PAD, MLA_HEADS * HEAD_DIM
    nw = NA_HEADS * HEAD_DIM
    kw = GQA_KV_HEADS * HEAD_DIM
    o_c = _attention(gq, [(cache_gqa_k, cached(GQA_KV_HEADS), cache_gqa_v, cached(GQA_KV_HEADS), None, None,
                           GQA_KV_HEADS),
                          (gk, own(kw, 0), gv, own(kw, 0), None, None, 0)],
                     q_index=lambda b, i: (b, 0), n_heads=GQA_HEADS, group=GQA_HEADS // GQA_KV_HEADS,
                     dk=HEAD_DIM, dv=HEAD_DIM, name="lat_gqa",
                     t=t, grid=(t // seq, 1), tq=seq)
    o_a = _attention(q_mla, [(kc_mla, flat(qw), vc_mla, flat(vw), None, None, 0),
                             (k_mla, own(qw, 0), v_mla, own(vw, 0), None, None, 0)],
                     q_index=lambda b, i: (b, 0), n_heads=MLA_HEADS, group=1, dk=MLA_PAD, dv=HEAD_DIM,
                     name="lat_mla", t=t, grid=(t // seq, 1), tq=seq)
    tq = NA_TQ
    nq = seq // tq
    common = dict(t=t, grid=(t // seq, nq), tq=tq)
    win_step, win_size = _na_key_window(seq, tq)
    window = (win_step * GRID_W, win_size * GRID_W)
    bias_spec = pl.BlockSpec((NA_HEADS, tq, window[1]), lambda b, i: (0, i, 0))
    o_b = _attention(na, [(cache_na_k, cached(NA_HEADS), cache_na_v, cached(NA_HEADS), None, None, NA_HEADS),
                          (na, own(nw, 1), na, own(nw, 2), bias, bias_spec, 0, window)],
                     q_index=lambda b, i: (b * nq + i, 0), n_heads=NA_HEADS, group=1,
                     dk=HEAD_DIM, dv=HEAD_DIM, name="lat_na", **common)
    return o_a, o_b, o_c


def _outproj_kernel(x_ref, gt_ref, oa_ref, ob_ref, oc_ref, w_ref, o_ref):
    wa = oa_ref.shape[1]
    wb = ob_ref.shape[1]
    acc = jnp.dot(oa_ref[...], w_ref[0:wa, :], preferred_element_type=F32)
    acc += jnp.dot(ob_ref[...], w_ref[wa:wa + wb, :], preferred_element_type=F32)
    acc += jnp.dot(oc_ref[...], w_ref[wa + wb:, :], preferred_element_type=F32)
    o_ref[...] = x_ref[...] + gt_ref[0] * acc


def _outproj(x, mods, l, stream, o_a, o_b, o_c, w_out):
    t, d = x.shape
    tm = 512
    row = lambda a: pl.BlockSpec((tm, a.shape[1]), lambda i: (i, 0))
    return pl.pallas_call(
        _outproj_kernel,
        grid=(t // tm,),
        in_specs=[
            row(x),
            _mod_spec(l, 5, stream, tm),
            row(o_a), row(o_b), row(o_c),
            _resident((None, d, d), lambda i: (l, 0, 0)),
        ],
        out_specs=row(x),
        out_shape=jax.ShapeDtypeStruct((t, d), F32),
        compiler_params=_params("parallel"),
        name="outproj",
    )(x, mods, o_a, o_b, o_c, w_out)


def _project_layout_kernel(w_ref, o_ref):
    split = COL_KR + MLA_ROPE
    o_ref[:, :split] = w_ref[:, :split].astype(BF16)
    o_ref[:, split:COL_NA] = jnp.zeros((o_ref.shape[0], COL_NA - split), BF16)
    o_ref[:, COL_NA:] = w_ref[:, split:].astype(BF16)


def _project_layout(w_in):
    depth, d, n = w_in.shape
    tk = 256
    return pl.pallas_call(
        _project_layout_kernel,
        grid=(depth, d // tk),
        in_specs=[pl.BlockSpec((None, tk, n), lambda l, i: (l, i, 0))],
        out_specs=pl.BlockSpec((None, tk, PROJ_W), lambda l, i: (l, i, 0)),
        out_shape=jax.ShapeDtypeStruct((depth, d, PROJ_W), BF16),
        compiler_params=_params("parallel", "parallel"),
        name="project_layout",
    )(w_in)


def _rope_tables(seq, d):
    quarter = d // 4
    tt = np.arange(seq)
    pos = np.stack([tt // GRID_W, tt % GRID_W], axis=-1).astype(np.float64)
    inv = ROPE_THETA ** (-np.arange(quarter, dtype=np.float64) / quarter)
    ang = pos[:, :, None] * inv
    cos = np.cos(ang)
    sin = np.sin(ang)
    cos_t = np.stack([cos, cos], axis=2).reshape(seq, d)
    sin_t = np.stack([-sin, sin], axis=2).reshape(seq, d)
    pad = LANES - d
    if pad:
        cos_t = np.concatenate([cos_t, np.ones((seq, pad))], axis=-1)
        sin_t = np.concatenate([sin_t, np.zeros((seq, pad))], axis=-1)
    return jnp.asarray(cos_t, F32), jnp.asarray(sin_t, F32)


def _na_key_window(seq, tq):
    rows = seq // GRID_W
    wh = min(NA_WIN_H, rows)
    qr = tq // GRID_W
    first = lambda r: min(max(r - wh // 2, 0), rows - wh)
    spans = [(first(i * qr), first(i * qr + qr - 1) + wh) for i in range(rows // qr)]
    size = max(hi - lo for lo, hi in spans)
    size += size % 2
    starts = [min(lo, rows - size) for lo, _ in spans]
    step = starts[1] - starts[0] if len(starts) > 1 else 0
    assert all(s == i * step and s <= lo and hi <= s + size for i, (s, (lo, hi)) in enumerate(zip(starts, spans)))
    return step, size


def _na_bias_kernel(t_ref, o_ref, *, rows, wh, qr, win_step, win_size):
    tile = pl.program_id(1)
    k0 = tile * win_step
    shape = (GRID_W, 2 * GRID_W)
    c = lax.broadcasted_iota(jnp.int32, shape, 0)
    lane = lax.broadcasted_iota(jnp.int32, shape, 1)
    kc = lane % GRID_W
    cs = jnp.clip(c - NA_WIN_W // 2, 0, GRID_W - NA_WIN_W)
    col_ok = (kc >= cs) & (kc < cs + NA_WIN_W)

    def one_row(rl, carry):
        r = tile * qr + rl
        rs = jnp.clip(r - wh // 2, 0, rows - wh)
        q_rows = pl.ds(pl.multiple_of(rl * GRID_W, GRID_W), GRID_W)
        for j in range(win_size // 2):
            kr = k0 + 2 * j + lane // GRID_W
            ok = col_ok & (kr >= rs) & (kr < rs + wh)
            d = jnp.clip(k0 + 2 * j - r + NA_WIN_H, 0, 2 * NA_WIN_H - 1)
            pair = jnp.broadcast_to(t_ref[0, pl.ds(d, 1), :], shape)
            toeplitz = pltpu.roll(pair, 2 * GRID_W - (NA_WIN_W - 1), 1, stride=1, stride_axis=0)
            o_ref[0, q_rows, j * 2 * GRID_W:(j + 1) * 2 * GRID_W] = jnp.where(ok, toeplitz * LOG2E, NEG_INF)
        return carry

    lax.fori_loop(0, qr, one_row, 0)


def _na_bias(rpb, seq, tq):
    nh, nr, nc = rpb.shape
    rows = seq // GRID_W
    wh = min(NA_WIN_H, rows)
    win_step, win_size = _na_key_window(seq, tq)
    padded = jnp.pad(rpb.astype(F32), ((0, 0), (1, 1), (0, GRID_W - nc)))
    table = jnp.concatenate([padded[:, :nr + 1], padded[:, 1:]], axis=-1)
    return pl.pallas_call(
        functools.partial(_na_bias_kernel, rows=rows, wh=wh, qr=tq // GRID_W, win_step=win_step,
                          win_size=win_size),
        grid=(nh, seq // tq),
        in_specs=[pl.BlockSpec((1, nr + 1, 2 * GRID_W), lambda h, i: (h, 0, 0))],
        out_specs=pl.BlockSpec((1, tq, win_size * GRID_W), lambda h, i: (h, i, 0)),
        out_shape=jax.ShapeDtypeStruct((nh, seq, win_size * GRID_W), F32),
        compiler_params=_params("parallel", "parallel"),
        name="na_bias",
    )(table)


def kernel(x_prompt, x_sample, cache_mla_ckv, cache_mla_krope, cache_na_k, cache_na_v, cache_gqa_k, cache_gqa_v, c, c_ctx, ada_w, ada_b, norm_g, ffn_wg, ffn_wu, ffn_wd, w_in, mla_q_norm, mla_wqb, mla_kv_norm, mla_wkvb, na_rpb, gqa_q_norm, gqa_k_norm, w_out, final_norm):
    depth = ada_w.shape[0]
    nb_ctx, seq_ctx, d = x_prompt.shape
    nb_lat, seq_lat, _ = x_sample.shape
    past = cache_mla_ckv.shape[2]

    wg = ffn_wg
    wu = ffn_wu
    wd = ffn_wd
    w_in_p = _project_layout(w_in)
    wqb =jnp.pad(mla_wqb.reshape(depth, Q_LORA, MLA_HEADS, MLA_QK),
                  ((0, 0), (0, 0), (0, 0), (0, MLA_PAD - MLA_QK))).reshape(depth, Q_LORA, MLA_HEADS * MLA_PAD)
    wkv = mla_wkvb.reshape(depth, KV_LORA, MLA_HEADS, 2 * HEAD_DIM)
    wts = {
        "q_norm": mla_q_norm.reshape(depth, 1, Q_LORA),
        "wqb": wqb.astype(BF16),
        "kv_norm": mla_kv_norm.reshape(depth, 1, KV_LORA),
        "wk": wkv[..., :MLA_NOPE].reshape(depth, KV_LORA, MLA_HEADS * MLA_NOPE).astype(BF16),
        "wv": wkv[..., MLA_NOPE:].reshape(depth, KV_LORA, MLA_HEADS * HEAD_DIM).astype(BF16),
        "gq_norm": gqa_q_norm.reshape(depth, 1, HEAD_DIM),
        "gk_norm": gqa_k_norm.reshape(depth, 1, HEAD_DIM),
    }
    w_out_b = w_out.astype(BF16)
    norm_g3 = norm_g.reshape(depth * 3, 1, d)
    final_g = final_norm.reshape(1, 1, d)
    tables = _rope_tables(seq_lat, MLA_ROPE) + _rope_tables(seq_lat, HEAD_DIM)
    kr_cache = jnp.pad(cache_mla_krope, ((0, 0), (0, 0), (0, 0), (0, LANES - MLA_ROPE)))
    caches = (cache_na_k.reshape(nb_lat, depth, past * NA_HEADS, HEAD_DIM),
              cache_na_v.reshape(nb_lat, depth, past * NA_HEADS, HEAD_DIM),
              cache_gqa_k.reshape(nb_lat, depth, past * GQA_KV_HEADS, HEAD_DIM),
              cache_gqa_v.reshape(nb_lat, depth, past * GQA_KV_HEADS, HEAD_DIM))

    cond = jnp.concatenate([c_ctx[None, :], c, jnp.zeros((MOD_ROWS - 1 - nb_lat, d), c.dtype)], axis=0)
    mods = _modulation(cond, ada_w, ada_b).reshape(depth * MOD_ROWS, 1, N_MOD * d)

    xp = x_prompt.reshape(nb_ctx * seq_ctx, d)
    xs = x_sample.reshape(nb_lat * seq_lat, d)
    ctx_row = (0, nb_ctx * seq_ctx)
    lat_row = (1, seq_lat)
    biases = [_na_bias(na_rpb[l], seq_lat, NA_TQ) for l in range(depth)]

    new_cache = None
    for l in range(depth):
        last = l == depth - 1
        xp = _ffn(xp, mods, norm_g3, l, 0, 0, ctx_row, wg, wu, wd)
        outs = _qkv(xp, mods, norm_g3, l, ctx_row, w_in_p, wts, seq_ctx, cache=(nb_ctx, depth, new_cache))
        q_mla, k_mla, v_mla, gq, gk, gv, na = outs[:N_QKV_OUT]
        new_cache = outs[N_QKV_OUT:]
        o_a, o_b, o_c = _context_attention(q_mla, k_mla, v_mla, gq, gk, gv, na, seq_ctx)
        xp = _outproj(xp, mods, l, ctx_row, o_a, o_b, o_c, w_out_b)
        xp = _ffn(xp, mods, norm_g3, l, 2, 1, ctx_row, wg, wu, wd, final_g if last else None)
        xs = _ffn(xs, mods, norm_g3, l, 0, 0, lat_row, wg, wu, wd)
        q_mla, k_mla, v_mla, gq, gk, gv, na = _qkv(xs, mods, norm_g3, l, lat_row, w_in_p, wts, 256, tables=tables)
        kc_mla, vc_mla = _cache_kv(cache_mla_ckv, kr_cache, l, wts)
        o_a, o_b, o_c = _latent_attention(q_mla, k_mla, v_mla, gq, gk, gv, na, kc_mla, vc_mla, caches, biases[l], l,
                                          seq_lat)
        xs = _outproj(xs, mods, l, lat_row, o_a, o_b, o_c, w_out_b)
        xs = _ffn(xs, mods, norm_g3, l, 2, 1, lat_row, wg, wu, wd, final_g if last else None)

    y_prompt = xp.reshape(nb_ctx, seq_ctx, d)
    y_sample = xs.reshape(nb_lat, seq_lat, d)
    ckv_c, kr_c, nak_c, nav_c, gk_c, gv_c = new_cache
    na_shape = (nb_ctx, depth, seq_ctx, NA_HEADS, HEAD_DIM)
    gqa_shape = (nb_ctx, depth, seq_ctx, GQA_KV_HEADS, HEAD_DIM)
    return (y_prompt, y_sample, ckv_c, kr_c, nak_c.reshape(na_shape), nav_c.reshape(na_shape),
            gk_c.reshape(gqa_shape), gv_c.reshape(gqa_shape))
```

```python
import functools

import jax
import jax.numpy as jnp
import numpy as np
from jax import lax
from jax.experimental import pallas as pl
from jax.experimental.pallas import tpu as pltpu

F32 = jnp.float32
BF16 = jnp.bfloat16

D_MODEL = 2048
D_FF = 5632
N_MOD = 9
GRID_W = 64
ROPE_THETA = 10000.0
HEAD_DIM = 128
MLA_HEADS = 8
MLA_NOPE = 128
MLA_ROPE = 64
MLA_QK = MLA_NOPE + MLA_ROPE
MLA_PAD = 256
Q_LORA = 512
KV_LORA = 256
NA_HEADS = 4
NA_WIN_H = 8
NA_WIN_W = 16
NA_TQ = 512
GQA_HEADS = 4
GQA_KV_HEADS = 2
EPS = 1e-6
NEG_INF = -1e30
LOG2E = 1.4426950408889634
MOD_ROWS = 16

LANES = 128
COL_CQ = 0
COL_CKV = COL_CQ + Q_LORA
COL_KR = COL_CKV + KV_LORA
COL_NA = COL_KR + LANES
COL_GQ = COL_NA + 3 * NA_HEADS * HEAD_DIM
COL_GK = COL_GQ + GQA_HEADS * HEAD_DIM
COL_GV = COL_GK + GQA_KV_HEADS * HEAD_DIM
PROJ_W = COL_GV + GQA_KV_HEADS * HEAD_DIM

V7X_VMEM_BYTES = 64 * 1024 * 1024
VMEM_LIMIT = V7X_VMEM_BYTES - 2 * 1024 * 1024


def _params(*sem):
    return pltpu.CompilerParams(dimension_semantics=sem, vmem_limit_bytes=VMEM_LIMIT)


def _resident(shape, index_map):
    return pl.BlockSpec(shape, index_map, pipeline_mode=pl.Buffered(1))


def _as_bf16(ref):
    return ref[...].astype(BF16)


def _rms(x, g):
    return x * lax.rsqrt(jnp.mean(x * x, axis=-1, keepdims=True) + EPS) * g


NORM_ROWS = 256


def _norm_modulate_store(x_ref, ng_ref, sc_ref, sh_ref, h_scr):
    gain = ng_ref[0] * (1.0 + sc_ref[0])
    shift = sh_ref[0]

    def body(i, carry):
        rows = pl.ds(pl.multiple_of(i * NORM_ROWS, NORM_ROWS), NORM_ROWS)
        x = x_ref[rows, :]
        r = lax.rsqrt(jnp.mean(x * x, axis=-1, keepdims=True) + EPS)
        h_scr[rows, :] = (x * r * gain + shift).astype(BF16)
        return carry

    lax.fori_loop(0, x_ref.shape[0] // NORM_ROWS, body, 0)


def _rope(x, cos, sin, quarter):
    n = x.shape[-1]
    lane = lax.broadcasted_iota(jnp.int32, x.shape, 1)
    first = (lane % (2 * quarter)) < quarter
    sw = jnp.where(first, pltpu.roll(x, n - quarter, 1), pltpu.roll(x, quarter, 1))
    return x * cos + sw * sin


def _mod_kernel(c_ref, w_ref, b_ref, o_ref):
    c = c_ref[...]
    s = (c * jax.nn.sigmoid(c)).astype(BF16)
    o_ref[0] = jnp.dot(s, w_ref[0].astype(BF16), preferred_element_type=F32) + b_ref[0]


def _modulation(cond, ada_w, ada_b):
    depth, d, n = ada_w.shape
    tn = 1024
    return pl.pallas_call(
        _mod_kernel,
        grid=(depth, n // tn),
        in_specs=[
            pl.BlockSpec((MOD_ROWS, d), lambda l, j: (0, 0)),
            pl.BlockSpec((1, d, tn), lambda l, j: (l, 0, j)),
            pl.BlockSpec((1, 1, tn), lambda l, j: (l, 0, j)),
        ],
        out_specs=pl.BlockSpec((1, MOD_ROWS, tn), lambda l, j: (l, 0, j)),
        out_shape=jax.ShapeDtypeStruct((depth, MOD_ROWS, n), F32),
        compiler_params=_params("parallel", "parallel"),
        name="modulation",
    )(cond, ada_w, ada_b.reshape(depth, 1, n))


def _mod_spec(l, chunk, stream, tm):
    base, tokens = stream
    return pl.BlockSpec((1, 1, D_MODEL), lambda i, *_: (l * MOD_ROWS + base + (i * tm) // tokens, 0, chunk))


def _ffn_kernel(x_ref, ng_ref, sh_ref, sc_ref, gt_ref, wg_ref, wu_ref, wd_ref, *rest, final):
    if final:
        fg_ref, o_ref, h_scr = rest
    else:
        o_ref, h_scr = rest
    f = pl.program_id(1)
    tm = x_ref.shape[0]

    def swiglu(rows, wg, wu, wd):
        h = h_scr[rows, :]
        g = jnp.dot(h, wg, preferred_element_type=F32)
        u = jnp.dot(h, wu, preferred_element_type=F32)
        a = (g * jax.nn.sigmoid(g) * u).astype(BF16)
        return jnp.dot(a, wd, preferred_element_type=F32)

    @pl.when(f == 0)
    def _():
        wg, wu, wd = _as_bf16(wg_ref), _as_bf16(wu_ref), _as_bf16(wd_ref)
        gain = ng_ref[0] * (1.0 + sc_ref[0])
        shift = sh_ref[0]
        for half in range(2):
            for r0 in range(half * tm // 2, (half + 1) * tm // 2, NORM_ROWS):
                x = x_ref[r0:r0 + NORM_ROWS, :]
                r = lax.rsqrt(jnp.mean(x * x, axis=-1, keepdims=True) + EPS)
                h_scr[r0:r0 + NORM_ROWS, :] = (x * r * gain + shift).astype(BF16)
            rows = slice(half * tm // 2, (half + 1) * tm // 2)
            o_ref[rows, :] = swiglu(rows, wg, wu, wd)

    last = pl.num_programs(1) - 1

    @pl.when((f > 0) & ((f < last) | final))
    def _():
        o_ref[...] += swiglu(slice(None), _as_bf16(wg_ref), _as_bf16(wu_ref), _as_bf16(wd_ref))

    @pl.when(f == last)
    def _():
        gate = 0.5 * gt_ref[0]
        if final:
            def body(i, carry):
                rows = pl.ds(pl.multiple_of(i * NORM_ROWS, NORM_ROWS), NORM_ROWS)
                o_ref[rows, :] = _rms(x_ref[rows, :] + gate * o_ref[rows, :], fg_ref[0])
                return carry

            lax.fori_loop(0, tm // NORM_ROWS, body, 0)
        else:
            wg, wu, wd = _as_bf16(wg_ref), _as_bf16(wu_ref), _as_bf16(wd_ref)
            for half in range(2):
                rows = slice(half * tm // 2, (half + 1) * tm // 2)
                o_ref[rows, :] += swiglu(rows, wg, wu, wd)
                for r0 in range(half * tm // 2, (half + 1) * tm // 2, NORM_ROWS):
                    o_ref[r0:r0 + NORM_ROWS, :] = x_ref[r0:r0 + NORM_ROWS, :] + gate * o_ref[r0:r0 + NORM_ROWS, :]


def _ffn(x, mods, norm_g, l, sub, k, stream, wg, wu, wd, final_g=None):
    t, d = x.shape
    tm, tf = 1024, 512
    chunk = 3 * sub
    in_specs = [
        pl.BlockSpec((tm, d), lambda i, j: (i, 0), pipeline_mode=pl.Buffered(1)),
        pl.BlockSpec((1, 1, d), lambda i, j: (l * 3 + sub, 0, 0)),
        _mod_spec(l, chunk, stream, tm),
        _mod_spec(l, chunk + 1, stream, tm),
        _mod_spec(l, chunk + 2, stream, tm),
        pl.BlockSpec((None, None, d, tf), lambda i, j: (l, k, 0, j)),
        pl.BlockSpec((None, None, d, tf), lambda i, j: (l, k, 0, j)),
        pl.BlockSpec((None, None, tf, d), lambda i, j: (l, k, j, 0)),
    ]
    args = [x, norm_g, mods, mods, mods, wg, wu, wd]
    if final_g is not None:
        in_specs.append(pl.BlockSpec((1, 1, d), lambda i, j: (0, 0, 0)))
        args.append(final_g)
    return pl.pallas_call(
        functools.partial(_ffn_kernel, final=final_g is not None),
        grid=(t // tm, D_FF // tf),
        in_specs=in_specs,
        out_specs=pl.BlockSpec((tm, d), lambda i, j: (i, 0)),
        out_shape=jax.ShapeDtypeStruct((t, d), F32),
        scratch_shapes=[pltpu.VMEM((tm, d), BF16)],
        compiler_params=_params("parallel", "arbitrary"),
        name="ffn",
    )(*args)


def _expand_kv(ckv_bf, kr, wk_ref, wv_ref, k_out, v_out):
    kn = jnp.dot(ckv_bf, wk_ref[...], preferred_element_type=F32)
    v_out[...] = jnp.dot(ckv_bf, wv_ref[...], preferred_element_type=F32).astype(BF16)
    kr_bf = kr.astype(BF16)
    for h in range(MLA_HEADS):
        k_out[:, h * MLA_PAD:h * MLA_PAD + MLA_NOPE] = kn[:, h * MLA_NOPE:(h + 1) * MLA_NOPE].astype(BF16)
        k_out[:, h * MLA_PAD + MLA_NOPE:(h + 1) * MLA_PAD] = kr_bf


N_QKV_OUT = 7
N_CACHE_OUT = 6


def _qkv_kernel(x_ref, ng_ref, sh_ref, sc_ref, w_ref, qn_ref, wqb_ref, kvn_ref, wk_ref, wv_ref, gqn_ref, gkn_ref,
                *rest, rope, n_prev, cache_out, layer):
    if rope:
        c64_ref, s64_ref, c128_ref, s128_ref = rest[:4]
        rest = rest[4:]
        rot64 = lambda v: _rope(v, c64_ref[...], s64_ref[...], MLA_ROPE // 4)
        rot128 = lambda v: _rope(v, c128_ref[...], s128_ref[...], HEAD_DIM // 4)
    else:
        rot64 = rot128 = lambda v: v
    rest = rest[n_prev:]
    q_out, k_out, v_out, gq_out, gk_out, gv_out, na_out = rest[:N_QKV_OUT]
    rest = rest[N_QKV_OUT:]
    if cache_out:
        cache_refs = rest[:N_CACHE_OUT]
        rest = rest[N_CACHE_OUT:]
        if n_prev == 0:
            for ref in cache_refs:
                for other in range(ref.shape[0]):
                    if other != layer:
                        ref[other] = jnp.zeros(ref.shape[1:], ref.dtype)
            cache_refs = [ref.at[layer] for ref in cache_refs]
        ckv_c, kr_c, nak_c, nav_c, gk_c, gv_c = cache_refs
    (h_scr,) = rest
    tm = x_ref.shape[0]

    _norm_modulate_store(x_ref, ng_ref, sc_ref, sh_ref, h_scr)
    h = h_scr[...]
    low = jnp.dot(h, w_ref[:, COL_CQ:COL_NA], preferred_element_type=F32)
    na = jnp.dot(h, w_ref[:, COL_NA:COL_GQ], preferred_element_type=F32)
    gg = jnp.dot(h, w_ref[:, COL_GQ:PROJ_W], preferred_element_type=F32)

    qn = _rms(low[:, COL_CQ:COL_CKV], qn_ref[0]).astype(BF16)
    q = jnp.dot(qn, wqb_ref[...], preferred_element_type=F32) * (MLA_QK ** -0.5 * LOG2E)
    for hh in range(MLA_HEADS):
        lo = hh * MLA_PAD
        q_out[:, lo:lo + MLA_NOPE] = q[:, lo:lo + MLA_NOPE].astype(BF16)
        q_out[:, lo + MLA_NOPE:lo + MLA_PAD] = rot64(q[:, lo + MLA_NOPE:lo + MLA_PAD]).astype(BF16)

    ckv = _rms(low[:, COL_CKV:COL_KR], kvn_ref[0])
    kr = rot64(low[:, COL_KR:COL_NA])
    _expand_kv(ckv.astype(BF16), kr, wk_ref, wv_ref, k_out, v_out)

    nw = NA_HEADS * HEAD_DIM
    gw = GQA_HEADS * HEAD_DIM
    kw = GQA_KV_HEADS * HEAD_DIM
    na_out[:, :nw] = (na[:, :nw] * (HEAD_DIM ** -0.5 * LOG2E)).astype(BF16)
    na_out[:, nw:] = na[:, nw:].astype(BF16)
    gq_gain = gqn_ref[0] * (HEAD_DIM ** -0.5 * LOG2E)
    for hh in range(GQA_HEADS):
        sl = slice(hh * HEAD_DIM, (hh + 1) * HEAD_DIM)
        gq_out[:, sl] = rot128(_rms(gg[:, sl], gq_gain)).astype(BF16)
    gv = gg[:, gw + kw:]
    gv_out[...] = gv.astype(BF16)
    for hh in range(GQA_KV_HEADS):
        sl = slice(hh * HEAD_DIM, (hh + 1) * HEAD_DIM)
        gk = rot128(_rms(gg[:, gw + hh * HEAD_DIM:gw + (hh + 1) * HEAD_DIM], gkn_ref[0]))
        gk_out[:, sl] = gk.astype(BF16)
        if cache_out:
            gk_c[pl.ds(hh, tm, stride=GQA_KV_HEADS), :] = gk
            gv_c[pl.ds(hh, tm, stride=GQA_KV_HEADS), :] = gv[:, sl]
    if cache_out:
        ckv_c[...] = ckv
        kr_c[...] = kr[:, :MLA_ROPE]
        for hh in range(NA_HEADS):
            sl = slice(hh * HEAD_DIM, (hh + 1) * HEAD_DIM)
            nak_c[pl.ds(hh, tm, stride=NA_HEADS), :] = na[:, nw + hh * HEAD_DIM:nw + (hh + 1) * HEAD_DIM]
            nav_c[pl.ds(hh, tm, stride=NA_HEADS), :] = na[:, 2 * nw + hh * HEAD_DIM:2 * nw + (hh + 1) * HEAD_DIM]


def _cache_shapes(nb, depth, seq):
    return [
        (nb, depth, seq, KV_LORA), (nb, depth, seq, MLA_ROPE),
        (nb, depth, seq * NA_HEADS, HEAD_DIM), (nb, depth, seq * NA_HEADS, HEAD_DIM),
        (nb, depth, seq * GQA_KV_HEADS, HEAD_DIM), (nb, depth, seq * GQA_KV_HEADS, HEAD_DIM),
    ]


def _qkv(x, mods, norm_g, l, stream, w_in, wts, tm, tables=None, cache=None):
    t, d = x.shape
    rope = tables is not None
    qw = MLA_HEADS * MLA_PAD
    vw = MLA_HEADS * HEAD_DIM
    in_specs = [
        pl.BlockSpec((tm, d), lambda i: (i, 0)),
        pl.BlockSpec((1, 1, d), lambda i: (l * 3 + 1, 0, 0)),
        _mod_spec(l, 3, stream, tm),
        _mod_spec(l, 4, stream, tm),
        _resident((None, d, PROJ_W), lambda i: (l, 0, 0)),
        pl.BlockSpec((1, 1, Q_LORA), lambda i: (l, 0, 0)),
        _resident((None, Q_LORA, qw), lambda i: (l, 0, 0)),
        pl.BlockSpec((1, 1, KV_LORA), lambda i: (l, 0, 0)),
        _resident((None, KV_LORA, vw), lambda i: (l, 0, 0)),
        _resident((None, KV_LORA, vw), lambda i: (l, 0, 0)),
        pl.BlockSpec((1, 1, HEAD_DIM), lambda i: (l, 0, 0)),
        pl.BlockSpec((1, 1, HEAD_DIM), lambda i: (l, 0, 0)),
    ]
    args = [x, norm_g, mods, mods, w_in, wts["q_norm"], wts["wqb"], wts["kv_norm"], wts["wk"], wts["wv"],
            wts["gq_norm"], wts["gk_norm"]]
    if rope:
        per = tables[0].shape[0] // tm
        in_specs += [pl.BlockSpec((tm, LANES), lambda i: (i % per, 0))] * 4
        args += list(tables)
    row = lambda w: pl.BlockSpec((tm, w), lambda i: (i, 0))
    widths = [qw, qw, vw, GQA_HEADS * HEAD_DIM, GQA_KV_HEADS * HEAD_DIM, GQA_KV_HEADS * HEAD_DIM,
              3 * NA_HEADS * HEAD_DIM]
    out_specs = [row(w) for w in widths]
    out_shape = [jax.ShapeDtypeStruct((t, w), BF16) for w in widths]
    aliases = {}
    n_prev = 0
    if cache is not None:
        nb, depth, prev = cache
        assert tm * nb == t
        shapes = _cache_shapes(nb, depth, tm)
        if prev is not None:
            n_prev = len(prev)
            aliases = {len(args) + k: N_QKV_OUT + k for k in range(n_prev)}
            in_specs += [pl.BlockSpec(memory_space=pl.ANY)] * n_prev
            args += list(prev)
            out_specs += [pl.BlockSpec((None, None) + s[2:], lambda i: (i, l, 0, 0)) for s in shapes]
        else:
            out_specs += [pl.BlockSpec((None,) + s[1:], lambda i: (i, 0, 0, 0)) for s in shapes]
        out_shape += [jax.ShapeDtypeStruct(s, F32) for s in shapes]
    return pl.pallas_call(
        functools.partial(_qkv_kernel, rope=rope, n_prev=n_prev, cache_out=cache is not None, layer=l),
        grid=(t // tm,),
        in_specs=in_specs,
        out_specs=out_specs,
        out_shape=out_shape,
        input_output_aliases=aliases,
        scratch_shapes=[pltpu.VMEM((tm, d), BF16)],
        compiler_params=_params("parallel"),
        name="qkv",
    )(*args)


def _cache_kv_kernel(ckv_ref, kr_ref, wk_ref, wv_ref, k_out, v_out):
    _expand_kv(ckv_ref[...].astype(BF16), kr_ref[...], wk_ref, wv_ref, k_out, v_out)


def _cache_kv(cache_ckv, cache_kr_pad, l, wts):
    nb, _, past, _ = cache_ckv.shape
    qw = MLA_HEADS * MLA_PAD
    vw = MLA_HEADS * HEAD_DIM
    return pl.pallas_call(
        _cache_kv_kernel,
        grid=(nb,),
        in_specs=[
            pl.BlockSpec((None, None, past, KV_LORA), lambda b: (b, l, 0, 0)),
            pl.BlockSpec((None, None, past, LANES), lambda b: (b, l, 0, 0)),
            pl.BlockSpec((None, KV_LORA, vw), lambda b: (l, 0, 0)),
            pl.BlockSpec((None, KV_LORA, vw), lambda b: (l, 0, 0)),
        ],
        out_specs=[pl.BlockSpec((past, qw), lambda b: (b, 0)), pl.BlockSpec((past, vw), lambda b: (b, 0))],
        out_shape=[jax.ShapeDtypeStruct((nb * past, qw), BF16), jax.ShapeDtypeStruct((nb * past, vw), BF16)],
        compiler_params=_params("parallel"),
        name="cache_kv",
    )(cache_ckv, cache_kr_pad, wts["wk"], wts["wv"])


def _attn_kernel(*refs, n_heads, group, dk, dv, seg_kinds, n_sub):
    q_ref = refs[0]
    pos = 1
    segs = []
    windows = [w for _, _, w in seg_kinds]
    for has_bias, inter, _ in seg_kinds:
        b_ref = refs[pos + 2] if has_bias else None
        segs.append((refs[pos], refs[pos + 1], b_ref, inter))
        pos += 3 if has_bias else 2
    o_ref = refs[pos]

    def head_rows(ref, rows, hk, width, inter):
        if inter:
            return ref[pl.ds(hk, ref.shape[0] // inter, stride=inter), :]
        return ref[rows, hk * width:(hk + 1) * width]

    def attend(q_rows, seg_rows):
        for h in range(n_heads):
            hk = h // group
            q = q_ref[q_rows, h * dk:(h + 1) * dk].astype(BF16)
            scores = []
            for (k_ref, _, b_ref, inter), rows in zip(segs, seg_rows):
                k = head_rows(k_ref, rows, hk, dk, inter).astype(BF16)
                s = lax.dot_general(q, k, (((1,), (1,)), ((), ())), preferred_element_type=F32)
                if b_ref is not None:
                    s = s + b_ref[h]
                scores.append(s)
            m = functools.reduce(jnp.maximum, [s.max(axis=-1, keepdims=True) for s in scores])
            o = None
            for s, (_, v_ref, _, inter), rows in zip(scores, segs, seg_rows):
                p = jnp.exp2(s - m).astype(BF16)
                v = head_rows(v_ref, rows, hk, dv, inter).astype(BF16)
                pv = jnp.dot(p, jnp.concatenate([v, jnp.ones_like(v)], axis=1), preferred_element_type=F32)
                o = pv if o is None else o + pv
            o_ref[q_rows, h * dv:(h + 1) * dv] = (o[:, :dv] / o[:, dv:]).astype(o_ref.dtype)

    if n_sub == 1:
        attend(slice(None), [slice(None) if w is None else
                             pl.ds(pl.multiple_of(pl.program_id(1) * w[0], w[0]), w[1]) for w in windows])
    else:
        def body(s, carry):
            rows = lambda ref: pl.ds(pl.multiple_of(s * (ref.shape[0] // n_sub), ref.shape[0] // n_sub),
                                     ref.shape[0] // n_sub)
            attend(rows(q_ref), [rows(seg[0]) for seg in segs])
            return carry

        lax.fori_loop(0, n_sub, body, 0, unroll=True)


def _attention(q, segs, *, t, grid, tq, q_index, n_heads, group, dk, dv, name, n_sub=1):
    assert n_sub == 1 or all(s[4] is None and not s[6] and len(s) == 7 for s in segs)
    in_specs = [pl.BlockSpec((tq, n_heads * dk), q_index)]
    args = [q]
    seg_kinds = []
    for k_arr, k_spec, v_arr, v_spec, b_arr, b_spec, inter, *window in segs:
        in_specs += [k_spec, v_spec]
        args += [k_arr, v_arr]
        seg_kinds.append((b_arr is not None, inter, window[0] if window else None))
        if b_arr is not None:
            in_specs.append(b_spec)
            args.append(b_arr)
    out_index = lambda *g: (q_index(*g)[0], 0)
    return pl.pallas_call(
        functools.partial(_attn_kernel, n_heads=n_heads, group=group, dk=dk, dv=dv,
                          seg_kinds=tuple(seg_kinds), n_sub=n_sub),
        grid=grid,
        in_specs=in_specs,
        out_specs=pl.BlockSpec((tq, n_heads * dv), out_index),
        out_shape=jax.ShapeDtypeStruct((t, n_heads * dv), BF16),
        compiler_params=_params(*(("parallel",) * len(grid))),
        name=name,
    )(*args)


def _context_attention(q_mla, k_mla, v_mla, gq, gk, gv, na, seq):
    t = na.shape[0]
    n_sub = 4
    rows = seq * n_sub
    grid = (t // rows,)
    blk = lambda w, c: pl.BlockSpec((rows, w), lambda b: (b, c))
    common = dict(t=t, grid=grid, tq=rows, n_sub=n_sub)
    nw = NA_HEADS * HEAD_DIM
    kw = GQA_KV_HEADS * HEAD_DIM
    o_a = _attention(q_mla, [(k_mla, blk(MLA_HEADS * MLA_PAD, 0), v_mla, blk(MLA_HEADS * HEAD_DIM, 0), None, None, 0)],
                     q_index=lambda b: (b, 0), n_heads=MLA_HEADS, group=1, dk=MLA_PAD, dv=HEAD_DIM,
                     name="ctx_mla", **common)
    o_b = _attention(na, [(na, blk(nw, 1), na, blk(nw, 2), None, None, 0)],
                     q_index=lambda b: (b, 0), n_heads=NA_HEADS, group=1, dk=HEAD_DIM, dv=HEAD_DIM,
                     name="ctx_na", **common)
    o_c = _attention(gq, [(gk, blk(kw, 0), gv, blk(kw, 0), None, None, 0)],
                     q_index=lambda b: (b, 0), n_heads=GQA_HEADS, group=GQA_HEADS // GQA_KV_HEADS,
                     dk=HEAD_DIM, dv=HEAD_DIM, name="ctx_gqa", **common)
    return o_a, o_b, o_c


def _latent_attention(q_mla, k_mla, v_mla, gq, gk, gv, na, kc_mla, vc_mla, caches, bias, l, seq):
    cache_na_k, cache_na_v, cache_gqa_k, cache_gqa_v = caches
    t = na.shape[0]
    past = kc_mla.shape[0] // (t // seq)
    own = lambda w, c: pl.BlockSpec((seq, w), lambda b, i: (b, c))
    flat = lambda w: pl.BlockSpec((past, w), lambda b, i: (b, 0))
    cached = lambda heads: pl.BlockSpec((None, None, past * heads, HEAD_DIM), lambda b, i: (b, l, 0, 0))
    qw, vw = MLA_HEADS * MLA_PAD, MLA_HEADS * HEAD_DIM
    nw = NA_HEADS * HEAD_DIM
    kw = GQA_KV_HEADS * HEAD_DIM
    o_c = _attention(gq, [(cache_gqa_k, cached(GQA_KV_HEADS), cache_gqa_v, cached(GQA_KV_HEADS), None, None,
                           GQA_KV_HEADS),
                          (gk, own(kw, 0), gv, own(kw, 0), None, None, 0)],
                     q_index=lambda b, i: (b, 0), n_heads=GQA_HEADS, group=GQA_HEADS // GQA_KV_HEADS,
                     dk=HEAD_DIM, dv=HEAD_DIM, name="lat_gqa",
                     t=t, grid=(t // seq, 1), tq=seq)
    o_a = _attention(q_mla, [(kc_mla, flat(qw), vc_mla, flat(vw), None, None, 0),
                             (k_mla, own(qw, 0), v_mla, own(vw, 0), None, None, 0)],
                     q_index=lambda b, i: (b, 0), n_heads=MLA_HEADS, group=1, dk=MLA_PAD, dv=HEAD_DIM,
                     name="lat_mla", t=t, grid=(t // seq, 1), tq=seq)
    tq = NA_TQ
    nq = seq // tq
    common = dict(t=t, grid=(t // seq, nq), tq=tq)
    win_step, win_size = _na_key_window(seq, tq)
    window = (win_step * GRID_W, win_size * GRID_W)
    bias_spec = pl.BlockSpec((NA_HEADS, tq, window[1]), lambda b, i: (0, i, 0))
    o_b = _attention(na, [(cache_na_k, cached(NA_HEADS), cache_na_v, cached(NA_HEADS), None, None, NA_HEADS),
                          (na, own(nw, 1), na, own(nw, 2), bias, bias_spec, 0, window)],
                     q_index=lambda b, i: (b * nq + i, 0), n_heads=NA_HEADS, group=1,
                     dk=HEAD_DIM, dv=HEAD_DIM, name="lat_na", **common)
    return o_a, o_b, o_c


def _outproj_kernel(x_ref, gt_ref, oa_ref, ob_ref, oc_ref, w_ref, o_ref):
    wa = oa_ref.shape[1]
    wb = ob_ref.shape[1]
    acc = jnp.dot(oa_ref[...], w_ref[0:wa, :], preferred_element_type=F32)
    acc += jnp.dot(ob_ref[...], w_ref[wa:wa + wb, :], preferred_element_type=F32)
    acc += jnp.dot(oc_ref[...], w_ref[wa + wb:, :], preferred_element_type=F32)
    o_ref[...] = x_ref[...] + gt_ref[0] * acc


def _outproj(x, mods, l, stream, o_a, o_b, o_c, w_out):
    t, d = x.shape
    tm = 512
    row = lambda a: pl.BlockSpec((tm, a.shape[1]), lambda i: (i, 0))
    return pl.pallas_call(
        _outproj_kernel,
        grid=(t // tm,),
        in_specs=[
            row(x),
            _mod_spec(l, 5, stream, tm),
            row(o_a), row(o_b), row(o_c),
            _resident((None, d, d), lambda i: (l, 0, 0)),
        ],
        out_specs=row(x),
        out_shape=jax.ShapeDtypeStruct((t, d), F32),
        compiler_params=_params("parallel"),
        name="outproj",
    )(x, mods, o_a, o_b, o_c, w_out)


def _project_layout_kernel(w_ref, o_ref):
    split = COL_KR + MLA_ROPE
    o_ref[:, :split] = w_ref[:, :split].astype(BF16)
    o_ref[:, split:COL_NA] = jnp.zeros((o_ref.shape[0], COL_NA - split), BF16)
    o_ref[:, COL_NA:] = w_ref[:, split:].astype(BF16)


def _project_layout(w_in):
    depth, d, n = w_in.shape
    tk = 256
    return pl.pallas_call(
        _project_layout_kernel,
        grid=(depth, d // tk),
        in_specs=[pl.BlockSpec((None, tk, n), lambda l, i: (l, i, 0))],
        out_specs=pl.BlockSpec((None, tk, PROJ_W), lambda l, i: (l, i, 0)),
        out_shape=jax.ShapeDtypeStruct((depth, d, PROJ_W), BF16),
        compiler_params=_params("parallel", "parallel"),
        name="project_layout",
    )(w_in)


def _rope_tables(seq, d):
    quarter = d // 4
    tt = np.arange(seq)
    pos = np.stack([tt // GRID_W, tt % GRID_W], axis=-1).astype(np.float64)
    inv = ROPE_THETA ** (-np.arange(quarter, dtype=np.float64) / quarter)
    ang = pos[:, :, None] * inv
    cos = np.cos(ang)
    sin = np.sin(ang)
    cos_t = np.stack([cos, cos], axis=2).reshape(seq, d)
    sin_t = np.stack([-sin, sin], axis=2).reshape(seq, d)
    pad = LANES - d
    if pad:
        cos_t = np.concatenate([cos_t, np.ones((seq, pad))], axis=-1)
        sin_t = np.concatenate([sin_t, np.zeros((seq, pad))], axis=-1)
    return jnp.asarray(cos_t, F32), jnp.asarray(sin_t, F32)


def _na_key_window(seq, tq):
    rows = seq // GRID_W
    wh = min(NA_WIN_H, rows)
    qr = tq // GRID_W
    first = lambda r: min(max(r - wh // 2, 0), rows - wh)
    spans = [(first(i * qr), first(i * qr + qr - 1) + wh) for i in range(rows // qr)]
    size = max(hi - lo for lo, hi in spans)
    size += size % 2
    starts = [min(lo, rows - size) for lo, _ in spans]
    step = starts[1] - starts[0] if len(starts) > 1 else 0
    assert all(s == i * step and s <= lo and hi <= s + size for i, (s, (lo, hi)) in enumerate(zip(starts, spans)))
    return step, size


def _na_bias_kernel(t_ref, o_ref, *, rows, wh, qr, win_step, win_size):
    tile = pl.program_id(1)
    k0 = tile * win_step
    shape = (GRID_W, 2 * GRID_W)
    c = lax.broadcasted_iota(jnp.int32, shape, 0)
    lane = lax.broadcasted_iota(jnp.int32, shape, 1)
    kc = lane % GRID_W
    cs = jnp.clip(c - NA_WIN_W // 2, 0, GRID_W - NA_WIN_W)
    col_ok = (kc >= cs) & (kc < cs + NA_WIN_W)

    def one_row(rl, carry):
        r = tile * qr + rl
        rs = jnp.clip(r - wh // 2, 0, rows - wh)
        q_rows = pl.ds(pl.multiple_of(rl * GRID_W, GRID_W), GRID_W)
        for j in range(win_size // 2):
            kr = k0 + 2 * j + lane // GRID_W
            ok = col_ok & (kr >= rs) & (kr < rs + wh)
            d = jnp.clip(k0 + 2 * j - r + NA_WIN_H, 0, 2 * NA_WIN_H - 1)
            pair = jnp.broadcast_to(t_ref[0, pl.ds(d, 1), :], shape)
            toeplitz = pltpu.roll(pair, 2 * GRID_W - (NA_WIN_W - 1), 1, stride=1, stride_axis=0)
            o_ref[0, q_rows, j * 2 * GRID_W:(j + 1) * 2 * GRID_W] = jnp.where(ok, toeplitz * LOG2E, NEG_INF)
        return carry

    lax.fori_loop(0, qr, one_row, 0)


def _na_bias(rpb, seq, tq):
    nh, nr, nc = rpb.shape
    rows = seq // GRID_W
    wh = min(NA_WIN_H, rows)
    win_step, win_size = _na_key_window(seq, tq)
    padded = jnp.pad(rpb.astype(F32), ((0, 0), (1, 1), (0, GRID_W - nc)))
    table = jnp.concatenate([padded[:, :nr + 1], padded[:, 1:]], axis=-1)
    return pl.pallas_call(
        functools.partial(_na_bias_kernel, rows=rows, wh=wh, qr=tq // GRID_W, win_step=win_step,
                          win_size=win_size),
        grid=(nh, seq // tq),
        in_specs=[pl.BlockSpec((1, nr + 1, 2 * GRID_W), lambda h, i: (h, 0, 0))],
        out_specs=pl.BlockSpec((1, tq, win_size * GRID_W), lambda h, i: (h, i, 0)),
        out_shape=jax.ShapeDtypeStruct((nh, seq, win_size * GRID_W), F32),
        compiler_params=_params("parallel", "parallel"),
        name="na_bias",
    )(table)


def kernel(x_prompt, x_sample, cache_mla_ckv, cache_mla_krope, cache_na_k, cache_na_v, cache_gqa_k, cache_gqa_v, c, c_ctx, ada_w, ada_b, norm_g, ffn_wg, ffn_wu, ffn_wd, w_in, mla_q_norm, mla_wqb, mla_kv_norm, mla_wkvb, na_rpb, gqa_q_norm, gqa_k_norm, w_out, final_norm):
    depth = ada_w.shape[0]
    nb_ctx, seq_ctx, d = x_prompt.shape
    nb_lat, seq_lat, _ = x_sample.shape
    past = cache_mla_ckv.shape[2]

    w_in_p = _project_layout(w_in)
    wqb = jnp.pad(mla_wqb.reshape(depth, Q_LORA, MLA_HEADS, MLA_QK),
                  ((0, 0), (0, 0), (0, 0), (0, MLA_PAD - MLA_QK))).reshape(depth, Q_LORA, MLA_HEADS * MLA_PAD)
    wkv = mla_wkvb.reshape(depth, KV_LORA, MLA_HEADS, 2 * HEAD_DIM)
    wts = {
        "q_norm": mla_q_norm.reshape(depth, 1, Q_LORA),
        "wqb": wqb.astype(BF16),
        "kv_norm": mla_kv_norm.reshape(depth, 1, KV_LORA),
        "wk": wkv[..., :MLA_NOPE].reshape(depth, KV_LORA, MLA_HEADS * MLA_NOPE).astype(BF16),
        "wv": wkv[..., MLA_NOPE:].reshape(depth, KV_LORA, MLA_HEADS * HEAD_DIM).astype(BF16),
        "gq_norm": gqa_q_norm.reshape(depth, 1, HEAD_DIM),
        "gk_norm": gqa_k_norm.reshape(depth, 1, HEAD_DIM),
    }
    w_out_b = w_out.astype(BF16)
    norm_g3 = norm_g.reshape(depth * 3, 1, d)
    final_g = final_norm.reshape(1, 1, d)
    tables = _rope_tables(seq_lat, MLA_ROPE) + _rope_tables(seq_lat, HEAD_DIM)
    kr_cache = jnp.pad(cache_mla_krope, ((0, 0), (0, 0), (0, 0), (0, LANES - MLA_ROPE)))
    caches = (cache_na_k.reshape(nb_lat, depth, past * NA_HEADS, HEAD_DIM),
              cache_na_v.reshape(nb_lat, depth, past * NA_HEADS, HEAD_DIM),
              cache_gqa_k.reshape(nb_lat, depth, past * GQA_KV_HEADS, HEAD_DIM),
              cache_gqa_v.reshape(nb_lat, depth, past * GQA_KV_HEADS, HEAD_DIM))

    cond = jnp.concatenate([c_ctx[None, :], c, jnp.zeros((MOD_ROWS - 1 - nb_lat, d), c.dtype)], axis=0)
    mods = _modulation(cond, ada_w, ada_b).reshape(depth * MOD_ROWS, 1, N_MOD * d)

    xp = x_prompt.reshape(nb_ctx * seq_ctx, d)
    xs = x_sample.reshape(nb_lat * seq_lat, d)
    ctx_row = (0, nb_ctx * seq_ctx)
    lat_row = (1, seq_lat)
    biases = [_na_bias(na_rpb[l], seq_lat, NA_TQ) for l in range(depth)]
    ffn_w = (ffn_wg, ffn_wu, ffn_wd)

    new_cache = None
    for l in range(depth):
        last = l == depth - 1
        xp = _ffn(xp, mods, norm_g3, l, 0, 0, ctx_row, *ffn_w)
        outs = _qkv(xp, mods, norm_g3, l, ctx_row, w_in_p, wts, seq_ctx, cache=(nb_ctx, depth, new_cache))
        q_mla, k_mla, v_mla, gq, gk, gv, na = outs[:N_QKV_OUT]
        new_cache = outs[N_QKV_OUT:]
        o_a, o_b, o_c = _context_attention(q_mla, k_mla, v_mla, gq, gk, gv, na, seq_ctx)
        xp = _outproj(xp, mods, l, ctx_row, o_a, o_b, o_c, w_out_b)
        xp = _ffn(xp, mods, norm_g3, l, 2, 1, ctx_row, *ffn_w, final_g if last else None)
        xs = _ffn(xs, mods, norm_g3, l, 0, 0, lat_row, *ffn_w)
        q_mla, k_mla, v_mla, gq, gk, gv, na = _qkv(xs, mods, norm_g3, l, lat_row, w_in_p, wts, 256, tables=tables)
        kc_mla, vc_mla = _cache_kv(cache_mla_ckv, kr_cache, l, wts)
        o_a, o_b, o_c = _latent_attention(q_mla, k_mla, v_mla, gq, gk, gv, na, kc_mla, vc_mla, caches, biases[l], l,
                                          seq_lat)
        xs = _outproj(xs, mods, l, lat_row, o_a, o_b, o_c, w_out_b)
        xs = _ffn(xs, mods, norm_g3, l, 2, 1, lat_row, *ffn_w, final_g if last else None)

    y_prompt = xp.reshape(nb_ctx, seq_ctx, d)
    y_sample = xs.reshape(nb_lat, seq_lat, d)
    ckv_c, kr_c, nak_c, nav_c, gk_c, gv_c = new_cache
    na_shape = (nb_ctx, depth, seq_ctx, NA_HEADS, HEAD_DIM)
    gqa_shape = (nb_ctx, depth, seq_ctx, GQA_KV_HEADS, HEAD_DIM)
    return (y_prompt, y_sample, ckv_c, kr_c, nak_c.reshape(na_shape), nav_c.reshape(na_shape),
            gk_c.reshape(gqa_shape), gv_c.reshape(gqa_shape))
```
